```python
import math, functools
import jax, jax.numpy as jnp
from jax import lax
import numpy as np

D_MODEL = 1024
BATCH = 4
SEQ = 4096
DEPTH = 2
DEC_BATCH = 128
DEC_SEQ = 1
PAST_LEN = 2048
PAGE_SIZE = 128

GROUP_W = D_MODEL // 4
HEAD_DIM = 64
N_HEADS_A = GROUP_W // HEAD_DIM
N_KV_HEADS = N_HEADS_A
N_IDX_HEADS = 8
IDX_DIM = 32
TOPK_MAX = 256
Q_BLOCK = 128
ROPE_THETA = 500000.0
B_W = GROUP_W
B_CONV = 3
C_W = GROUP_W
C_CONV = 31
D_HEAD_DIM = 64
D_HEADS = GROUP_W // D_HEAD_DIM
D_W = D_HEADS * D_HEAD_DIM
D_GROUPS = 2
D_STATE = 128
D_CONV = 4
D_XBC = D_W + 2 * D_GROUPS * D_STATE
SSD_CHUNK = 128
A_W = N_HEADS_A * HEAD_DIM
KV_W = N_KV_HEADS * HEAD_DIM
MIX_W = A_W + B_W + C_W + D_W
IN_SIZES = (A_W, KV_W, KV_W, N_IDX_HEADS * IDX_DIM, IDX_DIM, N_IDX_HEADS, B_W, B_W, B_W, 2 * C_W, D_W, D_XBC, D_HEADS)
IN_W = sum(IN_SIZES)
D_FF = -(-8 * D_MODEL // (3 * 256)) * 256
RMS_EPS = 1e-6
LN_EPS = 1e-5

kernel_name = 'hybrid_dsa_conv_ssd_decoder_step'


def rms_norm(x, g):
    xf = x.astype(jnp.float32)
    y = xf * lax.rsqrt(jnp.mean(xf * xf, axis=-1, keepdims=True) + RMS_EPS)
    return (y * g.astype(jnp.float32)).astype(x.dtype)


def layer_norm(x, g, b):
    xf = x.astype(jnp.float32)
    mu = jnp.mean(xf, axis=-1, keepdims=True)
    var = jnp.mean(jnp.square(xf - mu), axis=-1, keepdims=True)
    return ((xf - mu) * lax.rsqrt(var + LN_EPS) * g.astype(jnp.float32) + b.astype(jnp.float32)).astype(x.dtype)


def rope(x, pos):
    rot = x.shape[-1] // 4
    half = rot // 2
    inv = ROPE_THETA ** (-jnp.arange(half, dtype=jnp.float32) * 2.0 / rot)
    ang = pos.astype(jnp.float32)[:, None] * inv[None, :]
    cos = jnp.cos(ang)[None, :, None, :]
    sin = jnp.sin(ang)[None, :, None, :]
    xr = x[..., :rot].astype(jnp.float32)
    x1, x2 = xr[..., :half], xr[..., half:]
    out = jnp.concatenate([x1 * cos - x2 * sin, x2 * cos + x1 * sin], axis=-1).astype(x.dtype)
    return jnp.concatenate([out, x[..., rot:]], axis=-1)


def causal_dwconv(u, ctx, w, b=None):
    ext = jnp.concatenate([ctx.astype(u.dtype), u], axis=1)
    y = lax.conv_general_dilated(ext, w[:, None, :].astype(u.dtype), window_strides=(1,), padding='VALID',
                                 dimension_numbers=('NWC', 'WIO', 'NWC'), feature_group_count=u.shape[-1])
    if b is not None:
        y = y + b.astype(u.dtype)
    return y, ext[:, ext.shape[1] - (w.shape[0] - 1):]


def select_keys(qi, wi, ki, qpos, topk):
    s = jnp.einsum('bqhd,bsd->bqhs', qi.astype(jnp.float32), ki.astype(jnp.float32)) * IDX_DIM ** -0.5
    score = jnp.einsum('bqhs,bqh->bqs', jax.nn.relu(s), wi.astype(jnp.float32) * N_IDX_HEADS ** -0.5)
    kpos = jnp.arange(ki.shape[1])
    score = jnp.where(kpos[None, None, :] <= qpos[None, :, None], score, -jnp.inf)
    _, idx = lax.top_k(score, topk)
    return idx, idx <= qpos[None, :, None]


def attend(q, kg, vg, valid):
    logits = jnp.einsum('bqhd,bqkhd->bqhk', q.astype(jnp.float32), kg.astype(jnp.float32)) * HEAD_DIM ** -0.5
    logits = jnp.where(valid[:, :, None, :], logits, -jnp.inf)
    p = jax.nn.softmax(logits, axis=-1)
    return jnp.einsum('bqhk,bqkhd->bqhd', p, vg.astype(jnp.float32)).astype(q.dtype)


def dsa_prompt(q, k, v, qi, ki, wi):
    bsz, s_len = q.shape[:2]
    topk = min(TOPK_MAX, s_len // 4)
    nb = s_len // Q_BLOCK
    bidx = jnp.arange(bsz)[:, None, None]

    def block(args):
        i, qb, qib, wib = args
        qpos = i * Q_BLOCK + jnp.arange(Q_BLOCK)
        idx, valid = select_keys(qib, wib, ki, qpos, topk)
        return attend(qb, k[bidx, idx], v[bidx, idx], valid)

    split = lambda t: jnp.swapaxes(t.reshape(bsz, nb, Q_BLOCK, *t.shape[2:]), 0, 1)
    out = lax.map(block, (jnp.arange(nb), split(q), split(qi), split(wi)))
    return jnp.swapaxes(out, 0, 1).reshape(bsz, s_len, A_W)


def dsa_sample(q, k, v, qi, ki, wi, ck, cv, cki, page_table):
    n_dec, t_dec = q.shape[:2]
    past = page_table.shape[1] * PAGE_SIZE
    ki_all = jnp.concatenate([cki[page_table].reshape(n_dec, past, IDX_DIM), ki], axis=1)
    topk = min(TOPK_MAX, (past + t_dec) // 4)
    qpos = past + jnp.arange(t_dec)
    idx, valid = select_keys(qi, wi, ki_all, qpos, topk)
    bidx = jnp.arange(n_dec)[:, None, None]
    pidx = jnp.minimum(idx, past - 1)
    phys = page_table[bidx, pidx // PAGE_SIZE]
    off = pidx % PAGE_SIZE
    nidx = jnp.clip(idx - past, 0, t_dec - 1)
    is_past = (idx < past)[..., None, None]
    kg = jnp.where(is_past, ck[phys, off].astype(k.dtype), k[bidx, nidx])
    vg = jnp.where(is_past, cv[phys, off].astype(v.dtype), v[bidx, nidx])
    return attend(q, kg, vg, valid).reshape(n_dec, t_dec, A_W)


def ssd_chunked(x, dt, a, bm, cm, h0, chunk):
    bsz, s_len, n_h, p_dim = x.shape
    rep = n_h // bm.shape[2]
    bh = jnp.repeat(bm, rep, axis=2)
    ch = jnp.repeat(cm, rep, axis=2)
    nc = s_len // chunk
    blk = lambda t: jnp.moveaxis(t.reshape(bsz, nc, chunk, *t.shape[2:]), 1, 0)
    causal = jnp.tril(jnp.ones((chunk, chunk), bool))[None, :, :, None]

    def step(h, inp):
        xc, dtc, bc, cc = inp
        cs = jnp.cumsum(dtc * a, axis=1)
        seg = cs[:, :, None, :] - cs[:, None, :, :]
        decay = jnp.where(causal, jnp.exp(jnp.where(causal, seg, 0.0)), 0.0)
        cb = jnp.einsum('bthn,bshn->btsh', cc, bc)
        y = jnp.einsum('btsh,bshp->bthp', cb * decay * dtc[:, None], xc)
        y = y + jnp.einsum('bthn,bhpn->bthp', cc, h) * jnp.exp(cs)[..., None]
        w_end = jnp.exp(cs[:, -1:] - cs) * dtc
        h = h * jnp.exp(cs[:, -1])[:, :, None, None] + jnp.einsum('bsh,bshn,bshp->bhpn', w_end, bc, xc)
        return h, y

    h, ys = lax.scan(step, h0.astype(jnp.float32), (blk(x), blk(dt), blk(bh), blk(ch)))
    return jnp.moveaxis(ys, 0, 1).reshape(bsz, s_len, n_h, p_dim), h


def layer(x, pos, attn_fn, ctx_b, ctx_c, ctx_d, h0, p):
    bsz, s_len, _ = x.shape
    f32 = jnp.float32
    u = rms_norm(x, p['g_pre_mix'])
    cuts = [int(c) for c in np.cumsum(IN_SIZES)[:-1]]
    (q, k, v, qi, ki, wi, bx, gb, gc, glu, zg, xbc, dt) = jnp.split(u @ p['w_in'], cuts, axis=-1)
    heads = lambda t, n: t.reshape(bsz, s_len, n, -1)
    q = rope(heads(q, N_HEADS_A), pos)
    k = rope(heads(k, N_KV_HEADS), pos)
    v = heads(v, N_KV_HEADS)
    qi = rope(heads(qi, N_IDX_HEADS), pos)
    ki = rope(ki[:, :, None, :], pos)[:, :, 0]
    y_a = attn_fn(q, k, v, qi, ki, wi)
    conv_b, new_b = causal_dwconv(gc * bx, ctx_b, p['bconv_w'])
    y_b = gb * conv_b
    ga, gg = jnp.split(glu, 2, axis=-1)
    conv_c, new_c = causal_dwconv(ga * jax.nn.sigmoid(gg), ctx_c, p['cconv_w'], p['cconv_b'])
    y_c = jax.nn.silu(layer_norm(conv_c, p['cln_g'], p['cln_b']))
    conv_d, new_d = causal_dwconv(xbc, ctx_d, p['dconv_w'], p['dconv_b'])
    xs, bm, cm = jnp.split(jax.nn.silu(conv_d).astype(f32), [D_W, D_W + D_GROUPS * D_STATE], axis=-1)
    dtv = jax.nn.softplus(dt.astype(f32) + p['dt_bias'].astype(f32))
    a = -jnp.exp(p['a_log'].astype(f32))
    chunk = SSD_CHUNK if s_len % SSD_CHUNK == 0 else s_len
    xs4 = xs.reshape(bsz, s_len, D_HEADS, D_HEAD_DIM)
    ys, h_new = ssd_chunked(xs4, dtv, a, bm.reshape(bsz, s_len, D_GROUPS, D_STATE),
                            cm.reshape(bsz, s_len, D_GROUPS, D_STATE), h0, chunk)
    ys = (ys + p['d_skip'].astype(f32)[:, None] * xs4).reshape(bsz, s_len, D_W)
    yg = ys * jax.nn.silu(zg.astype(f32))
    y_d = (yg * lax.rsqrt(jnp.mean(yg * yg, axis=-1, keepdims=True) + RMS_EPS) * p['d_norm'].astype(f32)).astype(x.dtype)
    mix = jnp.concatenate([y_a, y_b, y_c, y_d], axis=-1) @ p['w_out']
    x = x + rms_norm(mix, p['g_post_mix'])
    hf = rms_norm(x, p['g_pre_ffn'])
    f = (jax.nn.silu(hf @ p['ffn_gate']) * (hf @ p['ffn_up'])) @ p['ffn_down']
    x = x + rms_norm(f, p['g_post_ffn'])
    return x, (k, v, ki, new_b, new_c, new_d, h_new.astype(x.dtype))


def setup_inputs(seed: int = 0) -> dict:
    key = jax.random.key(seed)
    ks = iter(jax.random.split(key, 40))
    nrm = lambda shape, scale: jax.random.normal(next(ks), shape, jnp.float32) * scale
    n_pages = PAST_LEN // PAGE_SIZE
    used = DEC_BATCH * n_pages
    n_phys = used + max(1, used // 4)
    page_table = jax.random.permutation(next(ks), n_phys)[:used].reshape(DEC_BATCH, n_pages).astype(jnp.int32)
    dt0 = jnp.exp(jax.random.uniform(next(ks), (DEPTH, D_HEADS), jnp.float32, math.log(1e-3), math.log(1e-1)))
    a_log = jnp.log(jax.random.uniform(next(ks), (DEPTH, D_HEADS), jnp.float32, 1.0, 16.0))
    return {
        'x_prompt': nrm((BATCH, SEQ, D_MODEL), 1.0),
        'x_sample': nrm((DEC_BATCH, DEC_SEQ, D_MODEL), 1.0),
        'cache_k': nrm((DEPTH, n_phys, PAGE_SIZE, N_KV_HEADS, HEAD_DIM), 1.0),
        'cache_v': nrm((DEPTH, n_phys, PAGE_SIZE, N_KV_HEADS, HEAD_DIM), 1.0),
        'cache_kidx': nrm((DEPTH, n_phys, PAGE_SIZE, IDX_DIM), 1.0),
        'page_table': page_table,
        'state_bconv': nrm((DEPTH, DEC_BATCH, B_CONV - 1, B_W), 1.0),
        'state_cconv': nrm((DEPTH, DEC_BATCH, C_CONV - 1, C_W), 0.5),
        'state_dconv': nrm((DEPTH, DEC_BATCH, D_CONV - 1, D_XBC), 1.0),
        'state_ssm': nrm((DEPTH, DEC_BATCH, D_HEADS, D_HEAD_DIM, D_STATE), 0.1),
        'w_in': nrm((DEPTH, D_MODEL, IN_W), D_MODEL ** -0.5),
        'w_out': nrm((DEPTH, MIX_W, D_MODEL), MIX_W ** -0.5),
        'g_pre_mix': 1.0 + nrm((DEPTH, D_MODEL), 0.02),
        'g_post_mix': 1.0 + nrm((DEPTH, D_MODEL), 0.02),
        'g_pre_ffn': 1.0 + nrm((DEPTH, D_MODEL), 0.02),
        'g_post_ffn': 1.0 + nrm((DEPTH, D_MODEL), 0.02),
        'bconv_w': nrm((DEPTH, B_CONV, B_W), B_CONV ** -0.5),
        'cconv_w': nrm((DEPTH, C_CONV, C_W), C_CONV ** -0.5),
        'cconv_b': nrm((DEPTH, C_W), 0.02),
        'cln_g': 1.0 + nrm((DEPTH, C_W), 0.02),
        'cln_b': nrm((DEPTH, C_W), 0.02),
        'dconv_w': nrm((DEPTH, D_CONV, D_XBC), D_CONV ** -0.5),
        'dconv_b': nrm((DEPTH, D_XBC), 0.02),
        'dt_bias': dt0 + jnp.log(-jnp.expm1(-dt0)),
        'a_log': a_log,
        'd_skip': 1.0 + nrm((DEPTH, D_HEADS), 0.1),
        'd_norm': 1.0 + nrm((DEPTH, D_W), 0.02),
        'ffn_gate': nrm((DEPTH, D_MODEL, D_FF), D_MODEL ** -0.5),
        'ffn_up': nrm((DEPTH, D_MODEL, D_FF), D_MODEL ** -0.5),
        'ffn_down': nrm((DEPTH, D_FF, D_MODEL), D_FF ** -0.5),
    }


def reference(x_prompt, x_sample, cache_k, cache_v, cache_kidx, page_table, state_bconv, state_cconv, state_dconv,
              state_ssm, w_in, w_out, g_pre_mix, g_post_mix, g_pre_ffn, g_post_ffn, bconv_w, cconv_w, cconv_b,
              cln_g, cln_b, dconv_w, dconv_b, dt_bias, a_log, d_skip, d_norm, ffn_gate, ffn_up, ffn_down):
    n_prompt, seq = x_prompt.shape[:2]
    t_dec = x_sample.shape[1]
    past = page_table.shape[1] * PAGE_SIZE
    pos_p = jnp.arange(seq, dtype=jnp.int32)
    pos_s = past + jnp.arange(t_dec, dtype=jnp.int32)
    dtype = x_prompt.dtype
    zb = jnp.zeros((n_prompt, B_CONV - 1, B_W), dtype)
    zc = jnp.zeros((n_prompt, C_CONV - 1, C_W), dtype)
    zd = jnp.zeros((n_prompt, D_CONV - 1, D_XBC), dtype)
    zh = jnp.zeros((n_prompt, D_HEADS, D_HEAD_DIM, D_STATE), jnp.float32)
    hp, hs = x_prompt, x_sample
    outs_p = [[] for _ in range(7)]
    outs_s = [[] for _ in range(7)]
    for l in range(DEPTH):
        p = {'w_in': w_in[l], 'w_out': w_out[l], 'g_pre_mix': g_pre_mix[l], 'g_post_mix': g_post_mix[l],
             'g_pre_ffn': g_pre_ffn[l], 'g_post_ffn': g_post_ffn[l], 'bconv_w': bconv_w[l], 'cconv_w': cconv_w[l],
             'cconv_b': cconv_b[l], 'cln_g': cln_g[l], 'cln_b': cln_b[l], 'dconv_w': dconv_w[l],
             'dconv_b': dconv_b[l], 'dt_bias': dt_bias[l], 'a_log': a_log[l], 'd_skip': d_skip[l],
             'd_norm': d_norm[l], 'ffn_gate': ffn_gate[l], 'ffn_up': ffn_up[l], 'ffn_down': ffn_down[l]}
        hp, st_p = layer(hp, pos_p, dsa_prompt, zb, zc, zd, zh, p)
        attn_s = functools.partial(dsa_sample, ck=cache_k[l], cv=cache_v[l], cki=cache_kidx[l], page_table=page_table)
        hs, st_s = layer(hs, pos_s, attn_s, state_bconv[l], state_cconv[l], state_dconv[l], state_ssm[l], p)
        for lst, arr in zip(outs_p, st_p):
            lst.append(arr)
        for lst, arr in zip(outs_s, st_s):
            lst.append(arr)
    k_p, v_p, kidx_p, bconv_p, cconv_p, dconv_p, ssm_p = [jnp.stack(a) for a in outs_p]
    k_s, v_s, kidx_s, bconv_s, cconv_s, dconv_s, ssm_s = [jnp.stack(a) for a in outs_s]
    return (hp, hs, k_p, v_p, kidx_p, k_s, v_s, kidx_s, bconv_p, bconv_s, cconv_p, cconv_s, dconv_p, dconv_s, ssm_p, ssm_s)
```

```python
import functools
import math

import jax
import jax.numpy as jnp
import numpy as np
from jax import lax
from jax.experimental import pallas as pl
from jax.experimental.pallas import tpu as pltpu

f32, bf16, i32 = jnp.float32, jnp.bfloat16, jnp.int32
HI = lax.Precision.HIGHEST

D_MODEL = 1024
PAGE = 128
GW = 256
HEAD_DIM = 64
N_HEADS = 4
N_IDX_HEADS = 8
IDX_DIM = 32
TOPK_MAX = 256
ROPE_THETA = 500000.0
C_CONV = 31
B_CONV = 3
D_CONV = 4
D_XBC = 768
D_STATE = 128
SSD_CHUNK = 128
D_FF = 2816
RMS_EPS = 1e-6
LN_EPS = 1e-5
IN_SIZES = (256, 256, 256, 256, 32, 8, 256, 256, 256, 512, 256, 768, 4)
INT_MIN = -(2 ** 31)
KEY_NEG_INF = INT_MIN + 0x7FFFFF
NEG = -1e30
VMEM_LIMIT = 56 * 1024 * 1024


def _dot_nt(a, b, prec=None):
    return lax.dot_general(a, b, (((1,), (1,)), ((), ())), precision=prec, preferred_element_type=f32)


def _dot(a, b, prec=None):
    return jnp.dot(a, b, precision=prec, preferred_element_type=f32)


def _cparams(sem):
    return pltpu.CompilerParams(dimension_semantics=sem, vmem_limit_bytes=VMEM_LIMIT)


def _rms(x, g):
    return x * lax.rsqrt(jnp.mean(x * x, axis=-1, keepdims=True) + RMS_EPS) * g


def _silu(x):
    return x * jax.nn.sigmoid(x)


def _key_to_float(key):
    return lax.bitcast_convert_type(jnp.where(key < 0, key ^ jnp.int32(0x7FFFFFFF), key), f32)


def _split_f32(x):
    c = x * (2.0 ** 16 + 1.0)
    hi = c - (c - x)
    return hi, x - hi


def _split(x):
    hi, lo = _split_f32(x)
    return hi.astype(bf16), lo.astype(bf16)


def _kth_largest(count_ge, shape, topk):
    def bit_step(bi, key):
        cand = key + lax.shift_left(jnp.int32(1), 31 - bi)
        ok = (cand <= KEY_NEG_INF) | (count_ge(_key_to_float(cand)) >= topk)
        return jnp.where(ok, cand, key)

    key = lax.fori_loop(0, 32, bit_step, jnp.full(shape, INT_MIN, i32))
    return key, _key_to_float(key)


def _refine_between_floats(count_ge, lo, key, rows, topk, steps=40):
    hi = _key_to_float(key + 1)

    def step(_, carry):
        lo, hi = carry
        mid = lo + (hi - lo) * 0.5
        ge = count_ge(mid) >= topk
        return jnp.where(rows & ge, mid, lo), jnp.where(rows & jnp.logical_not(ge), mid, hi)

    return lax.fori_loop(0, steps, step, (lo, hi))[0]


def _rope(v, c, sm, sp, half):
    outs = []
    for s in range(v.shape[1] // 128):
        xs = v[:, s * 128:(s + 1) * 128]
        outs.append(xs * c + pltpu.roll(xs, 128 - half, 1) * sm + pltpu.roll(xs, half, 1) * sp)
    return jnp.concatenate(outs, axis=1)


def _inproj_kernel(x_ref, g_ref, wm_ref, wih_ref, wil_ref, c64_ref, sm64_ref, sp64_ref, c32_ref, sm32_ref, sp32_ref,
                   q_ref, k_ref, kb_ref, v_ref, vb_ref, qi_ref, ki_ref, kcat_ref, wi_o_ref, dt_o_ref,
                   bgc_ref, glu_ref, zg_ref, xbc_ref):
    u = _rms(x_ref[...], g_ref[...])
    ub, ul = _split(u)

    def mm(c0, c1):
        return _dot(ub, wm_ref[:, c0:c1])

    def mm3(c0, c1):
        wh = wih_ref[:, c0:c1]
        return _dot(ub, wh) + _dot(ul, wh) + _dot(ub, wil_ref[:, c0:c1])

    c64, sm64, sp64 = c64_ref[...], sm64_ref[...], sp64_ref[...]
    q_ref[...] = _rope(mm(0, 256), c64, sm64, sp64, 8)
    k = _rope(mm(256, 512), c64, sm64, sp64, 8)
    k_ref[...] = k
    kb_ref[...] = k.astype(bf16)
    v = mm(512, 768)
    v_ref[...] = v
    vb_ref[...] = v.astype(bf16)
    bgc_ref[...] = mm(768, 1536)
    glu_ref[...] = mm(1536, 2048)
    zg_ref[...] = mm(2048, 2304)
    xbc_ref[...] = mm(2304, 3072)
    c32, sm32, sp32 = c32_ref[...], sm32_ref[...], sp32_ref[...]
    qi_ref[...] = _rope(mm3(0, 256), c32, sm32, sp32, 4)
    ki4 = _rope(mm3(256, 384), c32, sm32, sp32, 4)
    ki_ref[...] = ki4
    kh, kl = _split(ki4)
    lane = lax.broadcasted_iota(i32, ki4.shape, 1)
    kcat_ref[...] = jnp.where((lane >= IDX_DIM) & (lane < 2 * IDX_DIM), kl, kh)
    small = mm3(384, 512)
    wi_o_ref[...] = small
    dt_o_ref[...] = pltpu.roll(small, 128 - N_IDX_HEADS, 1)


def _inproj(x, g, wm, wih, wil, tabs, tm):
    t = x.shape[0]
    nt = t // tm
    ntab = tabs[0].shape[0] // tm
    row = lambda w: pl.BlockSpec((tm, w), lambda i: (i, 0))
    full = lambda a: pl.BlockSpec(a.shape, lambda i: (0,) * a.ndim)
    tab = pl.BlockSpec((tm, 128), lambda i: (i % ntab, 0))
    widths = (256, 256, 256, 256, 256, 256, 128, 128, 128, 128, 768, 512, 256, 768)
    dtypes = (f32, f32, bf16, f32, bf16, f32, f32, bf16, f32, f32, f32, f32, f32, f32)
    return pl.pallas_call(
        _inproj_kernel,
        grid=(nt,),
        in_specs=[row(D_MODEL), full(g), full(wm), full(wih), full(wil)] + [tab] * 6,
        out_specs=[row(w) for w in widths],
        out_shape=[jax.ShapeDtypeStruct((t, w), d) for w, d in zip(widths, dtypes)],
        compiler_params=_cparams(("arbitrary",)),
        name="inproj",
    )(x, g, wm, wih, wil, *tabs)


QB = 256
KC = 512


def _attn_kernel(q_ref, qi_ref, wi_ref, kcat_ref, k_ref, v_ref, o_ref,
                 sc_ref, bias_ref, qcat_ref, qm_ref, acc_ref, *, topk):
    i = pl.program_id(1)
    nch = (i * QB + QB + KC - 1) // KC
    lane = lax.broadcasted_iota(i32, (QB, 256), 1)
    qi_hi, qi_lo = _split_f32(qi_ref[...])
    l128 = lax.broadcasted_iota(i32, (QB, 128), 1)
    for h in range(N_IDX_HEADS):
        sl = slice((h // 4) * 128, (h // 4 + 1) * 128)
        o = (h % 4) * IDX_DIM
        shift = lambda x, d: x if (d - o) % 128 == 0 else pltpu.roll(x, (d - o) % 128, 1)
        cat = jnp.where(l128 < IDX_DIM, shift(qi_hi[:, sl], 0),
                        jnp.where(l128 < 2 * IDX_DIM, shift(qi_hi[:, sl], IDX_DIM),
                                  jnp.where(l128 < 3 * IDX_DIM, shift(qi_lo[:, sl], 2 * IDX_DIM), 0.0)))
        qcat_ref[h] = cat.astype(bf16)
    q = q_ref[...] * (HEAD_DIM ** -0.5)
    for h in range(N_HEADS):
        qm_ref[h] = jnp.where(lane // HEAD_DIM == h, q, 0.0).astype(bf16)
    wi = wi_ref[...] * (IDX_DIM ** -0.5 * N_IDX_HEADS ** -0.5)
    wcol = [wi[:, h:h + 1] for h in range(N_IDX_HEADS)]
    qpos = i * QB + lax.broadcasted_iota(i32, (QB, 1), 0)
    kiota = lax.broadcasted_iota(i32, (1, KC), 1)

    def score_chunk(c, carry):
        off = pl.multiple_of(c * KC, KC)
        kc = kcat_ref[pl.ds(off, KC), :]
        acc = jnp.zeros((QB, KC), f32)
        for h in range(N_IDX_HEADS):
            acc = acc + jnp.maximum(_dot_nt(qcat_ref[h], kc), 0.0) * wcol[h]
        sc_ref[:, pl.ds(off, KC)] = jnp.where(off + kiota <= qpos, acc, -jnp.inf)
        return carry

    lax.fori_loop(0, nch, score_chunk, 0)

    ones = jnp.ones((KC, 128), bf16)
    wide = lambda t: jnp.concatenate([t] * (KC // 128), axis=1)

    def count(pred):
        def body(c, cnt):
            s = sc_ref[:, pl.ds(pl.multiple_of(c * KC, KC), KC)]
            return cnt + _dot(jnp.where(pred(s), 1.0, 0.0).astype(bf16), ones)
        return lax.fori_loop(0, nch, body, jnp.zeros((QB, 128), f32))

    def count_ge(t):
        tw = wide(t)
        return count(lambda s: s >= tw)

    thr_key, thr = _kth_largest(count_ge, (QB, 128), topk)
    real = thr > -jnp.inf
    thr_fin = wide(jnp.maximum(thr, jnp.finfo(f32).min))

    def bias_chunk(c, carry):
        sl = pl.ds(pl.multiple_of(c * KC, KC), KC)
        bias_ref[:, sl] = jnp.where(sc_ref[:, sl] >= thr_fin, 0.0, NEG)
        return carry

    lax.fori_loop(0, nch, bias_chunk, 0)

    tie = real & (count_ge(thr) > topk)

    @pl.when(jnp.max(jnp.where(tie, 1.0, 0.0)) > 0.0)
    def _():
        thr2 = wide(_refine_between_floats(count_ge, thr, thr_key, tie, topk))
        need = wide(topk - count(lambda s: s > thr2))
        realw = wide(jnp.where(real, 1.0, 0.0))
        r = lax.broadcasted_iota(i32, (KC, KC), 0)
        cidx = lax.broadcasted_iota(i32, (KC, KC), 1)
        tri = jnp.where(r <= cidx, 1.0, 0.0).astype(bf16)

        def tie_chunk(c, carry):
            sl = pl.ds(pl.multiple_of(c * KC, KC), KC)
            s = sc_ref[:, sl]
            eqb = jnp.where(s == thr2, realw, 0.0).astype(bf16)
            cum = _dot(eqb, tri) + wide(carry)
            keep = jnp.where(s > thr2, 1.0, jnp.where(cum <= need, eqb.astype(f32), 0.0))
            bias_ref[:, sl] = jnp.where(keep > 0.0, 0.0, NEG)
            return carry + _dot(eqb, ones)

        lax.fori_loop(0, nch, tie_chunk, jnp.zeros((QB, 128), f32))

    def attn_chunk(c, carry):
        ms, ls = carry
        sl = pl.ds(pl.multiple_of(c * KC, KC), KC)
        kc = k_ref[sl, :]
        vc = v_ref[sl, :]
        bias = bias_ref[:, sl]
        new_m, new_l = [], []
        for h in range(N_HEADS):
            s = _dot_nt(qm_ref[h], kc) + bias
            m_new = jnp.maximum(ms[h], jnp.max(s, axis=-1, keepdims=True))
            alpha = jnp.exp(ms[h] - m_new)
            p = jnp.exp(s - m_new)
            new_l.append(alpha * ls[h] + jnp.sum(p, axis=-1, keepdims=True))
            acc_ref[h] = alpha * acc_ref[h] + _dot(p.astype(bf16), vc)
            new_m.append(m_new)
        return tuple(new_m), tuple(new_l)

    acc_ref[...] = jnp.zeros_like(acc_ref)
    m0 = tuple(jnp.full((QB, 1), NEG, f32) for _ in range(N_HEADS))
    l0 = tuple(jnp.zeros((QB, 1), f32) for _ in range(N_HEADS))
    _, ls = lax.fori_loop(0, nch, attn_chunk, (m0, l0))
    out = jnp.zeros((QB, 256), f32)
    for h in range(N_HEADS):
        out = jnp.where(lane // HEAD_DIM == h, acc_ref[h] / ls[h], out)
    o_ref[...] = out


def _attn_prompt(q, qi, wi, kcat, kb, vb, bsz, seq):
    nq = seq // QB
    topk = min(TOPK_MAX, seq // 4)
    qrow = lambda w: pl.BlockSpec((QB, w), lambda b, i: (b * nq + i, 0))
    seqblk = lambda w: pl.BlockSpec((seq, w), lambda b, i: (b, 0))
    return pl.pallas_call(
        functools.partial(_attn_kernel, topk=float(topk)),
        grid=(bsz, nq),
        in_specs=[qrow(256), qrow(256), qrow(128), seqblk(128), seqblk(256), seqblk(256)],
        out_specs=qrow(256),
        out_shape=jax.ShapeDtypeStruct((bsz * seq, 256), f32),
        scratch_shapes=[pltpu.VMEM((QB, seq), f32), pltpu.VMEM((QB, seq), f32),
                        pltpu.VMEM((N_IDX_HEADS, QB, 128), bf16), pltpu.VMEM((N_HEADS, QB, 256), bf16),
                        pltpu.VMEM((N_HEADS, QB, 256), f32)],
        compiler_params=_cparams(("arbitrary", "arbitrary")),
        name="attn_prompt",
    )(q, qi, wi, kcat, kb, vb)


TC = 256
HALO_B, HALO_C, HALO_D = 8, 32, 8


def _ssd_consts():
    r = lax.broadcasted_iota(i32, (128, 128), 0)
    c = lax.broadcasted_iota(i32, (128, 128), 1)
    tril = jnp.where(c <= r, 1.0, 0.0)
    causal = c <= r
    eh = lax.broadcasted_iota(i32, (128, 256), 0)
    ej = lax.broadcasted_iota(i32, (128, 256), 1)
    expand = jnp.where(ej // 64 == eh, 1.0, 0.0)
    th = lax.broadcasted_iota(i32, (256, 128), 1)
    tj = lax.broadcasted_iota(i32, (256, 128), 0)
    expand_t = jnp.where(tj // 64 == th, 1.0, 0.0)
    return tril, causal, expand, expand_t


def _ssd_chunk(xs, bm, cm, dtv, a_row, hcat_ref):
    tril, causal, expand, expand_t = _ssd_consts()
    lane = lax.broadcasted_iota(i32, (128, 256), 1)
    da = dtv * a_row
    cs = _dot(tril, da, HI)
    cs_t = cs.T
    dt_x = _dot(dtv, expand, HI)
    ecs_x = _dot(jnp.exp(cs), expand, HI)
    cs_last = cs[127:128, :]
    wend_x = _dot(jnp.exp(cs_last - cs) * dtv, expand, HI)
    xdt = (xs * dt_x).astype(bf16)
    bmb, cmb = bm.astype(bf16), cm.astype(bf16)
    hb = hcat_ref[...].astype(bf16)
    y = jnp.zeros((128, 256), f32)
    ystate = []
    for g in range(2):
        cg = cmb[:, g * 128:(g + 1) * 128]
        cb = _dot_nt(cg, bmb[:, g * 128:(g + 1) * 128])
        ystate.append(_dot_nt(cg, hb))
        for h in (2 * g, 2 * g + 1):
            seg = cs[:, h:h + 1] - cs_t[h:h + 1, :]
            dec = jnp.where(causal, jnp.exp(jnp.where(causal, seg, 0.0)), 0.0)
            yh = _dot((cb * dec).astype(bf16), xdt)
            y = jnp.where(lane // 64 == h, yh, y)
    y = y + jnp.where(lane < 128, ystate[0], ystate[1]) * ecs_x
    xw_t = (xs * wend_x).T.astype(bf16)
    upd = jnp.concatenate([_dot(xw_t[0:128], bmb[:, 0:128]), _dot(xw_t[128:256], bmb[:, 128:256])], axis=0)
    dlast = jnp.exp(jnp.broadcast_to(cs_t[:, 127:128], (128, 128)))
    dcol = _dot(expand_t, dlast, HI)
    hcat_ref[...] = hcat_ref[...] * dcol + upd
    return y


def _gated_rms(ys, xs, zg, dskip, dnorm):
    yg = (ys + dskip * xs) * _silu(zg)
    return yg * lax.rsqrt(jnp.mean(yg * yg, axis=-1, keepdims=True) + RMS_EPS) * dnorm


def _layer_norm_silu(x, g, b):
    mu = jnp.mean(x, axis=-1, keepdims=True)
    var = jnp.mean(jnp.square(x - mu), axis=-1, keepdims=True)
    return _silu((x - mu) * lax.rsqrt(var + LN_EPS) * g + b)


def _mix_kernel(bgc_ref, glu_ref, zg_ref, xbc_ref, dt_ref,
                bw_ref, cw_ref, cb_ref, lng_ref, lnb_ref, dw_ref, db_ref, dtb_ref, alog_ref, dskip_ref, dnorm_ref,
                yb_ref, yc_ref, yd_ref, nb_ref, nc_ref, nd_ref, ssm_ref,
                eb_ref, ec_ref, ed_ref, hcat_ref):
    i = pl.program_id(1)

    @pl.when(i == 0)
    def _():
        eb_ref[0:HALO_B, :] = jnp.zeros((HALO_B, 256), f32)
        ec_ref[0:HALO_C, :] = jnp.zeros((HALO_C, 256), f32)
        ed_ref[0:HALO_D, :] = jnp.zeros((HALO_D, D_XBC), f32)
        hcat_ref[...] = jnp.zeros_like(hcat_ref)

    bgc = bgc_ref[...]
    eb_ref[HALO_B:HALO_B + TC, :] = bgc[:, 512:768] * bgc[:, 0:256]
    conv = jnp.zeros((TC, 256), f32)
    for k in range(B_CONV):
        conv = conv + bw_ref[k:k + 1, :] * eb_ref[pl.ds(HALO_B - (B_CONV - 1) + k, TC), :]
    yb_ref[...] = bgc[:, 256:512] * conv
    nb_ref[0] = eb_ref[HALO_B + TC - (B_CONV - 1):HALO_B + TC, :]
    eb_ref[0:HALO_B, :] = eb_ref[TC:TC + HALO_B, :]

    glu = glu_ref[...]
    ec_ref[HALO_C:HALO_C + TC, :] = glu[:, 0:256] * jax.nn.sigmoid(glu[:, 256:512])
    conv = jnp.zeros((TC, 256), f32)
    for k in range(C_CONV):
        conv = conv + cw_ref[k:k + 1, :] * ec_ref[pl.ds(HALO_C - (C_CONV - 1) + k, TC), :]
    yc_ref[...] = _layer_norm_silu(conv + cb_ref[...], lng_ref[...], lnb_ref[...])
    nc_ref[0] = ec_ref[HALO_C + TC - (C_CONV - 1):HALO_C + TC, :]
    ec_ref[0:HALO_C, :] = ec_ref[TC:TC + HALO_C, :]

    ed_ref[HALO_D:HALO_D + TC, :] = xbc_ref[...]
    conv = jnp.zeros((TC, D_XBC), f32)
    for k in range(D_CONV):
        conv = conv + dw_ref[k:k + 1, :] * ed_ref[pl.ds(HALO_D - (D_CONV - 1) + k, TC), :]
    act = _silu(conv + db_ref[...])
    nd_ref[0] = ed_ref[HALO_D + TC - (D_CONV - 1):HALO_D + TC, :]
    ed_ref[0:HALO_D, :] = ed_ref[TC:TC + HALO_D, :]
    l128 = lax.broadcasted_iota(i32, (1, 128), 1)
    a_row = jnp.where(l128 < N_HEADS, -jnp.exp(alog_ref[...]), 0.0)
    dtv = jax.nn.softplus(dt_ref[...] + dtb_ref[...])
    zg = zg_ref[...]
    for j in range(TC // SSD_CHUNK):
        rs = slice(j * SSD_CHUNK, (j + 1) * SSD_CHUNK)
        xs = act[rs, 0:256]
        y = _ssd_chunk(xs, act[rs, 256:512], act[rs, 512:768], dtv[rs], a_row, hcat_ref)
        yd_ref[rs, :] = _gated_rms(y, xs, zg[rs], dskip_ref[...], dnorm_ref[...])
    ssm_ref[0] = hcat_ref[...]


def _mix_prompt(bgc, glu, zg, xbc, dt, prm, bsz, seq):
    nt = seq // TC
    row = lambda w: pl.BlockSpec((TC, w), lambda b, i: (b * nt + i, 0))
    full = lambda a: pl.BlockSpec(a.shape, lambda b, i: (0,) * a.ndim)
    st = lambda r, w: pl.BlockSpec((1, r, w), lambda b, i: (b, 0, 0))
    t = bsz * seq
    return pl.pallas_call(
        _mix_kernel,
        grid=(bsz, nt),
        in_specs=[row(768), row(512), row(256), row(768), row(128)] + [full(a) for a in prm],
        out_specs=[row(256), row(256), row(256), st(2, 256), st(30, 256), st(3, D_XBC), st(256, 128)],
        out_shape=[jax.ShapeDtypeStruct((t, 256), f32)] * 3 + [
            jax.ShapeDtypeStruct((bsz, 2, 256), f32), jax.ShapeDtypeStruct((bsz, 30, 256), f32),
            jax.ShapeDtypeStruct((bsz, 3, D_XBC), f32), jax.ShapeDtypeStruct((bsz, 256, 128), f32)],
        scratch_shapes=[pltpu.VMEM((HALO_B + TC, 256), f32), pltpu.VMEM((HALO_C + TC, 256), f32),
                        pltpu.VMEM((HALO_D + TC, D_XBC), f32), pltpu.VMEM((256, 128), f32)],
        compiler_params=_cparams(("arbitrary", "arbitrary")),
        name="mix_prompt",
    )(bgc, glu, zg, xbc, dt, *prm)


def _outproj_kernel(ya_ref, yb_ref, yc_ref, yd_ref, x_ref, w_ref, gpost_ref, gpre_ref, x1_ref, hf_ref):
    mix = _dot(ya_ref[...].astype(bf16), w_ref[0:256, :])
    mix = mix + _dot(yb_ref[...].astype(bf16), w_ref[256:512, :])
    mix = mix + _dot(yc_ref[...].astype(bf16), w_ref[512:768, :])
    mix = mix + _dot(yd_ref[...].astype(bf16), w_ref[768:1024, :])
    x1 = x_ref[...] + _rms(mix, gpost_ref[...])
    x1_ref[...] = x1
    hf_ref[...] = _rms(x1, gpre_ref[...]).astype(bf16)


def _outproj(ya, yb, yc, yd, x, w, gpost, gpre, tm):
    t = x.shape[0]
    row = lambda w_: pl.BlockSpec((tm, w_), lambda i: (i, 0))
    full = lambda a: pl.BlockSpec(a.shape, lambda i: (0,) * a.ndim)
    return pl.pallas_call(
        _outproj_kernel,
        grid=(t // tm,),
        in_specs=[row(256)] * 4 + [row(D_MODEL), full(w), full(gpost), full(gpre)],
        out_specs=[row(D_MODEL), row(D_MODEL)],
        out_shape=[jax.ShapeDtypeStruct((t, D_MODEL), f32), jax.ShapeDtypeStruct((t, D_MODEL), bf16)],
        compiler_params=_cparams(("arbitrary",)),
        name="outproj",
    )(ya, yb, yc, yd, x, w, gpost, gpre)


FF_CHUNK = 1408


def _ffn_kernel(hf_ref, x1_ref, wg_ref, wu_ref, wd_ref, g_ref, o_ref, acc_ref):
    j = pl.program_id(1)
    hf = hf_ref[...]
    a = _silu(_dot(hf, wg_ref[...])) * _dot(hf, wu_ref[...])
    part = _dot(a.astype(bf16), wd_ref[...])

    @pl.when(j == 0)
    def _():
        acc_ref[...] = part

    @pl.when(j > 0)
    def _():
        acc_ref[...] = acc_ref[...] + part

    @pl.when(j == pl.num_programs(1) - 1)
    def _():
        o_ref[...] = x1_ref[...] + _rms(acc_ref[...], g_ref[...])


def _ffn(hf, x1, wg, wu, wd, g, tm):
    t = x1.shape[0]
    nj = D_FF // FF_CHUNK
    row = lambda: pl.BlockSpec((tm, D_MODEL), lambda i, j: (i, 0))
    return pl.pallas_call(
        _ffn_kernel,
        grid=(t // tm, nj),
        in_specs=[row(), row(), pl.BlockSpec((D_MODEL, FF_CHUNK), lambda i, j: (0, j)),
                  pl.BlockSpec((D_MODEL, FF_CHUNK), lambda i, j: (0, j)),
                  pl.BlockSpec((FF_CHUNK, D_MODEL), lambda i, j: (j, 0)),
                  pl.BlockSpec((1, D_MODEL), lambda i, j: (0, 0))],
        out_specs=row(),
        out_shape=jax.ShapeDtypeStruct((t, D_MODEL), f32),
        scratch_shapes=[pltpu.VMEM((tm, D_MODEL), f32)],
        compiler_params=_cparams(("arbitrary", "arbitrary")),
        name="ffn",
    )(hf, x1, wg, wu, wd, g)


def _page_copies(pt_ref, src_hbm, dst, sem, b, n_pages, base):
    return [pltpu.make_async_copy(src_hbm.at[base + pt_ref[b * n_pages + p]],
                                  dst.at[:, pl.ds(p * PAGE, PAGE)], sem) for p in range(n_pages)]


def _dec_score_kernel(pt_ref, qi_ref, wi_ref, cki_hbm, o_ref, buf, sem, *, n_pages, base):
    b = pl.program_id(0)
    nb = pl.num_programs(0)

    def copies(bb, slot):
        return _page_copies(pt_ref, cki_hbm, buf.at[slot], sem.at[slot], bb, n_pages, base)

    @pl.when(b == 0)
    def _():
        for c in copies(0, 0):
            c.start()

    @pl.when(b + 1 < nb)
    def _():
        for c in copies(b + 1, (b + 1) % 2):
            c.start()

    slot = b % 2
    for c in copies(b, slot):
        c.wait()
    s = _dot(qi_ref[b], buf[slot], HI)
    w = jnp.tile(wi_ref[b], (1, n_pages))
    o_ref[0] = jnp.sum(jnp.maximum(s, 0.0) * w, axis=0, keepdims=True)


def _dec_scores(pt, qi_s, wi_t, cki, n_pages, base):
    nb = qi_s.shape[0]
    past = n_pages * PAGE
    gs = pltpu.PrefetchScalarGridSpec(
        num_scalar_prefetch=1, grid=(nb,),
        in_specs=[pl.BlockSpec(qi_s.shape, lambda b, pt_: (0, 0, 0)), pl.BlockSpec(wi_t.shape, lambda b, pt_: (0, 0, 0)),
                  pl.BlockSpec(memory_space=pl.ANY)],
        out_specs=pl.BlockSpec((1, 1, past), lambda b, pt_: (b, 0, 0)),
        scratch_shapes=[pltpu.VMEM((2, IDX_DIM, past), f32), pltpu.SemaphoreType.DMA((2,))])
    return pl.pallas_call(
        functools.partial(_dec_score_kernel, n_pages=n_pages, base=base),
        grid_spec=gs, out_shape=jax.ShapeDtypeStruct((nb, 1, past), f32),
        compiler_params=_cparams(("arbitrary",)), name="dec_scores",
    )(pt, qi_s, wi_t, cki)


def _dec_select_kernel(sc_ref, qi_ref, ki_ref, wi_ref, bias_ref, bnew_ref, *, topk):
    nb, past = sc_ref.shape
    wi = wi_ref[...] * (IDX_DIM ** -0.5 * N_IDX_HEADS ** -0.5)
    gj = lax.broadcasted_iota(i32, (256, 128), 0)
    gh = lax.broadcasted_iota(i32, (256, 128), 1)
    seg = jnp.where(gj // IDX_DIM == gh, 1.0, 0.0)
    ki = jnp.concatenate([ki_ref[...], ki_ref[...]], axis=1)
    s_new = _dot(qi_ref[...] * ki, seg, HI)
    sc_new = jnp.broadcast_to(jnp.sum(jnp.maximum(s_new, 0.0) * wi, axis=-1, keepdims=True), (nb, 128))
    sc = sc_ref[...]
    ones = jnp.ones((past, 128), bf16)
    wide = lambda t: jnp.concatenate([t] * (past // 128), axis=1)

    def count(pred_past, pred_new):
        return _dot(jnp.where(pred_past, 1.0, 0.0).astype(bf16), ones) + jnp.where(pred_new, 1.0, 0.0)

    def count_ge(t):
        return count(sc >= wide(t), sc_new >= t)

    thr_key, thr = _kth_largest(count_ge, (nb, 128), topk)
    bias_ref[...] = jnp.where(sc >= wide(thr), 0.0, NEG)
    bnew_ref[...] = jnp.where(sc_new >= thr, 0.0, NEG)
    tie = count_ge(thr) > topk

    @pl.when(jnp.max(jnp.where(tie, 1.0, 0.0)) > 0.0)
    def _():
        thr2 = _refine_between_floats(count_ge, thr, thr_key, tie, topk)
        need = topk - count(sc > wide(thr2), sc_new > thr2)
        eqb = jnp.where(sc == wide(thr2), 1.0, 0.0).astype(bf16)
        r = lax.broadcasted_iota(i32, (PAGE, PAGE), 0)
        c = lax.broadcasted_iota(i32, (PAGE, PAGE), 1)
        tri = jnp.where(r <= c, 1.0, 0.0).astype(bf16)
        carry = jnp.zeros((nb, 128), f32)
        for p in range(past // PAGE):
            sl = slice(p * PAGE, (p + 1) * PAGE)
            cum = _dot(eqb[:, sl], tri) + carry
            keep = jnp.where(sc[:, sl] > thr2, 1.0, jnp.where(cum <= need, eqb[:, sl].astype(f32), 0.0))
            bias_ref[:, sl] = jnp.where(keep > 0.0, 0.0, NEG)
            carry = carry + _dot(eqb[:, sl], ones[0:PAGE])
        keep_new = jnp.where(sc_new > thr2, 1.0, jnp.where((sc_new == thr2) & (carry + 1.0 <= need), 1.0, 0.0))
        bnew_ref[...] = jnp.where(keep_new > 0.0, 0.0, NEG)


def _dec_select(sc, qi, ki4, wi, topk):
    nb, past = sc.shape
    return pl.pallas_call(
        functools.partial(_dec_select_kernel, topk=float(topk)),
        out_shape=[jax.ShapeDtypeStruct((nb, past), f32), jax.ShapeDtypeStruct((nb, 128), f32)],
        compiler_params=pltpu.CompilerParams(vmem_limit_bytes=VMEM_LIMIT), name="dec_select",
    )(sc, qi, ki4, wi)


def _dec_attn_kernel(pt_ref, qm_ref, kn_ref, vn_ref, bias_ref, bnew_ref, ck_hbm, cv_hbm, o_ref,
                     kbuf, vbuf, sem, *, n_pages, base):
    b = pl.program_id(0)
    nb = pl.num_programs(0)

    def copies(bb, slot):
        return (_page_copies(pt_ref, ck_hbm, kbuf.at[slot], sem.at[0, slot], bb, n_pages, base)
                + _page_copies(pt_ref, cv_hbm, vbuf.at[slot], sem.at[1, slot], bb, n_pages, base))

    @pl.when(b == 0)
    def _():
        for c in copies(0, 0):
            c.start()

    @pl.when(b + 1 < nb)
    def _():
        for c in copies(b + 1, (b + 1) % 2):
            c.start()

    slot = b % 2
    for c in copies(b, slot):
        c.wait()
    qm = qm_ref[0] * (HEAD_DIM ** -0.5)
    s = _dot(qm.astype(bf16), kbuf[slot].astype(bf16)) + bias_ref[0]
    s_new = jnp.sum(qm * kn_ref[0], axis=-1, keepdims=True) + bnew_ref[0][:, 0:1]
    m = jnp.maximum(jnp.max(s, axis=-1, keepdims=True), s_new)
    p = jnp.exp(s - m)
    p_new = jnp.exp(s_new - m)
    den = jnp.sum(p, axis=-1, keepdims=True) + p_new
    out8 = (_dot_nt(p.astype(bf16), vbuf[slot].astype(bf16)) + p_new * vn_ref[0]) / den
    row = lax.broadcasted_iota(i32, (8, 256), 0)
    lane = lax.broadcasted_iota(i32, (8, 256), 1)
    o_ref[0] = jnp.sum(jnp.where(lane // HEAD_DIM == row, out8, 0.0), axis=0, keepdims=True)


def _dec_attn(pt, qm, kn, vn, bias, bnew, ck, cv, n_pages, base):
    nb = qm.shape[0]
    past = n_pages * PAGE
    per = lambda r, w: pl.BlockSpec((1, r, w), lambda b, pt_: (b, 0, 0))
    gs = pltpu.PrefetchScalarGridSpec(
        num_scalar_prefetch=1, grid=(nb,),
        in_specs=[per(8, 256), per(1, 256), per(1, 256), per(1, past), per(1, 128),
                  pl.BlockSpec(memory_space=pl.ANY), pl.BlockSpec(memory_space=pl.ANY)],
        out_specs=per(1, 256),
        scratch_shapes=[pltpu.VMEM((2, 256, past), f32), pltpu.VMEM((2, 256, past), f32),
                        pltpu.SemaphoreType.DMA((2, 2))])
    return pl.pallas_call(
        functools.partial(_dec_attn_kernel, n_pages=n_pages, base=base),
        grid_spec=gs, out_shape=jax.ShapeDtypeStruct((nb, 1, 256), f32),
        compiler_params=_cparams(("arbitrary",)), name="dec_attn",
    )(pt, qm, kn, vn, bias, bnew, ck, cv)


def _dec_conv_kernel(bgc_ref, glu_ref, xbc_ref, dt_ref, sb_ref, sc_ref, sd_ref,
                     bw_ref, cw_ref, cb_ref, lng_ref, lnb_ref, dw_ref, db_ref, dtb_ref,
                     yb_ref, yc_ref, act_ref, dtv_ref, nb_ref, nc_ref, nd_ref):
    bgc = bgc_ref[...]
    ub = bgc[:, 512:768] * bgc[:, 0:256]
    sb = sb_ref[...]
    conv = bw_ref[0:1, :] * sb[:, 0:256] + bw_ref[1:2, :] * sb[:, 256:512] + bw_ref[2:3, :] * ub
    yb_ref[...] = bgc[:, 256:512] * conv
    nb_ref[...] = jnp.concatenate([sb[:, 256:512], ub], axis=1)

    glu = glu_ref[...]
    uc = glu[:, 0:256] * jax.nn.sigmoid(glu[:, 256:512])
    conv = cw_ref[C_CONV - 1:C_CONV, :] * uc
    for k in range(C_CONV - 1):
        conv = conv + cw_ref[k:k + 1, :] * sc_ref[:, k * 256:(k + 1) * 256]
    yc_ref[...] = _layer_norm_silu(conv + cb_ref[...], lng_ref[...], lnb_ref[...])
    nc_ref[:, 0:(C_CONV - 2) * 256] = sc_ref[:, 256:(C_CONV - 1) * 256]
    nc_ref[:, (C_CONV - 2) * 256:(C_CONV - 1) * 256] = uc

    xbc = xbc_ref[...]
    conv = dw_ref[D_CONV - 1:D_CONV, :] * xbc
    for k in range(D_CONV - 1):
        conv = conv + dw_ref[k:k + 1, :] * sd_ref[:, k * D_XBC:(k + 1) * D_XBC]
    act_ref[...] = _silu(conv + db_ref[...])
    nd_ref[:, 0:(D_CONV - 2) * D_XBC] = sd_ref[:, D_XBC:(D_CONV - 1) * D_XBC]
    nd_ref[:, (D_CONV - 2) * D_XBC:(D_CONV - 1) * D_XBC] = xbc
    dtv_ref[...] = jax.nn.softplus(dt_ref[...] + dtb_ref[...])


def _dec_conv(bgc, glu, xbc, dt, sb, sc, sd, prm):
    nb = bgc.shape[0]
    shp = lambda w: jax.ShapeDtypeStruct((nb, w), f32)
    return pl.pallas_call(
        _dec_conv_kernel,
        out_shape=[shp(256), shp(256), shp(D_XBC), shp(128), shp(2 * 256), shp(30 * 256), shp(3 * D_XBC)],
        compiler_params=pltpu.CompilerParams(vmem_limit_bytes=VMEM_LIMIT), name="dec_conv",
    )(bgc, glu, xbc, dt, sb, sc, sd, *prm)


def _dec_ssm_kernel(act_ref, dtv_ref, zg_ref, h_ref, alog_ref, dskip_ref, dnorm_ref, yd_ref, hn_ref):
    act = act_ref[0]
    xs, bm, cm = act[:, 0:256], act[:, 256:512], act[:, 512:768]
    dtv = dtv_ref[0]
    l128 = lax.broadcasted_iota(i32, (1, 128), 1)
    a_row = jnp.where(l128 < N_HEADS, -jnp.exp(alog_ref[...]), 0.0)
    _, _, expand, _ = _ssd_consts()
    both = jnp.concatenate([jnp.exp(dtv * a_row), dtv, jnp.zeros((6, 128), f32)], axis=0)
    both_x = _dot(both, expand, HI)
    dec_x, dt_x = both_x[0:1, :], both_x[1:2, :]
    xdt = xs * dt_x
    r = lax.broadcasted_iota(i32, (128, 256), 0)
    lane = lax.broadcasted_iota(i32, (128, 256), 1)
    lrows = jnp.where((r == 0) & (lane < 128), xdt, 0.0) + jnp.where((r == 1) & (lane >= 128), xdt, 0.0)
    lrows = lrows + jnp.where(r == 2, dec_x, 0.0)
    lt = lrows.T
    rr = lax.broadcasted_iota(i32, (128, 256), 0)
    rl = lax.broadcasted_iota(i32, (128, 256), 1)
    bcat = jnp.concatenate([bm[:, 0:128], jnp.zeros((1, 128), f32)], axis=1)
    bcat1 = jnp.concatenate([bm[:, 128:256], jnp.zeros((1, 128), f32)], axis=1)
    rmat = jnp.where(rr == 0, bcat, 0.0) + jnp.where(rr == 1, bcat1, 0.0)
    rmat = rmat + jnp.where((rr == 2) & (rl >= 128), 1.0, 0.0)
    res = _dot(lt, rmat, HI)
    hn = h_ref[0] * res[:, 128:256] + res[:, 0:128]
    hn_ref[0] = hn
    r8 = lax.broadcasted_iota(i32, (8, 128), 0)
    crows = jnp.where(r8 == 0, cm[:, 0:128], 0.0) + jnp.where(r8 == 1, cm[:, 128:256], 0.0)
    y8 = _dot_nt(crows, hn, HI)
    l256 = lax.broadcasted_iota(i32, (1, 256), 1)
    y = jnp.where(l256 < 128, y8[0:1, :], y8[1:2, :])
    yd_ref[0] = _gated_rms(y, xs, zg_ref[0], dskip_ref[...], dnorm_ref[...])


def _dec_ssm(act, dtv, zg, h, alog, dskip, dnorm):
    nb = act.shape[0]
    per = lambda r, w: pl.BlockSpec((1, r, w), lambda b: (b, 0, 0))
    full = lambda a: pl.BlockSpec(a.shape, lambda b: (0,) * a.ndim)
    return pl.pallas_call(
        _dec_ssm_kernel,
        grid=(nb,),
        in_specs=[per(1, D_XBC), per(1, 128), per(1, 256), per(256, 128), full(alog), full(dskip), full(dnorm)],
        out_specs=[per(1, 256), per(256, 128)],
        out_shape=[jax.ShapeDtypeStruct((nb, 1, 256), f32), jax.ShapeDtypeStruct((nb, 256, 128), f32)],
        compiler_params=_cparams(("arbitrary",)), name="dec_ssm",
    )(act, dtv, zg, h, alog, dskip, dnorm)


def _rope_tables(pos, head_dim):
    rot = head_dim // 4
    half = rot // 2
    inv = ROPE_THETA ** (-jnp.arange(half, dtype=f32) * 2.0 / rot)
    ang = pos.astype(f32)[:, None] * inv[None, :]
    cos, sin = jnp.cos(ang), jnp.sin(ang)
    n = pos.shape[0]
    pad = jnp.zeros((n, head_dim - rot), f32)
    c = jnp.concatenate([cos, cos, pad + 1.0], axis=1)
    sm = jnp.concatenate([-sin, jnp.zeros((n, half), f32), pad], axis=1)
    sp = jnp.concatenate([jnp.zeros((n, half), f32), sin, pad], axis=1)
    rep = 128 // head_dim
    return [jnp.tile(t, (1, rep)) for t in (c, sm, sp)]


def _pad_lanes(v, width=128):
    return jnp.pad(v.astype(f32), (0, width - v.shape[0]))[None, :]


def _layer_params(l, w_in, w_out, g_pre_mix, g_post_mix, g_pre_ffn, g_post_ffn, bconv_w, cconv_w, cconv_b, cln_g,
                  cln_b, dconv_w, dconv_b, dt_bias, a_log, d_skip, d_norm, ffn_gate, ffn_up, ffn_down):
    w = w_in[l]
    cuts = np.cumsum((0,) + IN_SIZES)
    col = lambda j: w[:, cuts[j]:cuts[j + 1]]
    zpad = lambda a: jnp.pad(a, ((0, 0), (0, 128 - a.shape[1])))
    row = lambda a: a[l][None, :].astype(f32)
    wih, wil = _split(jnp.concatenate([col(3), jnp.tile(col(4), (1, 128 // IDX_DIM)),
                                       zpad(jnp.concatenate([col(5), col(12)], axis=1))], axis=1))
    return dict(
        wm=jnp.concatenate([w[:, :768], w[:, cuts[6]:cuts[12]]], axis=1).astype(bf16), wih=wih, wil=wil,
        g_pre_mix=row(g_pre_mix), g_post_mix=row(g_post_mix), g_pre_ffn=row(g_pre_ffn), g_post_ffn=row(g_post_ffn),
        w_out=w_out[l].astype(bf16), wg=ffn_gate[l].astype(bf16), wu=ffn_up[l].astype(bf16), wd=ffn_down[l].astype(bf16),
        bw=jnp.pad(bconv_w[l], ((0, 8 - B_CONV), (0, 0))), cw=jnp.pad(cconv_w[l], ((0, 32 - C_CONV), (0, 0))),
        cb=row(cconv_b), lng=row(cln_g), lnb=row(cln_b),
        dw=jnp.pad(dconv_w[l], ((0, 8 - D_CONV), (0, 0))), db=row(dconv_b),
        dtb=_pad_lanes(dt_bias[l]), alog=_pad_lanes(a_log[l]),
        dskip=jnp.repeat(d_skip[l].astype(f32), HEAD_DIM)[None, :], dnorm=row(d_norm),
    )


def _finish(p, ya, yb, yc, yd, x, tm):
    x1, hf = _outproj(ya, yb, yc, yd, x, p['w_out'], p['g_post_mix'], p['g_pre_ffn'], tm)
    return _ffn(hf, x1, p['wg'], p['wu'], p['wd'], p['g_post_ffn'], tm)


def kernel(x_prompt, x_sample, cache_k, cache_v, cache_kidx, page_table, state_bconv, state_cconv, state_dconv, state_ssm, w_in, w_out, g_pre_mix, g_post_mix, g_pre_ffn, g_post_ffn, bconv_w, cconv_w, cconv_b, cln_g, cln_b, dconv_w, dconv_b, dt_bias, a_log, d_skip, d_norm, ffn_gate, ffn_up, ffn_down):
    bsz, seq, _ = x_prompt.shape
    nb, t_dec, _ = x_sample.shape
    depth = w_in.shape[0]
    n_phys = cache_k.shape[1]
    n_pages = page_table.shape[1]
    past = n_pages * PAGE
    assert t_dec == 1 and seq % QB == 0 and seq % TC == 0 and past + t_dec > TOPK_MAX * 4
    weights = (w_in, w_out, g_pre_mix, g_post_mix, g_pre_ffn, g_post_ffn, bconv_w, cconv_w, cconv_b, cln_g, cln_b,
               dconv_w, dconv_b, dt_bias, a_log, d_skip, d_norm, ffn_gate, ffn_up, ffn_down)
    pos_p = jnp.arange(seq, dtype=jnp.int32)
    pos_s = jnp.full((nb,), past, jnp.int32)
    tabs_p = _rope_tables(pos_p, HEAD_DIM) + _rope_tables(pos_p, IDX_DIM)
    tabs_s = _rope_tables(pos_s, HEAD_DIM) + _rope_tables(pos_s, IDX_DIM)
    ck = cache_k.transpose(0, 1, 3, 4, 2).reshape(depth * n_phys, 256, PAGE)
    cv = cache_v.transpose(0, 1, 3, 4, 2).reshape(depth * n_phys, 256, PAGE)
    cki = cache_kidx.transpose(0, 1, 3, 2).reshape(depth * n_phys, IDX_DIM, PAGE)
    pt = page_table.reshape(-1).astype(jnp.int32)
    lane_head = (jnp.arange(256) // HEAD_DIM)[None, None, :] == jnp.arange(8)[None, :, None]

    hp = x_prompt.reshape(bsz * seq, D_MODEL)
    hs = x_sample.reshape(nb, D_MODEL)
    outs_p, outs_s = [], []
    for l in range(depth):
        p = _layer_params(l, *weights)
        conv_prm = (p['bw'], p['cw'], p['cb'], p['lng'], p['lnb'], p['dw'], p['db'], p['dtb'])
        ssm_prm = (p['alog'], p['dskip'], p['dnorm'])

        q, k, kb, v, vb, qi, ki4, kcat, wi, dt, bgc, glu, zg, xbc = _inproj(
            hp, p['g_pre_mix'], p['wm'], p['wih'], p['wil'], tabs_p, 256)
        ya = _attn_prompt(q, qi, wi, kcat, kb, vb, bsz, seq)
        yb, yc, yd, nbp, ncp, ndp, ssm_p = _mix_prompt(bgc, glu, zg, xbc, dt, conv_prm + ssm_prm, bsz, seq)
        hp = _finish(p, ya, yb, yc, yd, hp, 512)
        outs_p.append((k.reshape(bsz, seq, N_HEADS, HEAD_DIM), v.reshape(bsz, seq, N_HEADS, HEAD_DIM),
                       ki4[:, :IDX_DIM].reshape(bsz, seq, IDX_DIM), nbp, ncp, ndp,
                       ssm_p.reshape(bsz, N_HEADS, HEAD_DIM, D_STATE)))

        q, k, kb, v, vb, qi, ki4, kcat, wi, dt, bgc, glu, zg, xbc = _inproj(
            hs, p['g_pre_mix'], p['wm'], p['wih'], p['wil'], tabs_s, nb)
        topk = min(TOPK_MAX, (past + t_dec) // 4)
        qi_s = qi.reshape(nb, N_IDX_HEADS, IDX_DIM)
        wi_t = jnp.broadcast_to((wi[:, :N_IDX_HEADS] * (IDX_DIM ** -0.5 * N_IDX_HEADS ** -0.5))[:, :, None],
                                (nb, N_IDX_HEADS, 128))
        sc = _dec_scores(pt, qi_s, wi_t, cki, n_pages, l * n_phys).reshape(nb, past)
        bias, bnew = _dec_select(sc, qi, ki4, wi, topk)
        qm = jnp.where(lane_head, q[:, None, :], 0.0)
        ya = _dec_attn(pt, qm, k.reshape(nb, 1, 256), v.reshape(nb, 1, 256), bias.reshape(nb, 1, past),
                       bnew.reshape(nb, 1, 128), ck, cv, n_pages, l * n_phys).reshape(nb, 256)
        yb, yc, act, dtv, nbs, ncs, nds = _dec_conv(
            bgc, glu, xbc, dt, state_bconv[l].reshape(nb, -1), state_cconv[l].reshape(nb, -1),
            state_dconv[l].reshape(nb, -1), conv_prm)
        yd, ssm_s = _dec_ssm(act.reshape(nb, 1, D_XBC), dtv.reshape(nb, 1, 128), zg.reshape(nb, 1, 256),
                             state_ssm[l].reshape(nb, 256, D_STATE), *ssm_prm)
        hs = _finish(p, ya, yb, yc, yd.reshape(nb, 256), hs, nb)
        outs_s.append((k.reshape(nb, 1, N_HEADS, HEAD_DIM), v.reshape(nb, 1, N_HEADS, HEAD_DIM),
                       ki4[:, :IDX_DIM].reshape(nb, 1, IDX_DIM), nbs.reshape(nb, 2, 256), ncs.reshape(nb, 30, 256),
                       nds.reshape(nb, 3, D_XBC), ssm_s.reshape(nb, N_HEADS, HEAD_DIM, D_STATE)))

    k_p, v_p, kidx_p, bconv_p, cconv_p, dconv_p, ssm_p = [jnp.stack(a) for a in zip(*outs_p)]
    k_s, v_s, kidx_s, bconv_s, cconv_s, dconv_s, ssm_s = [jnp.stack(a) for a in zip(*outs_s)]
    return (hp.reshape(bsz, seq, D_MODEL), hs.reshape(nb, t_dec, D_MODEL), k_p, v_p, kidx_p, k_s, v_s, kidx_s,
            bconv_p, bconv_s, cconv_p, cconv_s, dconv_p, dconv_s, ssm_p, ssm_s)
```

```python
import functools
import math

import jax
import jax.numpy as jnp
import numpy as np
from jax import lax
from jax.experimental import pallas as pl
from jax.experimental.pallas import tpu as pltpu

f32, bf16, i32 = jnp.float32, jnp.bfloat16, jnp.int32
HI = lax.Precision.HIGHEST

D_MODEL = 1024
PAGE = 128
GW = 256
HEAD_DIM = 64
N_HEADS = 4
N_IDX_HEADS = 8
IDX_DIM = 32
TOPK_MAX = 256
ROPE_THETA = 500000.0
C_CONV = 31
B_CONV = 3
D_CONV = 4
D_XBC = 768
D_STATE = 128
SSD_CHUNK = 128
D_FF = 2816
RMS_EPS = 1e-6
LN_EPS = 1e-5
IN_SIZES = (256, 256, 256, 256, 32, 8, 256, 256, 256, 512, 256, 768, 4)
INT_MIN = -(2 ** 31)
KEY_NEG_INF = INT_MIN + 0x7FFFFF
NEG = -1e30
VMEM_LIMIT = 56 * 1024 * 1024


def _dot_nt(a, b, prec=None):
    return lax.dot_general(a, b, (((1,), (1,)), ((), ())), precision=prec, preferred_element_type=f32)


def _dot(a, b, prec=None):
    return jnp.dot(a, b, precision=prec, preferred_element_type=f32)


def _cparams(sem):
    return pltpu.CompilerParams(dimension_semantics=sem, vmem_limit_bytes=VMEM_LIMIT)


def _rms(x, g):
    return x * lax.rsqrt(jnp.mean(x * x, axis=-1, keepdims=True) + RMS_EPS) * g


def _silu(x):
    return x * jax.nn.sigmoid(x)


def _key_to_float(key):
    return lax.bitcast_convert_type(jnp.where(key < 0, key ^ jnp.int32(0x7FFFFFFF), key), f32)


def _split_f32(x):
    c = x * (2.0 ** 16 + 1.0)
    hi = c - (c - x)
    return hi, x - hi


def _split(x):
    hi, lo = _split_f32(x)
    return hi.astype(bf16), lo.astype(bf16)


def _kth_largest(count_ge, shape, topk):
    def bit_step(bi, key):
        cand = key + lax.shift_left(jnp.int32(1), 31 - bi)
        ok = (cand <= KEY_NEG_INF) | (count_ge(_key_to_float(cand)) >= topk)
        return jnp.where(ok, cand, key)

    key = lax.fori_loop(0, 32, bit_step, jnp.full(shape, INT_MIN, i32))
    return key, _key_to_float(key)


def _refine_between_floats(count_ge, lo, key, rows, topk, steps=40):
    hi = _key_to_float(key + 1)

    def step(_, carry):
        lo, hi = carry
        mid = lo + (hi - lo) * 0.5
        ge = count_ge(mid) >= topk
        return jnp.where(rows & ge, mid, lo), jnp.where(rows & jnp.logical_not(ge), mid, hi)

    return lax.fori_loop(0, steps, step, (lo, hi))[0]


def _rope(v, c, sm, sp, half):
    outs = []
    for s in range(v.shape[1] // 128):
        xs = v[:, s * 128:(s + 1) * 128]
        outs.append(xs * c + pltpu.roll(xs, 128 - half, 1) * sm + pltpu.roll(xs, half, 1) * sp)
    return jnp.concatenate(outs, axis=1)


def _inproj_kernel(x_ref, g_ref, wm_ref, wih_ref, wil_ref, c64_ref, sm64_ref, sp64_ref, c32_ref, sm32_ref, sp32_ref,
                   q_ref, k_ref, kb_ref, v_ref, vt_ref, qi_ref, ki_ref, kcat_ref, wi_o_ref, dt_o_ref,
                   bgc_ref, glu_ref, zg_ref, xbc_ref):
    u = _rms(x_ref[...], g_ref[...])
    ub, ul = _split(u)

    def mm(c0, c1):
        return _dot(ub, wm_ref[:, c0:c1])

    def mm3(c0, c1):
        wh = wih_ref[:, c0:c1]
        return _dot(ub, wh) + _dot(ul, wh) + _dot(ub, wil_ref[:, c0:c1])

    c64, sm64, sp64 = c64_ref[...], sm64_ref[...], sp64_ref[...]
    q_ref[...] = _rope(mm(0, 256), c64, sm64, sp64, 8)
    k = _rope(mm(256, 512), c64, sm64, sp64, 8)
    k_ref[...] = k
    kb_ref[...] = k.astype(bf16)
    v = mm(512, 768)
    v_ref[...] = v
    vt_ref[...] = v.T.astype(bf16)
    bgc_ref[...] = mm(768, 1536)
    glu_ref[...] = mm(1536, 2048)
    zg_ref[...] = mm(2048, 2304)
    xbc_ref[...] = mm(2304, 3072)
    c32, sm32, sp32 = c32_ref[...], sm32_ref[...], sp32_ref[...]
    qi_ref[...] = _rope(mm3(0, 256), c32, sm32, sp32, 4)
    ki4 = _rope(mm3(256, 384), c32, sm32, sp32, 4)
    ki_ref[...] = ki4
    kh, kl = _split(ki4)
    lane = lax.broadcasted_iota(i32, ki4.shape, 1)
    kcat_ref[...] = jnp.where((lane >= IDX_DIM) & (lane < 2 * IDX_DIM), kl, kh)
    small = mm3(384, 512)
    wi_o_ref[...] = small
    dt_o_ref[...] = pltpu.roll(small, 128 - N_IDX_HEADS, 1)


def _inproj(x, g, wm, wih, wil, tabs, tm):
    t = x.shape[0]
    nt = t // tm
    ntab = tabs[0].shape[0] // tm
    row = lambda w: pl.BlockSpec((tm, w), lambda i: (i, 0))
    full = lambda a: pl.BlockSpec(a.shape, lambda i: (0,) * a.ndim)
    tab = pl.BlockSpec((tm, 128), lambda i: (i % ntab, 0))
    widths = (256, 256, 256, 256, 256, 256, 128, 128, 128, 128, 768, 512, 256, 768)
    dtypes = (f32, f32, bf16, f32, bf16, f32, f32, bf16, f32, f32, f32, f32, f32, f32)
    out_specs = [row(w) for w in widths]
    out_shape = [jax.ShapeDtypeStruct((t, w), d) for w, d in zip(widths, dtypes)]
    out_specs[4] = pl.BlockSpec((256, tm), lambda i: (0, i))
    out_shape[4] = jax.ShapeDtypeStruct((256, t), bf16)
    return pl.pallas_call(
        _inproj_kernel,
        grid=(nt,),
        in_specs=[row(D_MODEL), full(g), full(wm), full(wih), full(wil)] + [tab] * 6,
        out_specs=out_specs,
        out_shape=out_shape,
        compiler_params=_cparams(("arbitrary",)),
        name="inproj",
    )(x, g, wm, wih, wil, *tabs)


QB = 256
SUB = 128


def _attn_t_kernel(q_ref, qi_ref, wi_ref, kcat_ref, k_ref, vt_ref, o_ref, sc_ref, cat_ref, qm_ref, acc_ref, *, topk):
    i = pl.program_id(1)
    nkb = i + 1
    n_sub = QB // SUB
    hi, lo = _split_f32(qi_ref[...])
    hi_t, lo_t = hi.T, lo.T
    for h in range(N_IDX_HEADS):
        rs = slice(h * IDX_DIM, (h + 1) * IDX_DIM)
        cat_ref[h] = jnp.concatenate([hi_t[rs], hi_t[rs], lo_t[rs], jnp.zeros((IDX_DIM, QB), f32)], axis=0).astype(bf16)
    q_t = (q_ref[...] * (HEAD_DIM ** -0.5)).T
    row = lax.broadcasted_iota(i32, (256, QB), 0)
    for h in range(N_HEADS):
        qm_ref[:, h * QB:(h + 1) * QB] = jnp.where(row // HEAD_DIM == h, q_t, 0.0).astype(bf16)
    w8 =(wi_ref[...] * (IDX_DIM ** -0.5 * N_IDX_HEADS ** -0.5)).T[0:N_IDX_HEADS]
    qpos = i * QB + lax.broadcasted_iota(i32, (1, QB), 1)
    kio = lax.broadcasted_iota(i32, (SUB, 1), 0)

    def tiles(kb):
        return [pl.ds(pl.multiple_of(kb * QB + j * SUB, SUB), SUB) for j in range(n_sub)]

    def score_block(kb, carry):
        for j, sl in enumerate(tiles(kb)):
            kc = kcat_ref[sl, :]
            acc = jnp.zeros((SUB, QB), f32)
            for h in range(N_IDX_HEADS):
                acc = acc + jnp.maximum(_dot(kc, cat_ref[h]), 0.0) * w8[h:h + 1, :]
            sc_ref[sl, :] = jnp.where(kb * QB + j * SUB + kio <= qpos, acc, -jnp.inf)
        return carry

    lax.fori_loop(0, nkb, score_block, 0)

    def count(pred):
        def body(kb, cnt):
            for sl in tiles(kb):
                cnt = cnt + jnp.where(pred(sc_ref[sl, :]), 1.0, 0.0).reshape(SUB // 8, 8, QB).sum(axis=0)
            return cnt
        return jnp.sum(lax.fori_loop(0, nkb, body, jnp.zeros((8, QB), f32)), axis=0, keepdims=True)

    def count_ge(t):
        return count(lambda s: s >= t)

    thr_key, thr = _kth_largest(count_ge, (1, QB), topk)
    real = thr > -jnp.inf
    tie = real & (count_ge(thr) > topk)
    any_tie = jnp.max(jnp.where(tie, 1.0, 0.0)) > 0.0

    @pl.when(jnp.logical_not(any_tie))
    def _():
        thr_fin = jnp.maximum(thr, jnp.finfo(f32).min)

        def mask_block(kb, carry):
            for sl in tiles(kb):
                sc_ref[sl, :] = jnp.where(sc_ref[sl, :] >= thr_fin, 0.0, NEG)
            return carry

        lax.fori_loop(0, nkb, mask_block, 0)

    @pl.when(any_tie)
    def _():
        thr2 = _refine_between_floats(count_ge, thr, thr_key, tie, topk)
        need = topk - count(lambda s: s > thr2)
        realf = jnp.where(real, 1.0, 0.0)
        r = lax.broadcasted_iota(i32, (SUB, SUB), 0)
        c = lax.broadcasted_iota(i32, (SUB, SUB), 1)
        tri = jnp.where(c <= r, 1.0, 0.0).astype(bf16)

        def tie_block(kb, carry):
            for sl in tiles(kb):
                s = sc_ref[sl, :]
                eqb = jnp.where(s == thr2, realf, 0.0).astype(bf16)
                cum = _dot(tri, eqb) + carry
                keep = jnp.where(s > thr2, 1.0, jnp.where(cum <= need, eqb.astype(f32), 0.0))
                sc_ref[sl, :] = jnp.where(keep > 0.0, 0.0, NEG)
                carry = cum[SUB - 1:SUB, :]
            return carry

        lax.fori_loop(0, nkb, tie_block, jnp.zeros((1, QB), f32))

    heads = [slice(h * HEAD_DIM, (h + 1) * HEAD_DIM) for h in range(N_HEADS)]

    def attn_block(kb, carry):
        ms, ls = list(carry[0]), list(carry[1])
        for sl in [pl.ds(pl.multiple_of(kb * QB, QB), QB)]:
            s_all = _dot(k_ref[sl, :], qm_ref[...])
            mask = sc_ref[sl, :]
            alphas, ps = [], []
            for h in range(N_HEADS):
                s = s_all[:, h * QB:(h + 1) * QB] + mask
                m_new = jnp.maximum(ms[h], jnp.max(s, axis=0, keepdims=True))
                alphas.append(jnp.exp(ms[h] - m_new))
                p = jnp.exp(s - m_new)
                ls[h] = alphas[h] * ls[h] + jnp.sum(p, axis=0, keepdims=True)
                ps.append(p.astype(bf16))
                ms[h] = m_new
            pvs = [_dot(vt_ref[heads[h], sl], ps[h]) for h in range(N_HEADS)]
            for h in range(N_HEADS):
                acc_ref[heads[h], :] = alphas[h] * acc_ref[heads[h], :] + pvs[h]
        return tuple(ms), tuple(ls)

    acc_ref[...] = jnp.zeros_like(acc_ref)
    m0 = tuple(jnp.full((1, QB), NEG, f32) for _ in range(N_HEADS))
    l0 = tuple(jnp.zeros((1, QB), f32) for _ in range(N_HEADS))
    _, ls = lax.fori_loop(0, nkb, attn_block, (m0, l0))
    out_t = jnp.concatenate([acc_ref[h * HEAD_DIM:(h + 1) * HEAD_DIM, :] / ls[h] for h in range(N_HEADS)], axis=0)
    o_ref[...] = out_t.T


def _attn_prompt_t(q, qi, wi, kcat, kb, vt, bsz, seq):
    nq = seq // QB
    topk = min(TOPK_MAX, seq // 4)
    qrow = lambda w: pl.BlockSpec((QB, w), lambda b, i: (b * nq + i, 0))
    seqblk = lambda w: pl.BlockSpec((seq, w), lambda b, i: (b, 0))
    return pl.pallas_call(
        functools.partial(_attn_t_kernel, topk=float(topk)),
        grid=(bsz, nq),
        in_specs=[qrow(256), qrow(256), qrow(128), seqblk(128), seqblk(256),
                  pl.BlockSpec((256, seq), lambda b, i: (0, b))],
        out_specs=qrow(256),
        out_shape=jax.ShapeDtypeStruct((bsz * seq, 256), f32),
        scratch_shapes=[pltpu.VMEM((seq, QB), f32), pltpu.VMEM((N_IDX_HEADS, 128, QB), bf16),
                        pltpu.VMEM((256, N_HEADS * QB), bf16), pltpu.VMEM((256, QB), f32)],
        compiler_params=_cparams(("arbitrary", "arbitrary")),
        name="attn_prompt",
    )(q, qi, wi, kcat, kb, vt)


TC = 256
HALO_B, HALO_C, HALO_D = 8, 32, 8


def _per_head_lanes(a):
    lane = lax.broadcasted_iota(i32, (a.shape[0], N_HEADS * HEAD_DIM), 1)
    out = jnp.broadcast_to(a[:, N_HEADS - 1:N_HEADS], lane.shape)
    for h in range(N_HEADS - 2, -1, -1):
        out = jnp.where(lane < (h + 1) * HEAD_DIM, a[:, h:h + 1], out)
    return out


def _cumsum_rows(x):
    n = x.shape[0]
    r = lax.broadcasted_iota(i32, (n, n), 0)
    c = lax.broadcasted_iota(i32, (n, n), 1)
    tril = jnp.where(c <= r, 1.0, 0.0).astype(bf16)
    p1, rest = _split_f32(x)
    p2, p3 = _split_f32(rest)
    return _dot(tril, p1.astype(bf16)) + _dot(tril, p2.astype(bf16)) + _dot(tril, p3.astype(bf16))


def _ssd_chunk(xs, bm, cm, dtv, a_row, hcat_ref):
    r = lax.broadcasted_iota(i32, (128, 128), 0)
    c = lax.broadcasted_iota(i32, (128, 128), 1)
    causal = c <= r
    lane = lax.broadcasted_iota(i32, (128, 256), 1)
    da = dtv * a_row
    cs = _cumsum_rows(da)
    cs_t = cs.T
    dt_x = _per_head_lanes(dtv)
    ecs_x = _per_head_lanes(jnp.exp(cs))
    cs_last = cs[127:128, :]
    wend_x = _per_head_lanes(jnp.exp(cs_last - cs) * dtv)
    xdt = (xs * dt_x).astype(bf16)
    bmb, cmb = bm.astype(bf16), cm.astype(bf16)
    hb = hcat_ref[...].astype(bf16)
    y = jnp.zeros((128, 256), f32)
    ystate = []
    for g in range(2):
        cg = cmb[:, g * 128:(g + 1) * 128]
        cb = _dot_nt(cg, bmb[:, g * 128:(g + 1) * 128])
        ystate.append(_dot_nt(cg, hb))
        for h in (2 * g, 2 * g + 1):
            seg = cs[:, h:h + 1] - cs_t[h:h + 1, :]
            dec = jnp.where(causal, jnp.exp(jnp.where(causal, seg, 0.0)), 0.0)
            yh = _dot((cb * dec).astype(bf16), xdt)
            y = jnp.where(lane // 64 == h, yh, y)
    y = y + jnp.where(lane < 128, ystate[0], ystate[1]) * ecs_x
    xw_t = (xs * wend_x).T.astype(bf16)
    upd = jnp.concatenate([_dot(xw_t[0:128], bmb[:, 0:128]), _dot(xw_t[128:256], bmb[:, 128:256])], axis=0)
    elast = jnp.exp(cs_last)
    dcol = jnp.concatenate([jnp.broadcast_to(elast[:, h:h + 1], (HEAD_DIM, D_STATE)) for h in range(N_HEADS)], axis=0)
    hcat_ref[...] = hcat_ref[...] * dcol + upd
    return y


def _gated_rms(ys, xs, zg, dskip, dnorm):
    yg = (ys + dskip * xs) * _silu(zg)
    return yg * lax.rsqrt(jnp.mean(yg * yg, axis=-1, keepdims=True) + RMS_EPS) * dnorm


def _layer_norm_silu(x, g, b):
    mu = jnp.mean(x, axis=-1, keepdims=True)
    var = jnp.mean(jnp.square(x - mu), axis=-1, keepdims=True)
    return _silu((x - mu) * lax.rsqrt(var + LN_EPS) * g + b)


def _mix_kernel(bgc_ref, glu_ref, zg_ref, xbc_ref, dt_ref,
                bw_ref, cw_ref, cb_ref, lng_ref, lnb_ref, dw_ref, db_ref, dtb_ref, alog_ref, dskip_ref, dnorm_ref,
                yb_ref, yc_ref, yd_ref, nb_ref, nc_ref, nd_ref, ssm_ref,
                eb_ref, ec_ref, ed_ref, hcat_ref):
    i = pl.program_id(1)

    @pl.when(i == 0)
    def _():
        eb_ref[0:HALO_B, :] = jnp.zeros((HALO_B, 256), f32)
        ec_ref[0:HALO_C, :] = jnp.zeros((HALO_C, 256), f32)
        ed_ref[0:HALO_D, :] = jnp.zeros((HALO_D, D_XBC), f32)
        hcat_ref[...] = jnp.zeros_like(hcat_ref)

    bgc = bgc_ref[...]
    eb_ref[HALO_B:HALO_B + TC, :] = bgc[:, 512:768] * bgc[:, 0:256]
    conv = jnp.zeros((TC, 256), f32)
    for k in range(B_CONV):
        conv = conv + bw_ref[k:k + 1, :] * eb_ref[pl.ds(HALO_B - (B_CONV - 1) + k, TC), :]
    yb_ref[...] = bgc[:, 256:512] * conv
    nb_ref[0] = eb_ref[HALO_B + TC - (B_CONV - 1):HALO_B + TC, :]
    eb_ref[0:HALO_B, :] = eb_ref[TC:TC + HALO_B, :]

    glu = glu_ref[...]
    ec_ref[HALO_C:HALO_C + TC, :] = glu[:, 0:256] * jax.nn.sigmoid(glu[:, 256:512])
    conv = jnp.zeros((TC, 256), f32)
    for k in range(C_CONV):
        conv = conv + cw_ref[k:k + 1, :] * ec_ref[pl.ds(HALO_C - (C_CONV - 1) + k, TC), :]
    yc_ref[...] = _layer_norm_silu(conv + cb_ref[...], lng_ref[...], lnb_ref[...])
    nc_ref[0] = ec_ref[HALO_C + TC - (C_CONV - 1):HALO_C + TC, :]
    ec_ref[0:HALO_C, :] = ec_ref[TC:TC + HALO_C, :]

    ed_ref[HALO_D:HALO_D + TC, :] = xbc_ref[...]
    conv = jnp.zeros((TC, D_XBC), f32)
    for k in range(D_CONV):
        conv = conv + dw_ref[k:k + 1, :] * ed_ref[pl.ds(HALO_D - (D_CONV - 1) + k, TC), :]
    act = _silu(conv + db_ref[...])
    nd_ref[0] = ed_ref[HALO_D + TC - (D_CONV - 1):HALO_D + TC, :]
    ed_ref[0:HALO_D, :] = ed_ref[TC:TC + HALO_D, :]
    l128 = lax.broadcasted_iota(i32, (1, 128), 1)
    a_row = jnp.where(l128 < N_HEADS, -jnp.exp(alog_ref[...]), 0.0)
    dtv = jax.nn.softplus(dt_ref[...] + dtb_ref[...])
    zg = zg_ref[...]
    for j in range(TC // SSD_CHUNK):
        rs = slice(j * SSD_CHUNK, (j + 1) * SSD_CHUNK)
        xs = act[rs, 0:256]
        y = _ssd_chunk(xs, act[rs, 256:512], act[rs, 512:768], dtv[rs], a_row, hcat_ref)
        yd_ref[rs, :] = _gated_rms(y, xs, zg[rs], dskip_ref[...], dnorm_ref[...])
    ssm_ref[0] = hcat_ref[...]


def _mix_prompt(bgc, glu, zg, xbc, dt, prm, bsz, seq):
    nt = seq // TC
    row = lambda w: pl.BlockSpec((TC, w), lambda b, i: (b * nt + i, 0))
    full = lambda a: pl.BlockSpec(a.shape, lambda b, i: (0,) * a.ndim)
    st = lambda r, w: pl.BlockSpec((1, r, w), lambda b, i: (b, 0, 0))
    t = bsz * seq
    return pl.pallas_call(
        _mix_kernel,
        grid=(bsz, nt),
        in_specs=[row(768), row(512), row(256), row(768), row(128)] + [full(a) for a in prm],
        out_specs=[row(256), row(256), row(256), st(2, 256), st(30, 256), st(3, D_XBC), st(256, 128)],
        out_shape=[jax.ShapeDtypeStruct((t, 256), f32)] * 3 + [
            jax.ShapeDtypeStruct((bsz, 2, 256), f32), jax.ShapeDtypeStruct((bsz, 30, 256), f32),
            jax.ShapeDtypeStruct((bsz, 3, D_XBC), f32), jax.ShapeDtypeStruct((bsz, 256, 128), f32)],
        scratch_shapes=[pltpu.VMEM((HALO_B + TC, 256), f32), pltpu.VMEM((HALO_C + TC, 256), f32),
                        pltpu.VMEM((HALO_D + TC, D_XBC), f32), pltpu.VMEM((256, 128), f32)],
        compiler_params=_cparams(("arbitrary", "arbitrary")),
        name="mix_prompt",
    )(bgc, glu, zg, xbc, dt, *prm)


def _outproj_kernel(ya_ref, yb_ref, yc_ref, yd_ref, x_ref, w_ref, gpost_ref, gpre_ref, x1_ref, hf_ref):
    mix = _dot(ya_ref[...].astype(bf16), w_ref[0:256, :])
    mix = mix + _dot(yb_ref[...].astype(bf16), w_ref[256:512, :])
    mix = mix + _dot(yc_ref[...].astype(bf16), w_ref[512:768, :])
    mix = mix + _dot(yd_ref[...].astype(bf16), w_ref[768:1024, :])
    x1 = x_ref[...] + _rms(mix, gpost_ref[...])
    x1_ref[...] = x1
    hf_ref[...] = _rms(x1, gpre_ref[...]).astype(bf16)


def _outproj(ya, yb, yc, yd, x, w, gpost, gpre, tm):
    t = x.shape[0]
    row = lambda w_: pl.BlockSpec((tm, w_), lambda i: (i, 0))
    full = lambda a: pl.BlockSpec(a.shape, lambda i: (0,) * a.ndim)
    return pl.pallas_call(
        _outproj_kernel,
        grid=(t // tm,),
        in_specs=[row(256)] * 4 + [row(D_MODEL), full(w), full(gpost), full(gpre)],
        out_specs=[row(D_MODEL), row(D_MODEL)],
        out_shape=[jax.ShapeDtypeStruct((t, D_MODEL), f32), jax.ShapeDtypeStruct((t, D_MODEL), bf16)],
        compiler_params=_cparams(("arbitrary",)),
        name="outproj",
    )(ya, yb, yc, yd, x, w, gpost, gpre)


FF_CHUNK = 1408


def _ffn_kernel(hf_ref, x1_ref, wg_ref, wu_ref, wd_ref, g_ref, o_ref, acc_ref):
    j = pl.program_id(1)
    hf = hf_ref[...]
    a = _silu(_dot(hf, wg_ref[...])) * _dot(hf, wu_ref[...])
    part = _dot(a.astype(bf16), wd_ref[...])

    @pl.when(j == 0)
    def _():
        acc_ref[...] = part

    @pl.when(j > 0)
    def _():
        acc_ref[...] = acc_ref[...] + part

    @pl.when(j == pl.num_programs(1) - 1)
    def _():
        o_ref[...] = x1_ref[...] + _rms(acc_ref[...], g_ref[...])


def _ffn(hf, x1, wg, wu, wd, g, tm):
    t = x1.shape[0]
    nj = D_FF // FF_CHUNK
    row = lambda: pl.BlockSpec((tm, D_MODEL), lambda i, j: (i, 0))
    return pl.pallas_call(
        _ffn_kernel,
        grid=(t // tm, nj),
        in_specs=[row(), row(), pl.BlockSpec((D_MODEL, FF_CHUNK), lambda i, j: (0, j)),
                  pl.BlockSpec((D_MODEL, FF_CHUNK), lambda i, j: (0, j)),
                  pl.BlockSpec((FF_CHUNK, D_MODEL), lambda i, j: (j, 0)),
                  pl.BlockSpec((1, D_MODEL), lambda i, j: (0, 0))],
        out_specs=row(),
        out_shape=jax.ShapeDtypeStruct((t, D_MODEL), f32),
        scratch_shapes=[pltpu.VMEM((tm, D_MODEL), f32)],
        compiler_params=_cparams(("arbitrary", "arbitrary")),
        name="ffn",
    )(hf, x1, wg, wu, wd, g)


def _page_copies(pt_ref, src_hbm, dst, sem, b, n_pages, base):
    return [pltpu.make_async_copy(src_hbm.at[base + pt_ref[b * n_pages + p]],
                                  dst.at[:, pl.ds(p * PAGE, PAGE)], sem) for p in range(n_pages)]


def _dec_score_kernel(pt_ref, qi_ref, wi_ref, cki_hbm, o_ref, buf, sem, *, n_pages, base):
    b = pl.program_id(0)
    nb = pl.num_programs(0)

    def copies(bb, slot):
        return _page_copies(pt_ref, cki_hbm, buf.at[slot], sem.at[slot], bb, n_pages, base)

    @pl.when(b == 0)
    def _():
        for c in copies(0, 0):
            c.start()

    @pl.when(b + 1 < nb)
    def _():
        for c in copies(b + 1, (b + 1) % 2):
            c.start()

    slot = b % 2
    for c in copies(b, slot):
        c.wait()
    s = _dot(qi_ref[b], buf[slot], HI)
    w = jnp.tile(wi_ref[b], (1, n_pages))
    o_ref[0] = jnp.sum(jnp.maximum(s, 0.0) * w, axis=0, keepdims=True)


def _dec_scores(pt, qi_s, wi_t, cki, n_pages, base):
    nb = qi_s.shape[0]
    past = n_pages * PAGE
    gs = pltpu.PrefetchScalarGridSpec(
        num_scalar_prefetch=1, grid=(nb,),
        in_specs=[pl.BlockSpec(qi_s.shape, lambda b, pt_: (0, 0, 0)), pl.BlockSpec(wi_t.shape, lambda b, pt_: (0, 0, 0)),
                  pl.BlockSpec(memory_space=pl.ANY)],
        out_specs=pl.BlockSpec((1, 1, past), lambda b, pt_: (b, 0, 0)),
        scratch_shapes=[pltpu.VMEM((2, IDX_DIM, past), f32), pltpu.SemaphoreType.DMA((2,))])
    return pl.pallas_call(
        functools.partial(_dec_score_kernel, n_pages=n_pages, base=base),
        grid_spec=gs, out_shape=jax.ShapeDtypeStruct((nb, 1, past), f32),
        compiler_params=_cparams(("arbitrary",)), name="dec_scores",
    )(pt, qi_s, wi_t, cki)


def _dec_select_kernel(sc_ref, qi_ref, ki_ref, wi_ref, bias_ref, bnew_ref, *, topk):
    nb, past = sc_ref.shape
    wi = wi_ref[...] * (IDX_DIM ** -0.5 * N_IDX_HEADS ** -0.5)
    gj = lax.broadcasted_iota(i32, (256, 128), 0)
    gh = lax.broadcasted_iota(i32, (256, 128), 1)
    seg = jnp.where(gj // IDX_DIM == gh, 1.0, 0.0)
    ki = jnp.concatenate([ki_ref[...], ki_ref[...]], axis=1)
    s_new = _dot(qi_ref[...] * ki, seg, HI)
    sc_new = jnp.broadcast_to(jnp.sum(jnp.maximum(s_new, 0.0) * wi, axis=-1, keepdims=True), (nb, 128))
    sc = sc_ref[...]
    ones = jnp.ones((past, 128), bf16)
    wide = lambda t: jnp.concatenate([t] * (past // 128), axis=1)

    def count(pred_past, pred_new):
        return _dot(jnp.where(pred_past, 1.0, 0.0).astype(bf16), ones) + jnp.where(pred_new, 1.0, 0.0)

    def count_ge(t):
        return count(sc >= wide(t), sc_new >= t)

    thr_key, thr = _kth_largest(count_ge, (nb, 128), topk)
    bias_ref[...] = jnp.where(sc >= wide(thr), 0.0, NEG)
    bnew_ref[...] = jnp.where(sc_new >= thr, 0.0, NEG)
    tie = count_ge(thr) > topk

    @pl.when(jnp.max(jnp.where(tie, 1.0, 0.0)) > 0.0)
    def _():
        thr2 = _refine_between_floats(count_ge, thr, thr_key, tie, topk)
        need = topk - count(sc > wide(thr2), sc_new > thr2)
        eqb = jnp.where(sc == wide(thr2), 1.0, 0.0).astype(bf16)
        r = lax.broadcasted_iota(i32, (PAGE, PAGE), 0)
        c = lax.broadcasted_iota(i32, (PAGE, PAGE), 1)
        tri = jnp.where(r <= c, 1.0, 0.0).astype(bf16)
        carry = jnp.zeros((nb, 128), f32)
        for p in range(past // PAGE):
            sl = slice(p * PAGE, (p + 1) * PAGE)
            cum = _dot(eqb[:, sl], tri) + carry
            keep = jnp.where(sc[:, sl] > thr2, 1.0, jnp.where(cum <= need, eqb[:, sl].astype(f32), 0.0))
            bias_ref[:, sl] = jnp.where(keep > 0.0, 0.0, NEG)
            carry = carry + _dot(eqb[:, sl], ones[0:PAGE])
        keep_new = jnp.where(sc_new > thr2, 1.0, jnp.where((sc_new == thr2) & (carry + 1.0 <= need), 1.0, 0.0))
        bnew_ref[...] = jnp.where(keep_new > 0.0, 0.0, NEG)


def _dec_select(sc, qi, ki4, wi, topk):
    nb, past = sc.shape
    return pl.pallas_call(
        functools.partial(_dec_select_kernel, topk=float(topk)),
        out_shape=[jax.ShapeDtypeStruct((nb, past), f32), jax.ShapeDtypeStruct((nb, 128), f32)],
        compiler_params=pltpu.CompilerParams(vmem_limit_bytes=VMEM_LIMIT), name="dec_select",
    )(sc, qi, ki4, wi)


def _dec_attn_kernel(pt_ref, qm_ref, kn_ref, vn_ref, bias_ref, bnew_ref, ck_hbm, cv_hbm, o_ref,
                     kbuf, vbuf, sem, *, n_pages, base):
    b = pl.program_id(0)
    nb = pl.num_programs(0)

    def copies(bb, slot):
        return (_page_copies(pt_ref, ck_hbm, kbuf.at[slot], sem.at[0, slot], bb, n_pages, base)
                + _page_copies(pt_ref, cv_hbm, vbuf.at[slot], sem.at[1, slot], bb, n_pages, base))

    @pl.when(b == 0)
    def _():
        for c in copies(0, 0):
            c.start()

    @pl.when(b + 1 < nb)
    def _():
        for c in copies(b + 1, (b + 1) % 2):
            c.start()

    slot = b % 2
    for c in copies(b, slot):
        c.wait()
    qm = qm_ref[0] * (HEAD_DIM ** -0.5)
    s = _dot(qm.astype(bf16), kbuf[slot].astype(bf16)) + bias_ref[0]
    s_new = jnp.sum(qm * kn_ref[0], axis=-1, keepdims=True) + bnew_ref[0][:, 0:1]
    m = jnp.maximum(jnp.max(s, axis=-1, keepdims=True), s_new)
    p = jnp.exp(s - m)
    p_new = jnp.exp(s_new - m)
    den = jnp.sum(p, axis=-1, keepdims=True) + p_new
    out8 = (_dot_nt(p.astype(bf16), vbuf[slot].astype(bf16)) + p_new * vn_ref[0]) / den
    row = lax.broadcasted_iota(i32, (8, 256), 0)
    lane = lax.broadcasted_iota(i32, (8, 256), 1)
    o_ref[0] = jnp.sum(jnp.where(lane // HEAD_DIM == row, out8, 0.0), axis=0, keepdims=True)


def _dec_attn(pt, qm, kn, vn, bias, bnew, ck, cv, n_pages, base):
    nb = qm.shape[0]
    past = n_pages * PAGE
    per = lambda r, w: pl.BlockSpec((1, r, w), lambda b, pt_: (b, 0, 0))
    gs = pltpu.PrefetchScalarGridSpec(
        num_scalar_prefetch=1, grid=(nb,),
        in_specs=[per(8, 256), per(1, 256), per(1, 256), per(1, past), per(1, 128),
                  pl.BlockSpec(memory_space=pl.ANY), pl.BlockSpec(memory_space=pl.ANY)],
        out_specs=per(1, 256),
        scratch_shapes=[pltpu.VMEM((2, 256, past), f32), pltpu.VMEM((2, 256, past), f32),
                        pltpu.SemaphoreType.DMA((2, 2))])
    return pl.pallas_call(
        functools.partial(_dec_attn_kernel, n_pages=n_pages, base=base),
        grid_spec=gs, out_shape=jax.ShapeDtypeStruct((nb, 1, 256), f32),
        compiler_params=_cparams(("arbitrary",)), name="dec_attn",
    )(pt, qm, kn, vn, bias, bnew, ck, cv)


def _dec_conv_kernel(bgc_ref, glu_ref, xbc_ref, dt_ref, sb_ref, sc_ref, sd_ref,
                     bw_ref, cw_ref, cb_ref, lng_ref, lnb_ref, dw_ref, db_ref, dtb_ref,
                     yb_ref, yc_ref, act_ref, dtv_ref, nb_ref, nc_ref, nd_ref):
    bgc = bgc_ref[...]
    ub = bgc[:, 512:768] * bgc[:, 0:256]
    sb = sb_ref[...]
    conv = bw_ref[0:1, :] * sb[:, 0:256] + bw_ref[1:2, :] * sb[:, 256:512] + bw_ref[2:3, :] * ub
    yb_ref[...] = bgc[:, 256:512] * conv
    nb_ref[...] = jnp.concatenate([sb[:, 256:512], ub], axis=1)

    glu = glu_ref[...]
    uc = glu[:, 0:256] * jax.nn.sigmoid(glu[:, 256:512])
    conv = cw_ref[C_CONV - 1:C_CONV, :] * uc
    for k in range(C_CONV - 1):
        conv = conv + cw_ref[k:k + 1, :] * sc_ref[:, k * 256:(k + 1) * 256]
    yc_ref[...] = _layer_norm_silu(conv + cb_ref[...], lng_ref[...], lnb_ref[...])
    nc_ref[:, 0:(C_CONV - 2) * 256] = sc_ref[:, 256:(C_CONV - 1) * 256]
    nc_ref[:, (C_CONV - 2) * 256:(C_CONV - 1) * 256] = uc

    xbc = xbc_ref[...]
    conv = dw_ref[D_CONV - 1:D_CONV, :] * xbc
    for k in range(D_CONV - 1):
        conv = conv + dw_ref[k:k + 1, :] * sd_ref[:, k * D_XBC:(k + 1) * D_XBC]
    act_ref[...] = _silu(conv + db_ref[...])
    nd_ref[:, 0:(D_CONV - 2) * D_XBC] = sd_ref[:, D_XBC:(D_CONV - 1) * D_XBC]
    nd_ref[:, (D_CONV - 2) * D_XBC:(D_CONV - 1) * D_XBC] = xbc
    dtv_ref[...] = jax.nn.softplus(dt_ref[...] + dtb_ref[...])


def _dec_conv(bgc, glu, xbc, dt, sb, sc, sd, prm):
    nb = bgc.shape[0]
    shp = lambda w: jax.ShapeDtypeStruct((nb, w), f32)
    return pl.pallas_call(
        _dec_conv_kernel,
        out_shape=[shp(256), shp(256), shp(D_XBC), shp(128), shp(2 * 256), shp(30 * 256), shp(3 * D_XBC)],
        compiler_params=pltpu.CompilerParams(vmem_limit_bytes=VMEM_LIMIT), name="dec_conv",
    )(bgc, glu, xbc, dt, sb, sc, sd, *prm)


def _dec_ssm_kernel(act_ref, dtv_ref, zg_ref, h_ref, alog_ref, dskip_ref, dnorm_ref, yd_ref, hn_ref):
    act = act_ref[0]
    xs, bm, cm = act[:, 0:256], act[:, 256:512], act[:, 512:768]
    dtv = dtv_ref[0]
    l128 = lax.broadcasted_iota(i32, (1, 128), 1)
    a_row = jnp.where(l128 < N_HEADS, -jnp.exp(alog_ref[...]), 0.0)
    dec_x = _per_head_lanes(jnp.exp(dtv * a_row))
    xdt = xs * _per_head_lanes(dtv)
    r = lax.broadcasted_iota(i32, (128, 256), 0)
    lane = lax.broadcasted_iota(i32, (128, 256), 1)
    lrows = jnp.where((r == 0) & (lane < 128), xdt, 0.0) + jnp.where((r == 1) & (lane >= 128), xdt, 0.0)
    lrows = lrows + jnp.where(r == 2, dec_x, 0.0)
    lt = lrows.T
    rr = lax.broadcasted_iota(i32, (128, 256), 0)
    rl = lax.broadcasted_iota(i32, (128, 256), 1)
    bcat = jnp.concatenate([bm[:, 0:128], jnp.zeros((1, 128), f32)], axis=1)
    bcat1 = jnp.concatenate([bm[:, 128:256], jnp.zeros((1, 128), f32)], axis=1)
    rmat = jnp.where(rr == 0, bcat, 0.0) + jnp.where(rr == 1, bcat1, 0.0)
    rmat = rmat + jnp.where((rr == 2) & (rl >= 128), 1.0, 0.0)
    res = _dot(lt, rmat, HI)
    hn = h_ref[0] * res[:, 128:256] + res[:, 0:128]
    hn_ref[0] = hn
    r8 = lax.broadcasted_iota(i32, (8, 128), 0)
    crows = jnp.where(r8 == 0, cm[:, 0:128], 0.0) + jnp.where(r8 == 1, cm[:, 128:256], 0.0)
    y8 = _dot_nt(crows, hn, HI)
    l256 = lax.broadcasted_iota(i32, (1, 256), 1)
    y = jnp.where(l256 < 128, y8[0:1, :], y8[1:2, :])
    yd_ref[0] = _gated_rms(y, xs, zg_ref[0], dskip_ref[...], dnorm_ref[...])


def _dec_ssm(act, dtv, zg, h, alog, dskip, dnorm):
    nb = act.shape[0]
    per = lambda r, w: pl.BlockSpec((1, r, w), lambda b: (b, 0, 0))
    full = lambda a: pl.BlockSpec(a.shape, lambda b: (0,) * a.ndim)
    return pl.pallas_call(
        _dec_ssm_kernel,
        grid=(nb,),
        in_specs=[per(1, D_XBC), per(1, 128), per(1, 256), per(256, 128), full(alog), full(dskip), full(dnorm)],
        out_specs=[per(1, 256), per(256, 128)],
        out_shape=[jax.ShapeDtypeStruct((nb, 1, 256), f32), jax.ShapeDtypeStruct((nb, 256, 128), f32)],
        compiler_params=_cparams(("arbitrary",)), name="dec_ssm",
    )(act, dtv, zg, h, alog, dskip, dnorm)


def _rope_tables(pos, head_dim):
    rot = head_dim // 4
    half = rot // 2
    inv = ROPE_THETA ** (-jnp.arange(half, dtype=f32) * 2.0 / rot)
    ang = pos.astype(f32)[:, None] * inv[None, :]
    cos, sin = jnp.cos(ang), jnp.sin(ang)
    n = pos.shape[0]
    pad = jnp.zeros((n, head_dim - rot), f32)
    c = jnp.concatenate([cos, cos, pad + 1.0], axis=1)
    sm = jnp.concatenate([-sin, jnp.zeros((n, half), f32), pad], axis=1)
    sp = jnp.concatenate([jnp.zeros((n, half), f32), sin, pad], axis=1)
    rep = 128 // head_dim
    return [jnp.tile(t, (1, rep)) for t in (c, sm, sp)]


def _pad_lanes(v, width=128):
    return jnp.pad(v.astype(f32), (0, width - v.shape[0]))[None, :]


def _layer_params(l, w_in, w_out, g_pre_mix, g_post_mix, g_pre_ffn, g_post_ffn, bconv_w, cconv_w, cconv_b, cln_g,
                  cln_b, dconv_w, dconv_b, dt_bias, a_log, d_skip, d_norm, ffn_gate, ffn_up, ffn_down):
    w = w_in[l]
    cuts = np.cumsum((0,) + IN_SIZES)
    col = lambda j: w[:, cuts[j]:cuts[j + 1]]
    zpad = lambda a: jnp.pad(a, ((0, 0), (0, 128 - a.shape[1])))
    row = lambda a: a[l][None, :].astype(f32)
    wih, wil = _split(jnp.concatenate([col(3), jnp.tile(col(4), (1, 128 // IDX_DIM)),
                                       zpad(jnp.concatenate([col(5), col(12)], axis=1))], axis=1))
    return dict(
        wm=jnp.concatenate([w[:, :768], w[:, cuts[6]:cuts[12]]], axis=1).astype(bf16), wih=wih, wil=wil,
        g_pre_mix=row(g_pre_mix), g_post_mix=row(g_post_mix), g_pre_ffn=row(g_pre_ffn), g_post_ffn=row(g_post_ffn),
        w_out=w_out[l].astype(bf16), wg=ffn_gate[l].astype(bf16), wu=ffn_up[l].astype(bf16), wd=ffn_down[l].astype(bf16),
        bw=jnp.pad(bconv_w[l], ((0, 8 - B_CONV), (0, 0))), cw=jnp.pad(cconv_w[l], ((0, 32 - C_CONV), (0, 0))),
        cb=row(cconv_b), lng=row(cln_g), lnb=row(cln_b),
        dw=jnp.pad(dconv_w[l], ((0, 8 - D_CONV), (0, 0))), db=row(dconv_b),
        dtb=_pad_lanes(dt_bias[l]), alog=_pad_lanes(a_log[l]),
        dskip=jnp.repeat(d_skip[l].astype(f32), HEAD_DIM)[None, :], dnorm=row(d_norm),
    )


def _finish(p, ya, yb, yc, yd, x, tm):
    x1, hf = _outproj(ya, yb, yc, yd, x, p['w_out'], p['g_post_mix'], p['g_pre_ffn'], tm)
    return _ffn(hf, x1, p['wg'], p['wu'], p['wd'], p['g_post_ffn'], tm)


def kernel(x_prompt, x_sample, cache_k, cache_v, cache_kidx, page_table, state_bconv, state_cconv, state_dconv, state_ssm, w_in, w_out, g_pre_mix, g_post_mix, g_pre_ffn, g_post_ffn, bconv_w, cconv_w, cconv_b, cln_g, cln_b, dconv_w, dconv_b, dt_bias, a_log, d_skip, d_norm, ffn_gate, ffn_up, ffn_down):
    bsz, seq, _ = x_prompt.shape
    nb, t_dec, _ = x_sample.shape
    depth = w_in.shape[0]
    n_phys = cache_k.shape[1]
    n_pages = page_table.shape[1]
    past = n_pages * PAGE
    assert t_dec == 1 and seq % QB == 0 and seq % TC == 0 and past + t_dec > TOPK_MAX * 4
    weights = (w_in, w_out, g_pre_mix, g_post_mix, g_pre_ffn, g_post_ffn, bconv_w, cconv_w, cconv_b, cln_g, cln_b,
               dconv_w, dconv_b, dt_bias, a_log, d_skip, d_norm, ffn_gate, ffn_up, ffn_down)
    pos_p = jnp.arange(seq, dtype=jnp.int32)
    pos_s = jnp.full((nb,), past, jnp.int32)
    tabs_p = _rope_tables(pos_p, HEAD_DIM) + _rope_tables(pos_p, IDX_DIM)
    tabs_s = _rope_tables(pos_s, HEAD_DIM) + _rope_tables(pos_s, IDX_DIM)
    ck = cache_k.transpose(0, 1, 3, 4, 2).reshape(depth * n_phys, 256, PAGE)
    cv = cache_v.transpose(0, 1, 3, 4, 2).reshape(depth * n_phys, 256, PAGE)
    cki = cache_kidx.transpose(0, 1, 3, 2).reshape(depth * n_phys, IDX_DIM, PAGE)
    pt = page_table.reshape(-1).astype(jnp.int32)
    lane_head = (jnp.arange(256) // HEAD_DIM)[None, None, :] == jnp.arange(8)[None, :, None]

    hp = x_prompt.reshape(bsz * seq, D_MODEL)
    hs = x_sample.reshape(nb, D_MODEL)
    outs_p, outs_s = [], []
    for l in range(depth):
        p = _layer_params(l, *weights)
        conv_prm = (p['bw'], p['cw'], p['cb'], p['lng'], p['lnb'], p['dw'], p['db'], p['dtb'])
        ssm_prm = (p['alog'], p['dskip'], p['dnorm'])

        q, k, kb, v, vt, qi, ki4, kcat, wi, dt, bgc, glu, zg, xbc = _inproj(
            hp, p['g_pre_mix'], p['wm'], p['wih'], p['wil'], tabs_p, 256)
        ya = _attn_prompt_t(q, qi, wi, kcat, kb, vt, bsz, seq)
        yb, yc, yd, nbp, ncp, ndp, ssm_p = _mix_prompt(bgc, glu, zg, xbc, dt, conv_prm + ssm_prm, bsz, seq)
        hp = _finish(p, ya, yb, yc, yd, hp, 512)
        outs_p.append((k.reshape(bsz, seq, N_HEADS, HEAD_DIM), v.reshape(bsz, seq, N_HEADS, HEAD_DIM),
                       ki4[:, :IDX_DIM].reshape(bsz, seq, IDX_DIM), nbp, ncp, ndp,
                       ssm_p.reshape(bsz, N_HEADS, HEAD_DIM, D_STATE)))

        q, k, kb, v, vt, qi, ki4, kcat, wi, dt, bgc, glu, zg, xbc = _inproj(
            hs, p['g_pre_mix'], p['wm'], p['wih'], p['wil'], tabs_s, nb)
        topk = min(TOPK_MAX, (past + t_dec) // 4)
        qi_s = qi.reshape(nb, N_IDX_HEADS, IDX_DIM)
        wi_t = jnp.broadcast_to((wi[:, :N_IDX_HEADS] * (IDX_DIM ** -0.5 * N_IDX_HEADS ** -0.5))[:, :, None],
                                (nb, N_IDX_HEADS, 128))
        sc = _dec_scores(pt, qi_s, wi_t, cki, n_pages, l * n_phys).reshape(nb, past)
        bias, bnew = _dec_select(sc, qi, ki4, wi, topk)
        qm = jnp.where(lane_head, q[:, None, :], 0.0)
        ya = _dec_attn(pt, qm, k.reshape(nb, 1, 256), v.reshape(nb, 1, 256), bias.reshape(nb, 1, past),
                       bnew.reshape(nb, 1, 128), ck, cv, n_pages, l * n_phys).reshape(nb, 256)
        yb, yc, act, dtv, nbs, ncs, nds = _dec_conv(
            bgc, glu, xbc, dt, state_bconv[l].reshape(nb, -1), state_cconv[l].reshape(nb, -1),
            state_dconv[l].reshape(nb, -1), conv_prm)
        yd, ssm_s = _dec_ssm(act.reshape(nb, 1, D_XBC), dtv.reshape(nb, 1, 128), zg.reshape(nb, 1, 256),
                             state_ssm[l].reshape(nb, 256, D_STATE), *ssm_prm)
        hs = _finish(p, ya, yb, yc, yd.reshape(nb, 256), hs, nb)
        outs_s.append((k.reshape(nb, 1, N_HEADS, HEAD_DIM), v.reshape(nb, 1, N_HEADS, HEAD_DIM),
                       ki4[:, :IDX_DIM].reshape(nb, 1, IDX_DIM), nbs.reshape(nb, 2, 256), ncs.reshape(nb, 30, 256),
                       nds.reshape(nb, 3, D_XBC), ssm_s.reshape(nb, N_HEADS, HEAD_DIM, D_STATE)))

    k_p, v_p, kidx_p, bconv_p, cconv_p, dconv_p, ssm_p = [jnp.stack(a) for a in zip(*outs_p)]
    k_s, v_s, kidx_s, bconv_s, cconv_s, dconv_s, ssm_s = [jnp.stack(a) for a in zip(*outs_s)]
    return (hp.reshape(bsz, seq, D_MODEL), hs.reshape(nb, t_dec, D_MODEL), k_p, v_p, kidx_p, k_s, v_s, kidx_s,
            bconv_p, bconv_s, cconv_p, cconv_s, dconv_p, dconv_s, ssm_p, ssm_s)
```

```python
import functools
import math

import jax
import jax.numpy as jnp
import numpy as np
from jax import lax
from jax.experimental import pallas as pl
from jax.experimental.pallas import tpu as pltpu

f32, bf16, i32 = jnp.float32, jnp.bfloat16, jnp.int32
HI = lax.Precision.HIGHEST

D_MODEL = 1024
PAGE = 128
GW = 256
HEAD_DIM = 64
N_HEADS = 4
N_IDX_HEADS = 8
IDX_DIM = 32
TOPK_MAX = 256
ROPE_THETA = 500000.0
C_CONV = 31
B_CONV = 3
D_CONV = 4
D_XBC = 768
D_STATE = 128
SSD_CHUNK = 128
D_FF = 2816
RMS_EPS = 1e-6
LN_EPS = 1e-5
IN_SIZES = (256, 256, 256, 256, 32, 8, 256, 256, 256, 512, 256, 768, 4)
INT_MIN = -(2 ** 31)
KEY_NEG_INF = INT_MIN + 0x7FFFFF
NEG = -1e30
VMEM_LIMIT = 56 * 1024 * 1024


def _dot_nt(a, b, prec=None):
    return lax.dot_general(a, b, (((1,), (1,)), ((), ())), precision=prec, preferred_element_type=f32)


def _dot(a, b, prec=None):
    return jnp.dot(a, b, precision=prec, preferred_element_type=f32)


def _cparams(sem):
    return pltpu.CompilerParams(dimension_semantics=sem, vmem_limit_bytes=VMEM_LIMIT)


def _rms(x, g):
    return x * lax.rsqrt(jnp.mean(x * x, axis=-1, keepdims=True) + RMS_EPS) * g


def _silu(x):
    return x * jax.nn.sigmoid(x)


def _key_to_float(key):
    return lax.bitcast_convert_type(jnp.where(key < 0, key ^ jnp.int32(0x7FFFFFFF), key), f32)


def _split_f32(x):
    c = x * (2.0 ** 16 + 1.0)
    hi = c - (c - x)
    return hi, x - hi


def _split(x):
    hi, lo = _split_f32(x)
    return hi.astype(bf16), lo.astype(bf16)


BISECT_CAP = 40
BISECT_STRIDE = 4


def _bisect_threshold(count_ge, smin, smax, short, topk):
    state = jnp.where(short > 0.0, 1.0, 0.0)
    thr = jnp.full(smin.shape, jnp.finfo(f32).min, f32)

    def cond(c):
        return (c[1] > 0.0) & (c[2] < BISECT_CAP)

    def step(_, c):
        lo, hi, thr, state = c
        mid = lo * 0.5 + hi * 0.5
        closed = (mid <= lo) | (mid >= hi)
        cnt = count_ge(mid)
        live = state == 0.0
        hit = live & (cnt == topk)
        thr = jnp.where(hit, mid, thr)
        state = jnp.where(hit, 1.0, jnp.where(live & closed, 2.0, state))
        ge = cnt >= topk
        return jnp.where(ge, mid, lo), jnp.where(ge, hi, mid), thr, state

    def body(c):
        vec = lax.fori_loop(0, BISECT_STRIDE, step, c[0])
        return vec, jnp.max(jnp.where(vec[3] == 0.0, 1.0, 0.0)), c[2] + BISECT_STRIDE

    init = ((smin, smax, thr, state), jnp.max(jnp.where(state == 0.0, 1.0, 0.0)), jnp.int32(0))
    (_, _, thr, state), _, _ = lax.while_loop(cond, body, init)
    return thr, jnp.where(state == 1.0, 1.0, 0.0)


def _kth_largest(count_ge, shape, topk):
    def bit_step(bi, key):
        cand = key + lax.shift_left(jnp.int32(1), 31 - bi)
        ok = (cand <= KEY_NEG_INF) | (count_ge(_key_to_float(cand)) >= topk)
        return jnp.where(ok, cand, key)

    key = lax.fori_loop(0, 32, bit_step, jnp.full(shape, INT_MIN, i32))
    return key, _key_to_float(key)


def _refine_between_floats(count_ge, lo, key, rows, topk, steps=40):
    hi = _key_to_float(key + 1)

    def step(_, carry):
        lo, hi = carry
        mid = lo + (hi - lo) * 0.5
        ge = count_ge(mid) >= topk
        return jnp.where(rows & ge, mid, lo), jnp.where(rows & jnp.logical_not(ge), mid, hi)

    return lax.fori_loop(0, steps, step, (lo, hi))[0]


def _rope(v, c, sm, sp, half):
    outs = []
    for s in range(v.shape[1] // 128):
        xs = v[:, s * 128:(s + 1) * 128]
        outs.append(xs * c + pltpu.roll(xs, 128 - half, 1) * sm + pltpu.roll(xs, half, 1) * sp)
    return jnp.concatenate(outs, axis=1)


def _inproj_kernel(x_ref, g_ref, wm_ref, wih_ref, wil_ref, c64_ref, sm64_ref, sp64_ref, c32_ref, sm32_ref, sp32_ref,
                   q_ref, k_ref, kb_ref, v_ref, vt_ref, qi_ref, ki_ref, kcat_ref, wi_o_ref, dt_o_ref,
                   bgc_ref, glu_ref, zg_ref, xbc_ref):
    u = _rms(x_ref[...], g_ref[...])
    ub, ul = _split(u)

    def mm(c0, c1):
        return _dot(ub, wm_ref[:, c0:c1])

    def mm3(c0, c1):
        wh = wih_ref[:, c0:c1]
        return _dot(ub, wh) + _dot(ul, wh) + _dot(ub, wil_ref[:, c0:c1])

    c64, sm64, sp64 = c64_ref[...], sm64_ref[...], sp64_ref[...]
    q_ref[...] = _rope(mm(0, 256), c64, sm64, sp64, 8)
    k = _rope(mm(256, 512), c64, sm64, sp64, 8)
    k_ref[...] = k
    kb_ref[...] = k.astype(bf16)
    v = mm(512, 768)
    v_ref[...] = v
    vt_ref[...] = v.T.astype(bf16)
    bgc_ref[...] = mm(768, 1536)
    glu_ref[...] = mm(1536, 2048)
    zg_ref[...] = mm(2048, 2304)
    xbc_ref[...] = mm(2304, 3072)
    c32, sm32, sp32 = c32_ref[...], sm32_ref[...], sp32_ref[...]
    qi_ref[...] = _rope(mm3(0, 256), c32, sm32, sp32, 4)
    ki4 = _rope(mm3(256, 384), c32, sm32, sp32, 4)
    ki_ref[...] = ki4
    kh, kl = _split(ki4)
    lane = lax.broadcasted_iota(i32, ki4.shape, 1)
    kcat_ref[...] = jnp.where((lane >= IDX_DIM) & (lane < 2 * IDX_DIM), kl, kh)
    small = mm3(384, 512)
    wi_o_ref[...] = small
    dt_o_ref[...] = pltpu.roll(small, 128 - N_IDX_HEADS, 1)


def _inproj(x, g, wm, wih, wil, tabs, tm):
    t = x.shape[0]
    nt = t // tm
    ntab = tabs[0].shape[0] // tm
    row = lambda w: pl.BlockSpec((tm, w), lambda i: (i, 0))
    full = lambda a: pl.BlockSpec(a.shape, lambda i: (0,) * a.ndim)
    tab = pl.BlockSpec((tm, 128), lambda i: (i % ntab, 0))
    widths = (256, 256, 256, 256, 256, 256, 128, 128, 128, 128, 768, 512, 256, 768)
    dtypes = (f32, f32, bf16, f32, bf16, f32, f32, bf16, f32, f32, f32, f32, f32, f32)
    out_specs = [row(w) for w in widths]
    out_shape = [jax.ShapeDtypeStruct((t, w), d) for w, d in zip(widths, dtypes)]
    out_specs[4] = pl.BlockSpec((256, tm), lambda i: (0, i))
    out_shape[4] = jax.ShapeDtypeStruct((256, t), bf16)
    return pl.pallas_call(
        _inproj_kernel,
        grid=(nt,),
        in_specs=[row(D_MODEL), full(g), full(wm), full(wih), full(wil)] + [tab] * 6,
        out_specs=out_specs,
        out_shape=out_shape,
        compiler_params=_cparams(("arbitrary",)),
        name="inproj",
    )(x, g, wm, wih, wil, *tabs)


QB = 256
SUB = 128


def _attn_t_kernel(q_ref, qi_ref, wi_ref, kcat_ref, k_ref, vt_ref, o_ref, sc_ref, cat_ref, qm_ref, acc_ref, *, topk):
    i = pl.program_id(1)
    nkb = i + 1
    n_sub = QB // SUB
    hi, lo = _split_f32(qi_ref[...])
    hi_t, lo_t = hi.T, lo.T
    for h in range(N_IDX_HEADS):
        rs = slice(h * IDX_DIM, (h + 1) * IDX_DIM)
        cat_ref[h] = jnp.concatenate([hi_t[rs], hi_t[rs], lo_t[rs], jnp.zeros((IDX_DIM, QB), f32)], axis=0).astype(bf16)
    q_t = (q_ref[...] * (HEAD_DIM ** -0.5)).T
    row = lax.broadcasted_iota(i32, (256, QB), 0)
    for h in range(N_HEADS):
        qm_ref[:, h * QB:(h + 1) * QB] = jnp.where(row // HEAD_DIM == h, q_t, 0.0).astype(bf16)
    w8 =(wi_ref[...] * (IDX_DIM ** -0.5 * N_IDX_HEADS ** -0.5)).T[0:N_IDX_HEADS]
    qpos = i * QB + lax.broadcasted_iota(i32, (1, QB), 1)
    kio = lax.broadcasted_iota(i32, (SUB, 1), 0)

    def tiles(kb):
        return [pl.ds(pl.multiple_of(kb * QB + j * SUB, SUB), SUB) for j in range(n_sub)]

    fold8 = lambda x, op: functools.reduce(op, [x[8 * g:8 * g + 8] for g in range(x.shape[0] // 8)])

    def score_block(kb, carry):
        mn, mx = carry
        for j, sl in enumerate(tiles(kb)):
            kc = kcat_ref[sl, :]
            acc = jnp.zeros((SUB, QB), f32)
            for h in range(N_IDX_HEADS):
                acc = acc + jnp.maximum(_dot(kc, cat_ref[h]), 0.0) * w8[h:h + 1, :]
            ok = kb * QB + j * SUB + kio <= qpos
            lowm = jnp.where(ok, acc, -jnp.inf)
            sc_ref[sl, :] = lowm
            mx = jnp.maximum(mx, fold8(lowm, jnp.maximum))
            mn = jnp.minimum(mn, fold8(jnp.where(ok, acc, jnp.inf), jnp.minimum))
        return mn, mx

    mn, mx = lax.fori_loop(0, nkb, score_block, (jnp.full((8, QB), jnp.inf, f32), jnp.full((8, QB), -jnp.inf, f32)))
    smin = jnp.min(mn, axis=0, keepdims=True)
    smax = jnp.max(mx, axis=0, keepdims=True)

    def count(pred):
        def body(kb, cnt):
            for sl in tiles(kb):
                cnt = cnt + jnp.where(pred(sc_ref[sl, :]), 1.0, 0.0).reshape(SUB // 8, 8, QB).sum(axis=0)
            return cnt
        return jnp.sum(lax.fori_loop(0, nkb, body, jnp.zeros((8, QB), f32)), axis=0, keepdims=True)

    def count_ge(t):
        return count(lambda s: s >= t)

    short = jnp.where(qpos < topk, 1.0, 0.0)
    thr_fast, settled = _bisect_threshold(count_ge, smin, smax, short, topk)
    all_settled = jnp.min(settled) > 0.0

    @pl.when(all_settled)
    def _():
        def mask_block(kb, carry):
            for sl in tiles(kb):
                sc_ref[sl, :] = jnp.where(sc_ref[sl, :] >= thr_fast, 0.0, NEG)
            return carry

        lax.fori_loop(0, nkb, mask_block, 0)

    @pl.when(jnp.logical_not(all_settled))
    def _():
        thr_key, thr = _kth_largest(count_ge, (1, QB), topk)
        real = thr > -jnp.inf
        tie = real & (count_ge(thr) > topk)
        thr2 = _refine_between_floats(count_ge, thr, thr_key, tie, topk)
        thr2 = jnp.where(settled > 0.0, thr_fast, thr2)
        need = topk - count(lambda s: s > thr2)
        realf = jnp.where(real, 1.0, 0.0)
        r = lax.broadcasted_iota(i32, (SUB, SUB), 0)
        c = lax.broadcasted_iota(i32, (SUB, SUB), 1)
        tri = jnp.where(c <= r, 1.0, 0.0).astype(bf16)

        def tie_block(kb, carry):
            for sl in tiles(kb):
                s = sc_ref[sl, :]
                eqb = jnp.where(s == thr2, realf, 0.0).astype(bf16)
                cum = _dot(tri, eqb) + carry
                keep = jnp.where(s > thr2, 1.0, jnp.where(cum <= need, eqb.astype(f32), 0.0))
                sc_ref[sl, :] = jnp.where(keep > 0.0, 0.0, NEG)
                carry = cum[SUB - 1:SUB, :]
            return carry

        lax.fori_loop(0, nkb, tie_block, jnp.zeros((1, QB), f32))

    heads = [slice(h * HEAD_DIM, (h + 1) * HEAD_DIM) for h in range(N_HEADS)]

    def attn_block(kb, carry):
        ms, ls = list(carry[0]), list(carry[1])
        for sl in [pl.ds(pl.multiple_of(kb * QB, QB), QB)]:
            s_all = _dot(k_ref[sl, :], qm_ref[...])
            mask = sc_ref[sl, :]
            alphas, ps = [], []
            for h in range(N_HEADS):
                s = s_all[:, h * QB:(h + 1) * QB] + mask
                m_new = jnp.maximum(ms[h], jnp.max(s, axis=0, keepdims=True))
                alphas.append(jnp.exp(ms[h] - m_new))
                p = jnp.exp(s - m_new)
                ls[h] = alphas[h] * ls[h] + jnp.sum(p, axis=0, keepdims=True)
                ps.append(p.astype(bf16))
                ms[h] = m_new
            pvs = [_dot(vt_ref[heads[h], sl], ps[h]) for h in range(N_HEADS)]
            for h in range(N_HEADS):
                acc_ref[heads[h], :] = alphas[h] * acc_ref[heads[h], :] + pvs[h]
        return tuple(ms), tuple(ls)

    acc_ref[...] = jnp.zeros_like(acc_ref)
    m0 = tuple(jnp.full((1, QB), NEG, f32) for _ in range(N_HEADS))
    l0 = tuple(jnp.zeros((1, QB), f32) for _ in range(N_HEADS))
    _, ls = lax.fori_loop(0, nkb, attn_block, (m0, l0))
    out_t = jnp.concatenate([acc_ref[h * HEAD_DIM:(h + 1) * HEAD_DIM, :] / ls[h] for h in range(N_HEADS)], axis=0)
    o_ref[...] = out_t.T


def _attn_prompt_t(q, qi, wi, kcat, kb, vt, bsz, seq):
    nq = seq // QB
    topk = min(TOPK_MAX, seq // 4)
    qrow = lambda w: pl.BlockSpec((QB, w), lambda b, i: (b * nq + i, 0))
    seqblk = lambda w: pl.BlockSpec((seq, w), lambda b, i: (b, 0))
    return pl.pallas_call(
        functools.partial(_attn_t_kernel, topk=float(topk)),
        grid=(bsz, nq),
        in_specs=[qrow(256), qrow(256), qrow(128), seqblk(128), seqblk(256),
                  pl.BlockSpec((256, seq), lambda b, i: (0, b))],
        out_specs=qrow(256),
        out_shape=jax.ShapeDtypeStruct((bsz * seq, 256), f32),
        scratch_shapes=[pltpu.VMEM((seq, QB), f32), pltpu.VMEM((N_IDX_HEADS, 128, QB), bf16),
                        pltpu.VMEM((256, N_HEADS * QB), bf16), pltpu.VMEM((256, QB), f32)],
        compiler_params=_cparams(("arbitrary", "arbitrary")),
        name="attn_prompt",
    )(q, qi, wi, kcat, kb, vt)


TC = 256
HALO_B, HALO_C, HALO_D = 8, 32, 8


def _per_head_lanes(a):
    lane = lax.broadcasted_iota(i32, (a.shape[0], N_HEADS * HEAD_DIM), 1)
    out = jnp.broadcast_to(a[:, N_HEADS - 1:N_HEADS], lane.shape)
    for h in range(N_HEADS - 2, -1, -1):
        out = jnp.where(lane < (h + 1) * HEAD_DIM, a[:, h:h + 1], out)
    return out


def _cumsum_rows(x):
    n = x.shape[0]
    r = lax.broadcasted_iota(i32, (n, n), 0)
    c = lax.broadcasted_iota(i32, (n, n), 1)
    tril = jnp.where(c <= r, 1.0, 0.0).astype(bf16)
    p1, rest = _split_f32(x)
    p2, p3 = _split_f32(rest)
    return _dot(tril, p1.astype(bf16)) + _dot(tril, p2.astype(bf16)) + _dot(tril, p3.astype(bf16))


def _ssd_chunk(xs, bm, cm, dtv, a_row, hcat_ref):
    r = lax.broadcasted_iota(i32, (128, 128), 0)
    c = lax.broadcasted_iota(i32, (128, 128), 1)
    causal = c <= r
    lane = lax.broadcasted_iota(i32, (128, 256), 1)
    da = dtv * a_row
    cs = _cumsum_rows(da)
    cs_t = cs.T
    dt_x = _per_head_lanes(dtv)
    ecs_x = _per_head_lanes(jnp.exp(cs))
    cs_last = cs[127:128, :]
    wend_x = _per_head_lanes(jnp.exp(cs_last - cs) * dtv)
    xdt = (xs * dt_x).astype(bf16)
    bmb, cmb = bm.astype(bf16), cm.astype(bf16)
    hb = hcat_ref[...].astype(bf16)
    y = jnp.zeros((128, 256), f32)
    ystate = []
    for g in range(2):
        cg = cmb[:, g * 128:(g + 1) * 128]
        cb = _dot_nt(cg, bmb[:, g * 128:(g + 1) * 128])
        ystate.append(_dot_nt(cg, hb))
        for h in (2 * g, 2 * g + 1):
            seg = cs[:, h:h + 1] - cs_t[h:h + 1, :]
            dec = jnp.where(causal, jnp.exp(jnp.where(causal, seg, 0.0)), 0.0)
            yh = _dot((cb * dec).astype(bf16), xdt)
            y = jnp.where(lane // 64 == h, yh, y)
    y = y + jnp.where(lane < 128, ystate[0], ystate[1]) * ecs_x
    xw_t = (xs * wend_x).T.astype(bf16)
    upd = jnp.concatenate([_dot(xw_t[0:128], bmb[:, 0:128]), _dot(xw_t[128:256], bmb[:, 128:256])], axis=0)
    elast = jnp.exp(cs_last)
    dcol = jnp.concatenate([jnp.broadcast_to(elast[:, h:h + 1], (HEAD_DIM, D_STATE)) for h in range(N_HEADS)], axis=0)
    hcat_ref[...] = hcat_ref[...] * dcol + upd
    return y


def _gated_rms(ys, xs, zg, dskip, dnorm):
    yg = (ys + dskip * xs) * _silu(zg)
    return yg * lax.rsqrt(jnp.mean(yg * yg, axis=-1, keepdims=True) + RMS_EPS) * dnorm


def _layer_norm_silu(x, g, b):
    mu = jnp.mean(x, axis=-1, keepdims=True)
    var = jnp.mean(jnp.square(x - mu), axis=-1, keepdims=True)
    return _silu((x - mu) * lax.rsqrt(var + LN_EPS) * g + b)


def _mix_kernel(bgc_ref, glu_ref, zg_ref, xbc_ref, dt_ref,
                bw_ref, cw_ref, cb_ref, lng_ref, lnb_ref, dw_ref, db_ref, dtb_ref, alog_ref, dskip_ref, dnorm_ref,
                yb_ref, yc_ref, yd_ref, nb_ref, nc_ref, nd_ref, ssm_ref,
                eb_ref, ec_ref, ed_ref, hcat_ref):
    i = pl.program_id(1)

    @pl.when(i == 0)
    def _():
        eb_ref[0:HALO_B, :] = jnp.zeros((HALO_B, 256), f32)
        ec_ref[0:HALO_C, :] = jnp.zeros((HALO_C, 256), f32)
        ed_ref[0:HALO_D, :] = jnp.zeros((HALO_D, D_XBC), f32)
        hcat_ref[...] = jnp.zeros_like(hcat_ref)

    bgc = bgc_ref[...]
    eb_ref[HALO_B:HALO_B + TC, :] = bgc[:, 512:768] * bgc[:, 0:256]
    conv = jnp.zeros((TC, 256), f32)
    for k in range(B_CONV):
        conv = conv + bw_ref[k:k + 1, :] * eb_ref[pl.ds(HALO_B - (B_CONV - 1) + k, TC), :]
    yb_ref[...] = bgc[:, 256:512] * conv
    nb_ref[0] = eb_ref[HALO_B + TC - (B_CONV - 1):HALO_B + TC, :]
    eb_ref[0:HALO_B, :] = eb_ref[TC:TC + HALO_B, :]

    glu = glu_ref[...]
    ec_ref[HALO_C:HALO_C + TC, :] = glu[:, 0:256] * jax.nn.sigmoid(glu[:, 256:512])
    conv = jnp.zeros((TC, 256), f32)
    for k in range(C_CONV):
        conv = conv + cw_ref[k:k + 1, :] * ec_ref[pl.ds(HALO_C - (C_CONV - 1) + k, TC), :]
    yc_ref[...] = _layer_norm_silu(conv + cb_ref[...], lng_ref[...], lnb_ref[...])
    nc_ref[0] = ec_ref[HALO_C + TC - (C_CONV - 1):HALO_C + TC, :]
    ec_ref[0:HALO_C, :] = ec_ref[TC:TC + HALO_C, :]

    ed_ref[HALO_D:HALO_D + TC, :] = xbc_ref[...]
    conv = jnp.zeros((TC, D_XBC), f32)
    for k in range(D_CONV):
        conv = conv + dw_ref[k:k + 1, :] * ed_ref[pl.ds(HALO_D - (D_CONV - 1) + k, TC), :]
    act = _silu(conv + db_ref[...])
    nd_ref[0] = ed_ref[HALO_D + TC - (D_CONV - 1):HALO_D + TC, :]
    ed_ref[0:HALO_D, :] = ed_ref[TC:TC + HALO_D, :]
    l128 = lax.broadcasted_iota(i32, (1, 128), 1)
    a_row = jnp.where(l128 < N_HEADS, -jnp.exp(alog_ref[...]), 0.0)
    dtv = jax.nn.softplus(dt_ref[...] + dtb_ref[...])
    zg = zg_ref[...]
    for j in range(TC // SSD_CHUNK):
        rs = slice(j * SSD_CHUNK, (j + 1) * SSD_CHUNK)
        xs = act[rs, 0:256]
        y = _ssd_chunk(xs, act[rs, 256:512], act[rs, 512:768], dtv[rs], a_row, hcat_ref)
        yd_ref[rs, :] = _gated_rms(y, xs, zg[rs], dskip_ref[...], dnorm_ref[...])
    ssm_ref[0] = hcat_ref[...]


def _mix_prompt(bgc, glu, zg, xbc, dt, prm, bsz, seq):
    nt = seq // TC
    row = lambda w: pl.BlockSpec((TC, w), lambda b, i: (b * nt + i, 0))
    full = lambda a: pl.BlockSpec(a.shape, lambda b, i: (0,) * a.ndim)
    st = lambda r, w: pl.BlockSpec((1, r, w), lambda b, i: (b, 0, 0))
    t = bsz * seq
    return pl.pallas_call(
        _mix_kernel,
        grid=(bsz, nt),
        in_specs=[row(768), row(512), row(256), row(768), row(128)] + [full(a) for a in prm],
        out_specs=[row(256), row(256), row(256), st(2, 256), st(30, 256), st(3, D_XBC), st(256, 128)],
        out_shape=[jax.ShapeDtypeStruct((t, 256), f32)] * 3 + [
            jax.ShapeDtypeStruct((bsz, 2, 256), f32), jax.ShapeDtypeStruct((bsz, 30, 256), f32),
            jax.ShapeDtypeStruct((bsz, 3, D_XBC), f32), jax.ShapeDtypeStruct((bsz, 256, 128), f32)],
        scratch_shapes=[pltpu.VMEM((HALO_B + TC, 256), f32), pltpu.VMEM((HALO_C + TC, 256), f32),
                        pltpu.VMEM((HALO_D + TC, D_XBC), f32), pltpu.VMEM((256, 128), f32)],
        compiler_params=_cparams(("arbitrary", "arbitrary")),
        name="mix_prompt",
    )(bgc, glu, zg, xbc, dt, *prm)


def _outproj_kernel(ya_ref, yb_ref, yc_ref, yd_ref, x_ref, w_ref, gpost_ref, gpre_ref, x1_ref, hf_ref):
    mix = _dot(ya_ref[...].astype(bf16), w_ref[0:256, :])
    mix = mix + _dot(yb_ref[...].astype(bf16), w_ref[256:512, :])
    mix = mix + _dot(yc_ref[...].astype(bf16), w_ref[512:768, :])
    mix = mix + _dot(yd_ref[...].astype(bf16), w_ref[768:1024, :])
    x1 = x_ref[...] + _rms(mix, gpost_ref[...])
    x1_ref[...] = x1
    hf_ref[...] = _rms(x1, gpre_ref[...]).astype(bf16)


def _outproj(ya, yb, yc, yd, x, w, gpost, gpre, tm):
    t = x.shape[0]
    row = lambda w_: pl.BlockSpec((tm, w_), lambda i: (i, 0))
    full = lambda a: pl.BlockSpec(a.shape, lambda i: (0,) * a.ndim)
    return pl.pallas_call(
        _outproj_kernel,
        grid=(t // tm,),
        in_specs=[row(256)] * 4 + [row(D_MODEL), full(w), full(gpost), full(gpre)],
        out_specs=[row(D_MODEL), row(D_MODEL)],
        out_shape=[jax.ShapeDtypeStruct((t, D_MODEL), f32), jax.ShapeDtypeStruct((t, D_MODEL), bf16)],
        compiler_params=_cparams(("arbitrary",)),
        name="outproj",
    )(ya, yb, yc, yd, x, w, gpost, gpre)


FF_CHUNK = 1408


def _ffn_kernel(hf_ref, x1_ref, wg_ref, wu_ref, wd_ref, g_ref, o_ref, acc_ref):
    j = pl.program_id(1)
    hf = hf_ref[...]
    a = _silu(_dot(hf, wg_ref[...])) * _dot(hf, wu_ref[...])
    part = _dot(a.astype(bf16), wd_ref[...])

    @pl.when(j == 0)
    def _():
        acc_ref[...] = part

    @pl.when(j > 0)
    def _():
        acc_ref[...] = acc_ref[...] + part

    @pl.when(j == pl.num_programs(1) - 1)
    def _():
        o_ref[...] = x1_ref[...] + _rms(acc_ref[...], g_ref[...])


def _ffn(hf, x1, wg, wu, wd, g, tm):
    t = x1.shape[0]
    nj = D_FF // FF_CHUNK
    row = lambda: pl.BlockSpec((tm, D_MODEL), lambda i, j: (i, 0))
    return pl.pallas_call(
        _ffn_kernel,
        grid=(t // tm, nj),
        in_specs=[row(), row(), pl.BlockSpec((D_MODEL, FF_CHUNK), lambda i, j: (0, j)),
                  pl.BlockSpec((D_MODEL, FF_CHUNK), lambda i, j: (0, j)),
                  pl.BlockSpec((FF_CHUNK, D_MODEL), lambda i, j: (j, 0)),
                  pl.BlockSpec((1, D_MODEL), lambda i, j: (0, 0))],
        out_specs=row(),
        out_shape=jax.ShapeDtypeStruct((t, D_MODEL), f32),
        scratch_shapes=[pltpu.VMEM((tm, D_MODEL), f32)],
        compiler_params=_cparams(("arbitrary", "arbitrary")),
        name="ffn",
    )(hf, x1, wg, wu, wd, g)


def _page_copies(pt_ref, src_hbm, dst, sem, b, n_pages, base):
    return [pltpu.make_async_copy(src_hbm.at[base + pt_ref[b * n_pages + p]],
                                  dst.at[:, pl.ds(p * PAGE, PAGE)], sem) for p in range(n_pages)]


def _dec_score_kernel(pt_ref, qi_ref, wi_ref, cki_hbm, o_ref, buf, sem, *, n_pages, base):
    b = pl.program_id(0)
    nb = pl.num_programs(0)

    def copies(bb, slot):
        return _page_copies(pt_ref, cki_hbm, buf.at[slot], sem.at[slot], bb, n_pages, base)

    @pl.when(b == 0)
    def _():
        for c in copies(0, 0):
            c.start()

    @pl.when(b + 1 < nb)
    def _():
        for c in copies(b + 1, (b + 1) % 2):
            c.start()

    slot = b % 2
    for c in copies(b, slot):
        c.wait()
    s = _dot(qi_ref[b], buf[slot], HI)
    w = jnp.tile(wi_ref[b], (1, n_pages))
    o_ref[0] = jnp.sum(jnp.maximum(s, 0.0) * w, axis=0, keepdims=True)


def _dec_scores(pt, qi_s, wi_t, cki, n_pages, base):
    nb = qi_s.shape[0]
    past = n_pages * PAGE
    gs = pltpu.PrefetchScalarGridSpec(
        num_scalar_prefetch=1, grid=(nb,),
        in_specs=[pl.BlockSpec(qi_s.shape, lambda b, pt_: (0, 0, 0)), pl.BlockSpec(wi_t.shape, lambda b, pt_: (0, 0, 0)),
                  pl.BlockSpec(memory_space=pl.ANY)],
        out_specs=pl.BlockSpec((1, 1, past), lambda b, pt_: (b, 0, 0)),
        scratch_shapes=[pltpu.VMEM((2, IDX_DIM, past), f32), pltpu.SemaphoreType.DMA((2,))])
    return pl.pallas_call(
        functools.partial(_dec_score_kernel, n_pages=n_pages, base=base),
        grid_spec=gs, out_shape=jax.ShapeDtypeStruct((nb, 1, past), f32),
        compiler_params=_cparams(("arbitrary",)), name="dec_scores",
    )(pt, qi_s, wi_t, cki)


def _dec_select_kernel(sc_ref, qi_ref, ki_ref, wi_ref, bias_ref, bnew_ref, *, topk):
    nb, past = sc_ref.shape
    wi = wi_ref[...] * (IDX_DIM ** -0.5 * N_IDX_HEADS ** -0.5)
    gj = lax.broadcasted_iota(i32, (256, 128), 0)
    gh = lax.broadcasted_iota(i32, (256, 128), 1)
    seg = jnp.where(gj // IDX_DIM == gh, 1.0, 0.0)
    ki = jnp.concatenate([ki_ref[...], ki_ref[...]], axis=1)
    s_new = _dot(qi_ref[...] * ki, seg, HI)
    sc_new = jnp.broadcast_to(jnp.sum(jnp.maximum(s_new, 0.0) * wi, axis=-1, keepdims=True), (nb, 128))
    sc = sc_ref[...]
    ones = jnp.ones((past, 128), bf16)
    wide = lambda t: jnp.concatenate([t] * (past // 128), axis=1)

    def count(pred_past, pred_new):
        return _dot(jnp.where(pred_past, 1.0, 0.0).astype(bf16), ones) + jnp.where(pred_new, 1.0, 0.0)

    def count_ge(t):
        return count(sc >= wide(t), sc_new >= t)

    thr_key, thr = _kth_largest(count_ge, (nb, 128), topk)
    bias_ref[...] = jnp.where(sc >= wide(thr), 0.0, NEG)
    bnew_ref[...] = jnp.where(sc_new >= thr, 0.0, NEG)
    tie = count_ge(thr) > topk

    @pl.when(jnp.max(jnp.where(tie, 1.0, 0.0)) > 0.0)
    def _():
        thr2 = _refine_between_floats(count_ge, thr, thr_key, tie, topk)
        need = topk - count(sc > wide(thr2), sc_new > thr2)
        eqb = jnp.where(sc == wide(thr2), 1.0, 0.0).astype(bf16)
        r = lax.broadcasted_iota(i32, (PAGE, PAGE), 0)
        c = lax.broadcasted_iota(i32, (PAGE, PAGE), 1)
        tri = jnp.where(r <= c, 1.0, 0.0).astype(bf16)
        carry = jnp.zeros((nb, 128), f32)
        for p in range(past // PAGE):
            sl = slice(p * PAGE, (p + 1) * PAGE)
            cum = _dot(eqb[:, sl], tri) + carry
            keep = jnp.where(sc[:, sl] > thr2, 1.0, jnp.where(cum <= need, eqb[:, sl].astype(f32), 0.0))
            bias_ref[:, sl] = jnp.where(keep > 0.0, 0.0, NEG)
            carry = carry + _dot(eqb[:, sl], ones[0:PAGE])
        keep_new = jnp.where(sc_new > thr2, 1.0, jnp.where((sc_new == thr2) & (carry + 1.0 <= need), 1.0, 0.0))
        bnew_ref[...] = jnp.where(keep_new > 0.0, 0.0, NEG)


def _dec_select(sc, qi, ki4, wi, topk):
    nb, past = sc.shape
    return pl.pallas_call(
        functools.partial(_dec_select_kernel, topk=float(topk)),
        out_shape=[jax.ShapeDtypeStruct((nb, past), f32), jax.ShapeDtypeStruct((nb, 128), f32)],
        compiler_params=pltpu.CompilerParams(vmem_limit_bytes=VMEM_LIMIT), name="dec_select",
    )(sc, qi, ki4, wi)


def _dec_attn_kernel(pt_ref, qm_ref, kn_ref, vn_ref, bias_ref, bnew_ref, ck_hbm, cv_hbm, o_ref,
                     kbuf, vbuf, sem, *, n_pages, base):
    b = pl.program_id(0)
    nb = pl.num_programs(0)

    def copies(bb, slot):
        return (_page_copies(pt_ref, ck_hbm, kbuf.at[slot], sem.at[0, slot], bb, n_pages, base)
                + _page_copies(pt_ref, cv_hbm, vbuf.at[slot], sem.at[1, slot], bb, n_pages, base))

    @pl.when(b == 0)
    def _():
        for c in copies(0, 0):
            c.start()

    @pl.when(b + 1 < nb)
    def _():
        for c in copies(b + 1, (b + 1) % 2):
            c.start()

    slot = b % 2
    for c in copies(b, slot):
        c.wait()
    qm = qm_ref[0] * (HEAD_DIM ** -0.5)
    s = _dot(qm.astype(bf16), kbuf[slot].astype(bf16)) + bias_ref[0]
    s_new = jnp.sum(qm * kn_ref[0], axis=-1, keepdims=True) + bnew_ref[0][:, 0:1]
    m = jnp.maximum(jnp.max(s, axis=-1, keepdims=True), s_new)
    p = jnp.exp(s - m)
    p_new = jnp.exp(s_new - m)
    den = jnp.sum(p, axis=-1, keepdims=True) + p_new
    out8 = (_dot_nt(p.astype(bf16), vbuf[slot].astype(bf16)) + p_new * vn_ref[0]) / den
    row = lax.broadcasted_iota(i32, (8, 256), 0)
    lane = lax.broadcasted_iota(i32, (8, 256), 1)
    o_ref[0] = jnp.sum(jnp.where(lane // HEAD_DIM == row, out8, 0.0), axis=0, keepdims=True)


def _dec_attn(pt, qm, kn, vn, bias, bnew, ck, cv, n_pages, base):
    nb = qm.shape[0]
    past = n_pages * PAGE
    per = lambda r, w: pl.BlockSpec((1, r, w), lambda b, pt_: (b, 0, 0))
    gs = pltpu.PrefetchScalarGridSpec(
        num_scalar_prefetch=1, grid=(nb,),
        in_specs=[per(8, 256), per(1, 256), per(1, 256), per(1, past), per(1, 128),
                  pl.BlockSpec(memory_space=pl.ANY), pl.BlockSpec(memory_space=pl.ANY)],
        out_specs=per(1, 256),
        scratch_shapes=[pltpu.VMEM((2, 256, past), f32), pltpu.VMEM((2, 256, past), f32),
                        pltpu.SemaphoreType.DMA((2, 2))])
    return pl.pallas_call(
        functools.partial(_dec_attn_kernel, n_pages=n_pages, base=base),
        grid_spec=gs, out_shape=jax.ShapeDtypeStruct((nb, 1, 256), f32),
        compiler_params=_cparams(("arbitrary",)), name="dec_attn",
    )(pt, qm, kn, vn, bias, bnew, ck, cv)


def _dec_conv_kernel(bgc_ref, glu_ref, xbc_ref, dt_ref, sb_ref, sc_ref, sd_ref,
                     bw_ref, cw_ref, cb_ref, lng_ref, lnb_ref, dw_ref, db_ref, dtb_ref,
                     yb_ref, yc_ref, act_ref, dtv_ref, nb_ref, nc_ref, nd_ref):
    bgc = bgc_ref[...]
    ub = bgc[:, 512:768] * bgc[:, 0:256]
    sb = sb_ref[...]
    conv = bw_ref[0:1, :] * sb[:, 0:256] + bw_ref[1:2, :] * sb[:, 256:512] + bw_ref[2:3, :] * ub
    yb_ref[...] = bgc[:, 256:512] * conv
    nb_ref[...] = jnp.concatenate([sb[:, 256:512], ub], axis=1)

    glu = glu_ref[...]
    uc = glu[:, 0:256] * jax.nn.sigmoid(glu[:, 256:512])
    conv = cw_ref[C_CONV - 1:C_CONV, :] * uc
    for k in range(C_CONV - 1):
        conv = conv + cw_ref[k:k + 1, :] * sc_ref[:, k * 256:(k + 1) * 256]
    yc_ref[...] = _layer_norm_silu(conv + cb_ref[...], lng_ref[...], lnb_ref[...])
    nc_ref[:, 0:(C_CONV - 2) * 256] = sc_ref[:, 256:(C_CONV - 1) * 256]
    nc_ref[:, (C_CONV - 2) * 256:(C_CONV - 1) * 256] = uc

    xbc = xbc_ref[...]
    conv = dw_ref[D_CONV - 1:D_CONV, :] * xbc
    for k in range(D_CONV - 1):
        conv = conv + dw_ref[k:k + 1, :] * sd_ref[:, k * D_XBC:(k + 1) * D_XBC]
    act_ref[...] = _silu(conv + db_ref[...])
    nd_ref[:, 0:(D_CONV - 2) * D_XBC] = sd_ref[:, D_XBC:(D_CONV - 1) * D_XBC]
    nd_ref[:, (D_CONV - 2) * D_XBC:(D_CONV - 1) * D_XBC] = xbc
    dtv_ref[...] = jax.nn.softplus(dt_ref[...] + dtb_ref[...])


def _dec_conv(bgc, glu, xbc, dt, sb, sc, sd, prm):
    nb = bgc.shape[0]
    shp = lambda w: jax.ShapeDtypeStruct((nb, w), f32)
    return pl.pallas_call(
        _dec_conv_kernel,
        out_shape=[shp(256), shp(256), shp(D_XBC), shp(128), shp(2 * 256), shp(30 * 256), shp(3 * D_XBC)],
        compiler_params=pltpu.CompilerParams(vmem_limit_bytes=VMEM_LIMIT), name="dec_conv",
    )(bgc, glu, xbc, dt, sb, sc, sd, *prm)


SSM_GROUP = 8


def _dec_ssm_kernel(act_ref, dtv_ref, zg_ref, h_ref, alog_ref, dskip_ref, dnorm_ref, yd_ref, hn_ref):
    l128 = lax.broadcasted_iota(i32, (1, 128), 1)
    a_row = jnp.where(l128 < N_HEADS, -jnp.exp(alog_ref[...]), 0.0)
    r = lax.broadcasted_iota(i32, (128, 256), 0)
    lane = lax.broadcasted_iota(i32, (128, 256), 1)
    r8 = lax.broadcasted_iota(i32, (8, 128), 0)
    l256 = lax.broadcasted_iota(i32, (1, 256), 1)
    zrow = jnp.zeros((1, 128), f32)
    group = range(act_ref.shape[0])
    xs_, cm_, lt_, rmat_ = [], [], [], []
    for g in group:
        act = act_ref[g]
        xs, bm, cm = act[:, 0:256], act[:, 256:512], act[:, 512:768]
        dtv = dtv_ref[g]
        dec_x = _per_head_lanes(jnp.exp(dtv * a_row))
        xdt = xs * _per_head_lanes(dtv)
        lrows = jnp.where((r == 0) & (lane < 128), xdt, 0.0) + jnp.where((r == 1) & (lane >= 128), xdt, 0.0)
        lt_.append((lrows + jnp.where(r == 2, dec_x, 0.0)).T)
        b0 = jnp.concatenate([bm[:, 0:128], zrow], axis=1)
        b1 = jnp.concatenate([bm[:, 128:256], zrow], axis=1)
        rmat_.append(jnp.where(r == 0, b0, 0.0) + jnp.where(r == 1, b1, 0.0)
                     + jnp.where((r == 2) & (lane >= 128), 1.0, 0.0))
        xs_.append(xs)
        cm_.append(cm)
    res_ = [_dot(lt_[g], rmat_[g], HI) for g in group]
    hn_ = [h_ref[g] * res_[g][:, 128:256] + res_[g][:, 0:128] for g in group]
    for g in group:
        hn_ref[g] = hn_[g]
    crows_ = [jnp.where(r8 == 0, cm_[g][:, 0:128], 0.0) + jnp.where(r8 == 1, cm_[g][:, 128:256], 0.0) for g in group]
    y8_ = [_dot_nt(crows_[g], hn_[g], HI) for g in group]
    for g in group:
        y = jnp.where(l256 < 128, y8_[g][0:1, :], y8_[g][1:2, :])
        yd_ref[g] = _gated_rms(y, xs_[g], zg_ref[g], dskip_ref[...], dnorm_ref[...])


def _dec_ssm(act, dtv, zg, h, alog, dskip, dnorm):
    nb = act.shape[0]
    per = lambda r, w: pl.BlockSpec((SSM_GROUP, r, w), lambda b: (b, 0, 0))
    full = lambda a: pl.BlockSpec(a.shape, lambda b: (0,) * a.ndim)
    return pl.pallas_call(
        _dec_ssm_kernel,
        grid=(nb // SSM_GROUP,),
        in_specs=[per(1, D_XBC), per(1, 128), per(1, 256), per(256, 128), full(alog), full(dskip), full(dnorm)],
        out_specs=[per(1, 256), per(256, 128)],
        out_shape=[jax.ShapeDtypeStruct((nb, 1, 256), f32), jax.ShapeDtypeStruct((nb, 256, 128), f32)],
        compiler_params=_cparams(("arbitrary",)), name="dec_ssm",
    )(act, dtv, zg, h, alog, dskip, dnorm)


def _rope_tables(pos, head_dim):
    rot = head_dim // 4
    half = rot // 2
    inv = ROPE_THETA ** (-jnp.arange(half, dtype=f32) * 2.0 / rot)
    ang = pos.astype(f32)[:, None] * inv[None, :]
    cos, sin = jnp.cos(ang), jnp.sin(ang)
    n = pos.shape[0]
    pad = jnp.zeros((n, head_dim - rot), f32)
    c = jnp.concatenate([cos, cos, pad + 1.0], axis=1)
    sm = jnp.concatenate([-sin, jnp.zeros((n, half), f32), pad], axis=1)
    sp = jnp.concatenate([jnp.zeros((n, half), f32), sin, pad], axis=1)
    rep = 128 // head_dim
    return [jnp.tile(t, (1, rep)) for t in (c, sm, sp)]


def _pad_lanes(v, width=128):
    return jnp.pad(v.astype(f32), (0, width - v.shape[0]))[None, :]


def _layer_params(l, w_in, w_out, g_pre_mix, g_post_mix, g_pre_ffn, g_post_ffn, bconv_w, cconv_w, cconv_b, cln_g,
                  cln_b, dconv_w, dconv_b, dt_bias, a_log, d_skip, d_norm, ffn_gate, ffn_up, ffn_down):
    w = w_in[l]
    cuts = np.cumsum((0,) + IN_SIZES)
    col = lambda j: w[:, cuts[j]:cuts[j + 1]]
    zpad = lambda a: jnp.pad(a, ((0, 0), (0, 128 - a.shape[1])))
    row = lambda a: a[l][None, :].astype(f32)
    wih, wil = _split(jnp.concatenate([col(3), jnp.tile(col(4), (1, 128 // IDX_DIM)),
                                       zpad(jnp.concatenate([col(5), col(12)], axis=1))], axis=1))
    return dict(
        wm=jnp.concatenate([w[:, :768], w[:, cuts[6]:cuts[12]]], axis=1).astype(bf16), wih=wih, wil=wil,
        g_pre_mix=row(g_pre_mix), g_post_mix=row(g_post_mix), g_pre_ffn=row(g_pre_ffn), g_post_ffn=row(g_post_ffn),
        w_out=w_out[l].astype(bf16), wg=ffn_gate[l].astype(bf16), wu=ffn_up[l].astype(bf16), wd=ffn_down[l].astype(bf16),
        bw=jnp.pad(bconv_w[l], ((0, 8 - B_CONV), (0, 0))), cw=jnp.pad(cconv_w[l], ((0, 32 - C_CONV), (0, 0))),
        cb=row(cconv_b), lng=row(cln_g), lnb=row(cln_b),
        dw=jnp.pad(dconv_w[l], ((0, 8 - D_CONV), (0, 0))), db=row(dconv_b),
        dtb=_pad_lanes(dt_bias[l]), alog=_pad_lanes(a_log[l]),
        dskip=jnp.repeat(d_skip[l].astype(f32), HEAD_DIM)[None, :], dnorm=row(d_norm),
    )


def _finish(p, ya, yb, yc, yd, x, tm):
    x1, hf = _outproj(ya, yb, yc, yd, x, p['w_out'], p['g_post_mix'], p['g_pre_ffn'], tm)
    return _ffn(hf, x1, p['wg'], p['wu'], p['wd'], p['g_post_ffn'], tm)


def kernel(x_prompt, x_sample, cache_k, cache_v, cache_kidx, page_table, state_bconv, state_cconv, state_dconv, state_ssm, w_in, w_out, g_pre_mix, g_post_mix, g_pre_ffn, g_post_ffn, bconv_w, cconv_w, cconv_b, cln_g, cln_b, dconv_w, dconv_b, dt_bias, a_log, d_skip, d_norm, ffn_gate, ffn_up, ffn_down):
    bsz, seq, _ = x_prompt.shape
    nb, t_dec, _ = x_sample.shape
    depth = w_in.shape[0]
    n_phys = cache_k.shape[1]
    n_pages = page_table.shape[1]
    past = n_pages * PAGE
    assert t_dec == 1 and seq % QB == 0 and seq % TC == 0 and past + t_dec > TOPK_MAX * 4
    weights = (w_in, w_out, g_pre_mix, g_post_mix, g_pre_ffn, g_post_ffn, bconv_w, cconv_w, cconv_b, cln_g, cln_b,
               dconv_w, dconv_b, dt_bias, a_log, d_skip, d_norm, ffn_gate, ffn_up, ffn_down)
    pos_p = jnp.arange(seq, dtype=jnp.int32)
    pos_s = jnp.full((nb,), past, jnp.int32)
    tabs_p = _rope_tables(pos_p, HEAD_DIM) + _rope_tables(pos_p, IDX_DIM)
    tabs_s = _rope_tables(pos_s, HEAD_DIM) + _rope_tables(pos_s, IDX_DIM)
    ck = cache_k.transpose(0, 1, 3, 4, 2).reshape(depth * n_phys, 256, PAGE)
    cv = cache_v.transpose(0, 1, 3, 4, 2).reshape(depth * n_phys, 256, PAGE)
    cki = cache_kidx.transpose(0, 1, 3, 2).reshape(depth * n_phys, IDX_DIM, PAGE)
    pt = page_table.reshape(-1).astype(jnp.int32)
    lane_head = (jnp.arange(256) // HEAD_DIM)[None, None, :] == jnp.arange(8)[None, :, None]

    hp = x_prompt.reshape(bsz * seq, D_MODEL)
    hs = x_sample.reshape(nb, D_MODEL)
    outs_p, outs_s = [], []
    for l in range(depth):
        p = _layer_params(l, *weights)
        conv_prm = (p['bw'], p['cw'], p['cb'], p['lng'], p['lnb'], p['dw'], p['db'], p['dtb'])
        ssm_prm = (p['alog'], p['dskip'], p['dnorm'])

        q, k, kb, v, vt, qi, ki4, kcat, wi, dt, bgc, glu, zg, xbc = _inproj(
            hp, p['g_pre_mix'], p['wm'], p['wih'], p['wil'], tabs_p, 256)
        ya = _attn_prompt_t(q, qi, wi, kcat, kb, vt, bsz, seq)
        yb, yc, yd, nbp, ncp, ndp, ssm_p = _mix_prompt(bgc, glu, zg, xbc, dt, conv_prm + ssm_prm, bsz, seq)
        hp = _finish(p, ya, yb, yc, yd, hp, 512)
        outs_p.append((k.reshape(bsz, seq, N_HEADS, HEAD_DIM), v.reshape(bsz, seq, N_HEADS, HEAD_DIM),
                       ki4[:, :IDX_DIM].reshape(bsz, seq, IDX_DIM), nbp, ncp, ndp,
                       ssm_p.reshape(bsz, N_HEADS, HEAD_DIM, D_STATE)))

        q, k, kb, v, vt, qi, ki4, kcat, wi, dt, bgc, glu, zg, xbc = _inproj(
            hs, p['g_pre_mix'], p['wm'], p['wih'], p['wil'], tabs_s, nb)
        topk = min(TOPK_MAX, (past + t_dec) // 4)
        qi_s = qi.reshape(nb, N_IDX_HEADS, IDX_DIM)
        wi_t = jnp.broadcast_to((wi[:, :N_IDX_HEADS] * (IDX_DIM ** -0.5 * N_IDX_HEADS ** -0.5))[:, :, None],
                                (nb, N_IDX_HEADS, 128))
        sc = _dec_scores(pt, qi_s, wi_t, cki, n_pages, l * n_phys).reshape(nb, past)
        bias, bnew = _dec_select(sc, qi, ki4, wi, topk)
        qm = jnp.where(lane_head, q[:, None, :], 0.0)
        ya = _dec_attn(pt, qm, k.reshape(nb, 1, 256), v.reshape(nb, 1, 256), bias.reshape(nb, 1, past),
                       bnew.reshape(nb, 1, 128), ck, cv, n_pages, l * n_phys).reshape(nb, 256)
        yb, yc, act, dtv, nbs, ncs, nds = _dec_conv(
            bgc, glu, xbc, dt, state_bconv[l].reshape(nb, -1), state_cconv[l].reshape(nb, -1),
            state_dconv[l].reshape(nb, -1), conv_prm)
        yd, ssm_s = _dec_ssm(act.reshape(nb, 1, D_XBC), dtv.reshape(nb, 1, 128), zg.reshape(nb, 1, 256),
                             state_ssm[l].reshape(nb, 256, D_STATE), *ssm_prm)
        hs = _finish(p, ya, yb, yc, yd.reshape(nb, 256), hs, nb)
        outs_s.append((k.reshape(nb, 1, N_HEADS, HEAD_DIM), v.reshape(nb, 1, N_HEADS, HEAD_DIM),
                       ki4[:, :IDX_DIM].reshape(nb, 1, IDX_DIM), nbs.reshape(nb, 2, 256), ncs.reshape(nb, 30, 256),
                       nds.reshape(nb, 3, D_XBC), ssm_s.reshape(nb, N_HEADS, HEAD_DIM, D_STATE)))

    k_p, v_p, kidx_p, bconv_p, cconv_p, dconv_p, ssm_p = [jnp.stack(a) for a in zip(*outs_p)]
    k_s, v_s, kidx_s, bconv_s, cconv_s, dconv_s, ssm_s = [jnp.stack(a) for a in zip(*outs_s)]
    return (hp.reshape(bsz, seq, D_MODEL), hs.reshape(nb, t_dec, D_MODEL), k_p, v_p, kidx_p, k_s, v_s, kidx_s,
            bconv_p, bconv_s, cconv_p, cconv_s, dconv_p, dconv_s, ssm_p, ssm_s)
```

```python
import functools
import math

import jax
import jax.numpy as jnp
import numpy as np
from jax import lax
from jax.experimental import pallas as pl
from jax.experimental.pallas import tpu as pltpu

f32, bf16, i32 = jnp.float32, jnp.bfloat16, jnp.int32
HI = lax.Precision.HIGHEST

D_MODEL = 1024
PAGE = 128
GW = 256
HEAD_DIM = 64
N_HEADS = 4
N_IDX_HEADS = 8
IDX_DIM = 32
TOPK_MAX = 256
ROPE_THETA = 500000.0
C_CONV = 31
B_CONV = 3
D_CONV = 4
D_XBC = 768
D_STATE = 128
SSD_CHUNK = 128
D_FF = 2816
RMS_EPS = 1e-6
LN_EPS = 1e-5
IN_SIZES = (256, 256, 256, 256, 32, 8, 256, 256, 256, 512, 256, 768, 4)
INT_MIN = -(2 ** 31)
KEY_NEG_INF = INT_MIN + 0x7FFFFF
NEG = -1e30
VMEM_LIMIT = 56 * 1024 * 1024


def _dot_nt(a, b, prec=None):
    return lax.dot_general(a, b, (((1,), (1,)), ((), ())), precision=prec, preferred_element_type=f32)


def _dot(a, b, prec=None):
    return jnp.dot(a, b, precision=prec, preferred_element_type=f32)


def _cparams(sem):
    return pltpu.CompilerParams(dimension_semantics=sem, vmem_limit_bytes=VMEM_LIMIT)


def _rms(x, g):
    return x * lax.rsqrt(jnp.mean(x * x, axis=-1, keepdims=True) + RMS_EPS) * g


def _silu(x):
    return x * jax.nn.sigmoid(x)


def _key_to_float(key):
    return lax.bitcast_convert_type(jnp.where(key < 0, key ^ jnp.int32(0x7FFFFFFF), key), f32)


def _split_f32(x):
    c = x * (2.0 ** 16 + 1.0)
    hi = c - (c - x)
    return hi, x - hi


def _split(x):
    hi, lo = _split_f32(x)
    return hi.astype(bf16), lo.astype(bf16)


def _kth_largest(count_ge, shape, topk):
    def bit_step(bi, key):
        cand = key + lax.shift_left(jnp.int32(1), 31 - bi)
        ok = (cand <= KEY_NEG_INF) | (count_ge(_key_to_float(cand)) >= topk)
        return jnp.where(ok, cand, key)

    key = lax.fori_loop(0, 32, bit_step, jnp.full(shape, INT_MIN, i32))
    return key, _key_to_float(key)


def _refine_between_floats(count_ge, lo, key, rows, topk, steps=32):
    hi = _key_to_float(key + 1)

    def step(_, carry):
        lo, hi = carry
        mid = lo + (hi - lo) * 0.5
        ge = count_ge(mid) >= topk
        return jnp.where(rows & ge, mid, lo), jnp.where(rows & jnp.logical_not(ge), mid, hi)

    return lax.fori_loop(0, steps, step, (lo, hi))[0]


def _rope(v, c, sm, sp, half):
    outs = []
    for s in range(v.shape[1] // 128):
        xs = v[:, s * 128:(s + 1) * 128]
        outs.append(xs * c + pltpu.roll(xs, 128 - half, 1) * sm + pltpu.roll(xs, half, 1) * sp)
    return jnp.concatenate(outs, axis=1)


def _inproj_kernel(x_ref, g_ref, wm_ref, wih_ref, wil_ref, c64_ref, sm64_ref, sp64_ref, c32_ref, sm32_ref, sp32_ref,
                   q_ref, k_ref, kb_ref, v_ref, vt_ref, qi_ref, ki_ref, kcat_ref, wi_o_ref, dt_o_ref,
                   bgc_ref, glu_ref, zg_ref, xbc_ref):
    u = _rms(x_ref[...], g_ref[...])
    ub, ul = _split(u)

    def mm(c0, c1):
        return _dot(ub, wm_ref[:, c0:c1])

    def mm3(c0, c1):
        wh = wih_ref[:, c0:c1]
        return _dot(ub, wh) + _dot(ul, wh) + _dot(ub, wil_ref[:, c0:c1])

    c64, sm64, sp64 = c64_ref[...], sm64_ref[...], sp64_ref[...]
    q_ref[...] = _rope(mm(0, 256), c64, sm64, sp64, 8)
    k = _rope(mm(256, 512), c64, sm64, sp64, 8)
    k_ref[...] = k
    kb_ref[...] = k.astype(bf16)
    v = mm(512, 768)
    v_ref[...] = v
    vt_ref[...] = v.T.astype(bf16)
    bgc_ref[...] = mm(768, 1536)
    glu_ref[...] = mm(1536, 2048)
    zg_ref[...] = mm(2048, 2304)
    xbc_ref[...] = mm(2304, 3072)
    c32, sm32, sp32 = c32_ref[...], sm32_ref[...], sp32_ref[...]
    qi_ref[...] = _rope(mm3(0, 256), c32, sm32, sp32, 4)
    ki4 = _rope(mm3(256, 384), c32, sm32, sp32, 4)
    ki_ref[...] = ki4
    kh, kl = _split(ki4)
    lane = lax.broadcasted_iota(i32, ki4.shape, 1)
    kcat_ref[...] = jnp.where((lane >= IDX_DIM) & (lane < 2 * IDX_DIM), kl, kh)
    small = mm3(384, 512)
    wi_o_ref[...] = small
    dt_o_ref[...] = pltpu.roll(small, 128 - N_IDX_HEADS, 1)


def _inproj(x, g, wm, wih, wil, tabs, tm):
    t = x.shape[0]
    nt = t // tm
    ntab = tabs[0].shape[0] // tm
    row = lambda w: pl.BlockSpec((tm, w), lambda i: (i, 0))
    full = lambda a: pl.BlockSpec(a.shape, lambda i: (0,) * a.ndim)
    tab = pl.BlockSpec((tm, 128), lambda i: (i % ntab, 0))
    widths = (256, 256, 256, 256, 256, 256, 128, 128, 128, 128, 768, 512, 256, 768)
    dtypes = (f32, f32, bf16, f32, bf16, f32, f32, bf16, f32, f32, f32, f32, f32, f32)
    out_specs = [row(w) for w in widths]
    out_shape = [jax.ShapeDtypeStruct((t, w), d) for w, d in zip(widths, dtypes)]
    out_specs[4] = pl.BlockSpec((256, tm), lambda i: (0, i))
    out_shape[4] = jax.ShapeDtypeStruct((256, t), bf16)
    return pl.pallas_call(
        _inproj_kernel,
        grid=(nt,),
        in_specs=[row(D_MODEL), full(g), full(wm), full(wih), full(wil)] + [tab] * 6,
        out_specs=out_specs,
        out_shape=out_shape,
        compiler_params=_cparams(("arbitrary",)),
        name="inproj",
    )(x, g, wm, wih, wil, *tabs)


QB = 256
SUB = 128


def _attn_t_kernel(q_ref, qi_ref, wi_ref, kcat_ref, k_ref, vt_ref, o_ref, sc_ref, cat_ref, qm_ref, acc_ref, *, topk):
    i = pl.program_id(1)
    nkb = i + 1
    n_sub = QB // SUB
    hi, lo = _split_f32(qi_ref[...])
    hi_t, lo_t = hi.T, lo.T
    for h in range(N_IDX_HEADS):
        rs = slice(h * IDX_DIM, (h + 1) * IDX_DIM)
        cat_ref[h] = jnp.concatenate([hi_t[rs], hi_t[rs], lo_t[rs], jnp.zeros((IDX_DIM, QB), f32)], axis=0).astype(bf16)
    q_t = (q_ref[...] * (HEAD_DIM ** -0.5)).T
    row = lax.broadcasted_iota(i32, (256, QB), 0)
    for h in range(N_HEADS):
        qm_ref[:, h * QB:(h + 1) * QB] = jnp.where(row // HEAD_DIM == h, q_t, 0.0).astype(bf16)
    w8 =(wi_ref[...] * (IDX_DIM ** -0.5 * N_IDX_HEADS ** -0.5)).T[0:N_IDX_HEADS]
    qpos = i * QB + lax.broadcasted_iota(i32, (1, QB), 1)
    kio = lax.broadcasted_iota(i32, (SUB, 1), 0)

    def tiles(kb):
        return [pl.ds(pl.multiple_of(kb * QB + j * SUB, SUB), SUB) for j in range(n_sub)]

    def score_block(kb, carry):
        for j, sl in enumerate(tiles(kb)):
            kc = kcat_ref[sl, :]
            acc = jnp.zeros((SUB, QB), f32)
            for h in range(N_IDX_HEADS):
                acc = acc + jnp.maximum(_dot(kc, cat_ref[h]), 0.0) * w8[h:h + 1, :]
            sc_ref[sl, :] = jnp.where(kb * QB + j * SUB + kio <= qpos, acc, -jnp.inf)
        return carry

    lax.fori_loop(0, nkb, score_block, 0)

    def count(pred):
        def body(kb, cnt):
            for sl in tiles(kb):
                cnt = cnt + jnp.where(pred(sc_ref[sl, :]), 1.0, 0.0).reshape(SUB // 8, 8, QB).sum(axis=0)
            return cnt

        def body2(kb2, cnt):
            return body(2 * kb2 + 1, body(2 * kb2, cnt))

        cnt = lax.fori_loop(0, nkb // 2, body2, jnp.zeros((8, QB), f32))
        cnt = lax.fori_loop(2 * (nkb // 2), nkb, body, cnt)
        return jnp.sum(cnt, axis=0, keepdims=True)

    def count_ge(t):
        return count(lambda s: s >= t)

    thr_key, thr = _kth_largest(count_ge, (1, QB), topk)
    real = thr > -jnp.inf
    tie = real & (count_ge(thr) > topk)
    any_tie = jnp.max(jnp.where(tie, 1.0, 0.0)) > 0.0

    @pl.when(jnp.logical_not(any_tie))
    def _():
        thr_fin = jnp.maximum(thr, jnp.finfo(f32).min)

        def mask_block(kb, carry):
            for sl in tiles(kb):
                sc_ref[sl, :] = jnp.where(sc_ref[sl, :] >= thr_fin, 0.0, NEG)
            return carry

        lax.fori_loop(0, nkb, mask_block, 0)

    @pl.when(any_tie)
    def _():
        thr2 = _refine_between_floats(count_ge, thr, thr_key, tie, topk)
        need = topk - count(lambda s: s > thr2)
        realf = jnp.where(real, 1.0, 0.0)
        r = lax.broadcasted_iota(i32, (SUB, SUB), 0)
        c = lax.broadcasted_iota(i32, (SUB, SUB), 1)
        tri = jnp.where(c <= r, 1.0, 0.0).astype(bf16)

        def tie_block(kb, carry):
            for sl in tiles(kb):
                s = sc_ref[sl, :]
                eqb = jnp.where(s == thr2, realf, 0.0).astype(bf16)
                cum = _dot(tri, eqb) + carry
                keep = jnp.where(s > thr2, 1.0, jnp.where(cum <= need, eqb.astype(f32), 0.0))
                sc_ref[sl, :] = jnp.where(keep > 0.0, 0.0, NEG)
                carry = cum[SUB - 1:SUB, :]
            return carry

        lax.fori_loop(0, nkb, tie_block, jnp.zeros((1, QB), f32))

    heads = [slice(h * HEAD_DIM, (h + 1) * HEAD_DIM) for h in range(N_HEADS)]

    def attn_block(kb, carry):
        ms, ls = list(carry[0]), list(carry[1])
        for sl in [pl.ds(pl.multiple_of(kb * QB, QB), QB)]:
            s_all = _dot(k_ref[sl, :], qm_ref[...])
            mask = sc_ref[sl, :]
            alphas, ps = [], []
            for h in range(N_HEADS):
                s = s_all[:, h * QB:(h + 1) * QB] + mask
                m_new = jnp.maximum(ms[h], jnp.max(s, axis=0, keepdims=True))
                alphas.append(jnp.exp(ms[h] - m_new))
                p = jnp.exp(s - m_new)
                ls[h] = alphas[h] * ls[h] + jnp.sum(p, axis=0, keepdims=True)
                ps.append(p.astype(bf16))
                ms[h] = m_new
            pvs = [_dot(vt_ref[heads[h], sl], ps[h]) for h in range(N_HEADS)]
            for h in range(N_HEADS):
                acc_ref[heads[h], :] = alphas[h] * acc_ref[heads[h], :] + pvs[h]
        return tuple(ms), tuple(ls)

    acc_ref[...] = jnp.zeros_like(acc_ref)
    m0 = tuple(jnp.full((1, QB), NEG, f32) for _ in range(N_HEADS))
    l0 = tuple(jnp.zeros((1, QB), f32) for _ in range(N_HEADS))
    _, ls = lax.fori_loop(0, nkb, attn_block, (m0, l0))
    out_t = jnp.concatenate([acc_ref[h * HEAD_DIM:(h + 1) * HEAD_DIM, :] / ls[h] for h in range(N_HEADS)], axis=0)
    o_ref[...] = out_t.T


def _attn_prompt_t(q, qi, wi, kcat, kb, vt, bsz, seq):
    nq = seq // QB
    topk = min(TOPK_MAX, seq // 4)
    qrow = lambda w: pl.BlockSpec((QB, w), lambda b, i: (b * nq + i, 0))
    seqblk = lambda w: pl.BlockSpec((seq, w), lambda b, i: (b, 0))
    return pl.pallas_call(
        functools.partial(_attn_t_kernel, topk=float(topk)),
        grid=(bsz, nq),
        in_specs=[qrow(256), qrow(256), qrow(128), seqblk(128), seqblk(256),
                  pl.BlockSpec((256, seq), lambda b, i: (0, b))],
        out_specs=qrow(256),
        out_shape=jax.ShapeDtypeStruct((bsz * seq, 256), f32),
        scratch_shapes=[pltpu.VMEM((seq, QB), f32), pltpu.VMEM((N_IDX_HEADS, 128, QB), bf16),
                        pltpu.VMEM((256, N_HEADS * QB), bf16), pltpu.VMEM((256, QB), f32)],
        compiler_params=_cparams(("arbitrary", "arbitrary")),
        name="attn_prompt",
    )(q, qi, wi, kcat, kb, vt)


TC = 256
HALO_B, HALO_C, HALO_D = 8, 32, 8


def _per_head_lanes(a):
    lane = lax.broadcasted_iota(i32, (a.shape[0], N_HEADS * HEAD_DIM), 1)
    out = jnp.broadcast_to(a[:, N_HEADS - 1:N_HEADS], lane.shape)
    for h in range(N_HEADS - 2, -1, -1):
        out = jnp.where(lane < (h + 1) * HEAD_DIM, a[:, h:h + 1], out)
    return out


def _cumsum_rows(x):
    n = x.shape[0]
    r = lax.broadcasted_iota(i32, (n, n), 0)
    c = lax.broadcasted_iota(i32, (n, n), 1)
    tril = jnp.where(c <= r, 1.0, 0.0).astype(bf16)
    p1, rest = _split_f32(x)
    p2, p3 = _split_f32(rest)
    return _dot(tril, p1.astype(bf16)) + _dot(tril, p2.astype(bf16)) + _dot(tril, p3.astype(bf16))


def _ssd_chunk(xs, bm, cm, dtv, a_row, hcat_ref):
    r = lax.broadcasted_iota(i32, (128, 128), 0)
    c = lax.broadcasted_iota(i32, (128, 128), 1)
    causal = c <= r
    lane = lax.broadcasted_iota(i32, (128, 256), 1)
    da = dtv * a_row
    cs = _cumsum_rows(da)
    cs_t = cs.T
    dt_x = _per_head_lanes(dtv)
    ecs_x = _per_head_lanes(jnp.exp(cs))
    cs_last = cs[127:128, :]
    wend_x = _per_head_lanes(jnp.exp(cs_last - cs) * dtv)
    xdt = (xs * dt_x).astype(bf16)
    bmb, cmb = bm.astype(bf16), cm.astype(bf16)
    hb = hcat_ref[...].astype(bf16)
    y = jnp.zeros((128, 256), f32)
    ystate = []
    for g in range(2):
        cg = cmb[:, g * 128:(g + 1) * 128]
        cb = _dot_nt(cg, bmb[:, g * 128:(g + 1) * 128])
        ystate.append(_dot_nt(cg, hb))
        for h in (2 * g, 2 * g + 1):
            seg = cs[:, h:h + 1] - cs_t[h:h + 1, :]
            dec = jnp.where(causal, jnp.exp(jnp.where(causal, seg, 0.0)), 0.0)
            yh = _dot((cb * dec).astype(bf16), xdt)
            y = jnp.where(lane // 64 == h, yh, y)
    y = y + jnp.where(lane < 128, ystate[0], ystate[1]) * ecs_x
    xw_t = (xs * wend_x).T.astype(bf16)
    upd = jnp.concatenate([_dot(xw_t[0:128], bmb[:, 0:128]), _dot(xw_t[128:256], bmb[:, 128:256])], axis=0)
    elast = jnp.exp(cs_last)
    dcol = jnp.concatenate([jnp.broadcast_to(elast[:, h:h + 1], (HEAD_DIM, D_STATE)) for h in range(N_HEADS)], axis=0)
    hcat_ref[...] = hcat_ref[...] * dcol + upd
    return y


def _gated_rms(ys, xs, zg, dskip, dnorm):
    yg = (ys + dskip * xs) * _silu(zg)
    return yg * lax.rsqrt(jnp.mean(yg * yg, axis=-1, keepdims=True) + RMS_EPS) * dnorm


def _layer_norm_silu(x, g, b):
    mu = jnp.mean(x, axis=-1, keepdims=True)
    var = jnp.mean(jnp.square(x - mu), axis=-1, keepdims=True)
    return _silu((x - mu) * lax.rsqrt(var + LN_EPS) * g + b)


def _mix_kernel(bgc_ref, glu_ref, zg_ref, xbc_ref, dt_ref,
                bw_ref, cw_ref, cb_ref, lng_ref, lnb_ref, dw_ref, db_ref, dtb_ref, alog_ref, dskip_ref, dnorm_ref,
                yb_ref, yc_ref, yd_ref, nb_ref, nc_ref, nd_ref, ssm_ref,
                eb_ref, ec_ref, ed_ref, hcat_ref):
    i = pl.program_id(1)

    @pl.when(i == 0)
    def _():
        eb_ref[0:HALO_B, :] = jnp.zeros((HALO_B, 256), f32)
        ec_ref[0:HALO_C, :] = jnp.zeros((HALO_C, 256), f32)
        ed_ref[0:HALO_D, :] = jnp.zeros((HALO_D, D_XBC), f32)
        hcat_ref[...] = jnp.zeros_like(hcat_ref)

    bgc = bgc_ref[...]
    eb_ref[HALO_B:HALO_B + TC, :] = bgc[:, 512:768] * bgc[:, 0:256]
    conv = jnp.zeros((TC, 256), f32)
    for k in range(B_CONV):
        conv = conv + bw_ref[k:k + 1, :] * eb_ref[pl.ds(HALO_B - (B_CONV - 1) + k, TC), :]
    yb_ref[...] = bgc[:, 256:512] * conv
    nb_ref[0] = eb_ref[HALO_B + TC - (B_CONV - 1):HALO_B + TC, :]
    eb_ref[0:HALO_B, :] = eb_ref[TC:TC + HALO_B, :]

    glu = glu_ref[...]
    ec_ref[HALO_C:HALO_C + TC, :] = glu[:, 0:256] * jax.nn.sigmoid(glu[:, 256:512])
    conv = jnp.zeros((TC, 256), f32)
    for k in range(C_CONV):
        conv = conv + cw_ref[k:k + 1, :] * ec_ref[pl.ds(HALO_C - (C_CONV - 1) + k, TC), :]
    yc_ref[...] = _layer_norm_silu(conv + cb_ref[...], lng_ref[...], lnb_ref[...])
    nc_ref[0] = ec_ref[HALO_C + TC - (C_CONV - 1):HALO_C + TC, :]
    ec_ref[0:HALO_C, :] = ec_ref[TC:TC + HALO_C, :]

    ed_ref[HALO_D:HALO_D + TC, :] = xbc_ref[...]
    conv = jnp.zeros((TC, D_XBC), f32)
    for k in range(D_CONV):
        conv = conv + dw_ref[k:k + 1, :] * ed_ref[pl.ds(HALO_D - (D_CONV - 1) + k, TC), :]
    act = _silu(conv + db_ref[...])
    nd_ref[0] = ed_ref[HALO_D + TC - (D_CONV - 1):HALO_D + TC, :]
    ed_ref[0:HALO_D, :] = ed_ref[TC:TC + HALO_D, :]
    l128 = lax.broadcasted_iota(i32, (1, 128), 1)
    a_row = jnp.where(l128 < N_HEADS, -jnp.exp(alog_ref[...]), 0.0)
    dtv = jax.nn.softplus(dt_ref[...] + dtb_ref[...])
    zg = zg_ref[...]
    for j in range(TC // SSD_CHUNK):
        rs = slice(j * SSD_CHUNK, (j + 1) * SSD_CHUNK)
        xs = act[rs, 0:256]
        y = _ssd_chunk(xs, act[rs, 256:512], act[rs, 512:768], dtv[rs], a_row, hcat_ref)
        yd_ref[rs, :] = _gated_rms(y, xs, zg[rs], dskip_ref[...], dnorm_ref[...])
    ssm_ref[0] = hcat_ref[...]


def _mix_prompt(bgc, glu, zg, xbc, dt, prm, bsz, seq):
    nt = seq // TC
    row = lambda w: pl.BlockSpec((TC, w), lambda b, i: (b * nt + i, 0))
    full = lambda a: pl.BlockSpec(a.shape, lambda b, i: (0,) * a.ndim)
    st = lambda r, w: pl.BlockSpec((1, r, w), lambda b, i: (b, 0, 0))
    t = bsz * seq
    return pl.pallas_call(
        _mix_kernel,
        grid=(bsz, nt),
        in_specs=[row(768), row(512), row(256), row(768), row(128)] + [full(a) for a in prm],
        out_specs=[row(256), row(256), row(256), st(2, 256), st(30, 256), st(3, D_XBC), st(256, 128)],
        out_shape=[jax.ShapeDtypeStruct((t, 256), f32)] * 3 + [
            jax.ShapeDtypeStruct((bsz, 2, 256), f32), jax.ShapeDtypeStruct((bsz, 30, 256), f32),
            jax.ShapeDtypeStruct((bsz, 3, D_XBC), f32), jax.ShapeDtypeStruct((bsz, 256, 128), f32)],
        scratch_shapes=[pltpu.VMEM((HALO_B + TC, 256), f32), pltpu.VMEM((HALO_C + TC, 256), f32),
                        pltpu.VMEM((HALO_D + TC, D_XBC), f32), pltpu.VMEM((256, 128), f32)],
        compiler_params=_cparams(("arbitrary", "arbitrary")),
        name="mix_prompt",
    )(bgc, glu, zg, xbc, dt, *prm)


def _outproj_kernel(ya_ref, yb_ref, yc_ref, yd_ref, x_ref, w_ref, gpost_ref, gpre_ref, x1_ref, hf_ref):
    mix = _dot(ya_ref[...].astype(bf16), w_ref[0:256, :])
    mix = mix + _dot(yb_ref[...].astype(bf16), w_ref[256:512, :])
    mix = mix + _dot(yc_ref[...].astype(bf16), w_ref[512:768, :])
    mix = mix + _dot(yd_ref[...].astype(bf16), w_ref[768:1024, :])
    x1 = x_ref[...] + _rms(mix, gpost_ref[...])
    x1_ref[...] = x1
    hf_ref[...] = _rms(x1, gpre_ref[...]).astype(bf16)


def _outproj(ya, yb, yc, yd, x, w, gpost, gpre, tm):
    t = x.shape[0]
    row = lambda w_: pl.BlockSpec((tm, w_), lambda i: (i, 0))
    full = lambda a: pl.BlockSpec(a.shape, lambda i: (0,) * a.ndim)
    return pl.pallas_call(
        _outproj_kernel,
        grid=(t // tm,),
        in_specs=[row(256)] * 4 + [row(D_MODEL), full(w), full(gpost), full(gpre)],
        out_specs=[row(D_MODEL), row(D_MODEL)],
        out_shape=[jax.ShapeDtypeStruct((t, D_MODEL), f32), jax.ShapeDtypeStruct((t, D_MODEL), bf16)],
        compiler_params=_cparams(("arbitrary",)),
        name="outproj",
    )(ya, yb, yc, yd, x, w, gpost, gpre)


FF_CHUNK = 1408


def _ffn_kernel(hf_ref, x1_ref, wg_ref, wu_ref, wd_ref, g_ref, o_ref, acc_ref):
    j = pl.program_id(1)
    hf = hf_ref[...]
    a = _silu(_dot(hf, wg_ref[...])) * _dot(hf, wu_ref[...])
    part = _dot(a.astype(bf16), wd_ref[...])

    @pl.when(j == 0)
    def _():
        acc_ref[...] = part

    @pl.when(j > 0)
    def _():
        acc_ref[...] = acc_ref[...] + part

    @pl.when(j == pl.num_programs(1) - 1)
    def _():
        o_ref[...] = x1_ref[...] + _rms(acc_ref[...], g_ref[...])


def _ffn(hf, x1, wg, wu, wd, g, tm):
    t = x1.shape[0]
    nj = D_FF // FF_CHUNK
    row = lambda: pl.BlockSpec((tm, D_MODEL), lambda i, j: (i, 0))
    return pl.pallas_call(
        _ffn_kernel,
        grid=(t // tm, nj),
        in_specs=[row(), row(), pl.BlockSpec((D_MODEL, FF_CHUNK), lambda i, j: (0, j)),
                  pl.BlockSpec((D_MODEL, FF_CHUNK), lambda i, j: (0, j)),
                  pl.BlockSpec((FF_CHUNK, D_MODEL), lambda i, j: (j, 0)),
                  pl.BlockSpec((1, D_MODEL), lambda i, j: (0, 0))],
        out_specs=row(),
        out_shape=jax.ShapeDtypeStruct((t, D_MODEL), f32),
        scratch_shapes=[pltpu.VMEM((tm, D_MODEL), f32)],
        compiler_params=_cparams(("arbitrary", "arbitrary")),
        name="ffn",
    )(hf, x1, wg, wu, wd, g)


def _page_copies(pt_ref, src_hbm, dst, sem, b, n_pages, base):
    return [pltpu.make_async_copy(src_hbm.at[base + pt_ref[b * n_pages + p]],
                                  dst.at[:, pl.ds(p * PAGE, PAGE)], sem) for p in range(n_pages)]


SCORE_GROUP = 8


def _dec_score_kernel(pt_ref, qi_ref, wi_ref, cki_hbm, o_ref, buf, sem, *, n_pages, base):
    g = pl.program_id(0)
    ng = pl.num_programs(0)

    def copies(gg, slot):
        return [c for j in range(SCORE_GROUP)
                for c in _page_copies(pt_ref, cki_hbm, buf.at[slot, j], sem.at[slot], gg * SCORE_GROUP + j, n_pages, base)]

    @pl.when(g == 0)
    def _():
        for c in copies(0, 0):
            c.start()

    @pl.when(g + 1 < ng)
    def _():
        for c in copies(g + 1, (g + 1) % 2):
            c.start()

    slot = g % 2
    for c in copies(g, slot):
        c.wait()
    for j in range(SCORE_GROUP):
        b = g * SCORE_GROUP + j
        s = _dot(qi_ref[b], buf[slot, j], HI)
        w = jnp.tile(wi_ref[b], (1, n_pages))
        o_ref[j] = jnp.sum(jnp.maximum(s, 0.0) * w, axis=0, keepdims=True)


def _dec_scores(pt, qi_s, wi_t, cki, n_pages, base):
    nb = qi_s.shape[0]
    past = n_pages * PAGE
    gs = pltpu.PrefetchScalarGridSpec(
        num_scalar_prefetch=1, grid=(nb // SCORE_GROUP,),
        in_specs=[pl.BlockSpec(qi_s.shape, lambda b, pt_: (0, 0, 0)), pl.BlockSpec(wi_t.shape, lambda b, pt_: (0, 0, 0)),
                  pl.BlockSpec(memory_space=pl.ANY)],
        out_specs=pl.BlockSpec((SCORE_GROUP, 1, past), lambda b, pt_: (b, 0, 0)),
        scratch_shapes=[pltpu.VMEM((2, SCORE_GROUP, IDX_DIM, past), f32), pltpu.SemaphoreType.DMA((2,))])
    return pl.pallas_call(
        functools.partial(_dec_score_kernel, n_pages=n_pages, base=base),
        grid_spec=gs, out_shape=jax.ShapeDtypeStruct((nb, 1, past), f32),
        compiler_params=_cparams(("arbitrary",)), name="dec_scores",
    )(pt, qi_s, wi_t, cki)


def _dec_select_kernel(sc_ref, qi_ref, ki_ref, wi_ref, bias_ref, bnew_ref, *, topk):
    nb, past = sc_ref.shape
    wi = wi_ref[...] * (IDX_DIM ** -0.5 * N_IDX_HEADS ** -0.5)
    gj = lax.broadcasted_iota(i32, (256, 128), 0)
    gh = lax.broadcasted_iota(i32, (256, 128), 1)
    seg = jnp.where(gj // IDX_DIM == gh, 1.0, 0.0)
    ki = jnp.concatenate([ki_ref[...], ki_ref[...]], axis=1)
    s_new = _dot(qi_ref[...] * ki, seg, HI)
    sc_new = jnp.broadcast_to(jnp.sum(jnp.maximum(s_new, 0.0) * wi, axis=-1, keepdims=True), (nb, 128))
    sc = sc_ref[...]
    ones = jnp.ones((past, 128), bf16)
    wide = lambda t: jnp.concatenate([t] * (past // 128), axis=1)

    def count(pred_past, pred_new):
        return _dot(jnp.where(pred_past, 1.0, 0.0).astype(bf16), ones) + jnp.where(pred_new, 1.0, 0.0)

    def count_ge(t):
        return count(sc >= wide(t), sc_new >= t)

    thr_key, thr = _kth_largest(count_ge, (nb, 128), topk)
    bias_ref[...] = jnp.where(sc >= wide(thr), 0.0, NEG)
    bnew_ref[...] = jnp.where(sc_new >= thr, 0.0, NEG)
    tie = count_ge(thr) > topk

    @pl.when(jnp.max(jnp.where(tie, 1.0, 0.0)) > 0.0)
    def _():
        thr2 = _refine_between_floats(count_ge, thr, thr_key, tie, topk)
        need = topk - count(sc > wide(thr2), sc_new > thr2)
        eqb = jnp.where(sc == wide(thr2), 1.0, 0.0).astype(bf16)
        r = lax.broadcasted_iota(i32, (PAGE, PAGE), 0)
        c = lax.broadcasted_iota(i32, (PAGE, PAGE), 1)
        tri = jnp.where(r <= c, 1.0, 0.0).astype(bf16)
        carry = jnp.zeros((nb, 128), f32)
        for p in range(past // PAGE):
            sl = slice(p * PAGE, (p + 1) * PAGE)
            cum = _dot(eqb[:, sl], tri) + carry
            keep = jnp.where(sc[:, sl] > thr2, 1.0, jnp.where(cum <= need, eqb[:, sl].astype(f32), 0.0))
            bias_ref[:, sl] = jnp.where(keep > 0.0, 0.0, NEG)
            carry = carry + _dot(eqb[:, sl], ones[0:PAGE])
        keep_new = jnp.where(sc_new > thr2, 1.0, jnp.where((sc_new == thr2) & (carry + 1.0 <= need), 1.0, 0.0))
        bnew_ref[...] = jnp.where(keep_new > 0.0, 0.0, NEG)


def _dec_select(sc, qi, ki4, wi, topk):
    nb, past = sc.shape
    return pl.pallas_call(
        functools.partial(_dec_select_kernel, topk=float(topk)),
        out_shape=[jax.ShapeDtypeStruct((nb, past), f32), jax.ShapeDtypeStruct((nb, 128), f32)],
        compiler_params=pltpu.CompilerParams(vmem_limit_bytes=VMEM_LIMIT), name="dec_select",
    )(sc, qi, ki4, wi)


def _dec_attn_kernel(pt_ref, qm_ref, kn_ref, vn_ref, bias_ref, bnew_ref, ck_hbm, cv_hbm, o_ref,
                     kbuf, vbuf, sem, *, n_pages, base):
    b = pl.program_id(0)
    nb = pl.num_programs(0)

    def copies(bb, slot):
        return (_page_copies(pt_ref, ck_hbm, kbuf.at[slot], sem.at[0, slot], bb, n_pages, base)
                + _page_copies(pt_ref, cv_hbm, vbuf.at[slot], sem.at[1, slot], bb, n_pages, base))

    @pl.when(b == 0)
    def _():
        for c in copies(0, 0):
            c.start()

    @pl.when(b + 1 < nb)
    def _():
        for c in copies(b + 1, (b + 1) % 2):
            c.start()

    slot = b % 2
    for c in copies(b, slot):
        c.wait()
    qm = qm_ref[0] * (HEAD_DIM ** -0.5)
    s = _dot(qm.astype(bf16), kbuf[slot].astype(bf16)) + bias_ref[0]
    s_new = jnp.sum(qm * kn_ref[0], axis=-1, keepdims=True) + bnew_ref[0][:, 0:1]
    m = jnp.maximum(jnp.max(s, axis=-1, keepdims=True), s_new)
    p = jnp.exp(s - m)
    p_new = jnp.exp(s_new - m)
    den = jnp.sum(p, axis=-1, keepdims=True) + p_new
    out8 = (_dot_nt(p.astype(bf16), vbuf[slot].astype(bf16)) + p_new * vn_ref[0]) / den
    row = lax.broadcasted_iota(i32, (8, 256), 0)
    lane = lax.broadcasted_iota(i32, (8, 256), 1)
    o_ref[0] = jnp.sum(jnp.where(lane // HEAD_DIM == row, out8, 0.0), axis=0, keepdims=True)


def _dec_attn(pt, qm, kn, vn, bias, bnew, ck, cv, n_pages, base):
    nb = qm.shape[0]
    past = n_pages * PAGE
    per = lambda r, w: pl.BlockSpec((1, r, w), lambda b, pt_: (b, 0, 0))
    gs = pltpu.PrefetchScalarGridSpec(
        num_scalar_prefetch=1, grid=(nb,),
        in_specs=[per(8, 256), per(1, 256), per(1, 256), per(1, past), per(1, 128),
                  pl.BlockSpec(memory_space=pl.ANY), pl.BlockSpec(memory_space=pl.ANY)],
        out_specs=per(1, 256),
        scratch_shapes=[pltpu.VMEM((2, 256, past), f32), pltpu.VMEM((2, 256, past), f32),
                        pltpu.SemaphoreType.DMA((2, 2))])
    return pl.pallas_call(
        functools.partial(_dec_attn_kernel, n_pages=n_pages, base=base),
        grid_spec=gs, out_shape=jax.ShapeDtypeStruct((nb, 1, 256), f32),
        compiler_params=_cparams(("arbitrary",)), name="dec_attn",
    )(pt, qm, kn, vn, bias, bnew, ck, cv)


def _dec_conv_kernel(bgc_ref, glu_ref, xbc_ref, dt_ref, sb_ref, sc_ref, sd_ref,
                     bw_ref, cw_ref, cb_ref, lng_ref, lnb_ref, dw_ref, db_ref, dtb_ref,
                     yb_ref, yc_ref, act_ref, dtv_ref, nb_ref, nc_ref, nd_ref):
    bgc = bgc_ref[...]
    ub = bgc[:, 512:768] * bgc[:, 0:256]
    sb = sb_ref[...]
    conv = bw_ref[0:1, :] * sb[:, 0:256] + bw_ref[1:2, :] * sb[:, 256:512] + bw_ref[2:3, :] * ub
    yb_ref[...] = bgc[:, 256:512] * conv
    nb_ref[...] = jnp.concatenate([sb[:, 256:512], ub], axis=1)

    glu = glu_ref[...]
    uc = glu[:, 0:256] * jax.nn.sigmoid(glu[:, 256:512])
    conv = cw_ref[C_CONV - 1:C_CONV, :] * uc
    for k in range(C_CONV - 1):
        conv = conv + cw_ref[k:k + 1, :] * sc_ref[:, k * 256:(k + 1) * 256]
    yc_ref[...] = _layer_norm_silu(conv + cb_ref[...], lng_ref[...], lnb_ref[...])
    nc_ref[:, 0:(C_CONV - 2) * 256] = sc_ref[:, 256:(C_CONV - 1) * 256]
    nc_ref[:, (C_CONV - 2) * 256:(C_CONV - 1) * 256] = uc

    xbc = xbc_ref[...]
    conv = dw_ref[D_CONV - 1:D_CONV, :] * xbc
    for k in range(D_CONV - 1):
        conv = conv + dw_ref[k:k + 1, :] * sd_ref[:, k * D_XBC:(k + 1) * D_XBC]
    act_ref[...] = _silu(conv + db_ref[...])
    nd_ref[:, 0:(D_CONV - 2) * D_XBC] = sd_ref[:, D_XBC:(D_CONV - 1) * D_XBC]
    nd_ref[:, (D_CONV - 2) * D_XBC:(D_CONV - 1) * D_XBC] = xbc
    dtv_ref[...] = jax.nn.softplus(dt_ref[...] + dtb_ref[...])


def _dec_conv(bgc, glu, xbc, dt, sb, sc, sd, prm):
    nb = bgc.shape[0]
    shp = lambda w: jax.ShapeDtypeStruct((nb, w), f32)
    return pl.pallas_call(
        _dec_conv_kernel,
        out_shape=[shp(256), shp(256), shp(D_XBC), shp(128), shp(2 * 256), shp(30 * 256), shp(3 * D_XBC)],
        compiler_params=pltpu.CompilerParams(vmem_limit_bytes=VMEM_LIMIT), name="dec_conv",
    )(bgc, glu, xbc, dt, sb, sc, sd, *prm)


SSM_GROUP = 8


def _dec_ssm_kernel(act_ref, dtv_ref, zg_ref, h_ref, alog_ref, dskip_ref, dnorm_ref, yd_ref, hn_ref):
    l128 = lax.broadcasted_iota(i32, (1, 128), 1)
    a_row = jnp.where(l128 < N_HEADS, -jnp.exp(alog_ref[...]), 0.0)
    r = lax.broadcasted_iota(i32, (128, 256), 0)
    lane = lax.broadcasted_iota(i32, (128, 256), 1)
    r8 = lax.broadcasted_iota(i32, (8, 128), 0)
    l256 = lax.broadcasted_iota(i32, (1, 256), 1)
    zrow = jnp.zeros((1, 128), f32)
    group = range(act_ref.shape[0])
    xs_, cm_, lt_, rmat_ = [], [], [], []
    for g in group:
        act = act_ref[g]
        xs, bm, cm = act[:, 0:256], act[:, 256:512], act[:, 512:768]
        dtv = dtv_ref[g]
        dec_x = _per_head_lanes(jnp.exp(dtv * a_row))
        xdt = xs * _per_head_lanes(dtv)
        lrows = jnp.where((r == 0) & (lane < 128), xdt, 0.0) + jnp.where((r == 1) & (lane >= 128), xdt, 0.0)
        lt_.append((lrows + jnp.where(r == 2, dec_x, 0.0)).T)
        b0 = jnp.concatenate([bm[:, 0:128], zrow], axis=1)
        b1 = jnp.concatenate([bm[:, 128:256], zrow], axis=1)
        rmat_.append(jnp.where(r == 0, b0, 0.0) + jnp.where(r == 1, b1, 0.0)
                     + jnp.where((r == 2) & (lane >= 128), 1.0, 0.0))
        xs_.append(xs)
        cm_.append(cm)
    res_ = [_dot(lt_[g], rmat_[g], HI) for g in group]
    hn_ = [h_ref[g] * res_[g][:, 128:256] + res_[g][:, 0:128] for g in group]
    for g in group:
        hn_ref[g] = hn_[g]
    crows_ = [jnp.where(r8 == 0, cm_[g][:, 0:128], 0.0) + jnp.where(r8 == 1, cm_[g][:, 128:256], 0.0) for g in group]
    y8_ = [_dot_nt(crows_[g], hn_[g], HI) for g in group]
    for g in group:
        y = jnp.where(l256 < 128, y8_[g][0:1, :], y8_[g][1:2, :])
        yd_ref[g] = _gated_rms(y, xs_[g], zg_ref[g], dskip_ref[...], dnorm_ref[...])


def _dec_ssm(act, dtv, zg, h, alog, dskip, dnorm):
    nb = act.shape[0]
    per = lambda r, w: pl.BlockSpec((SSM_GROUP, r, w), lambda b: (b, 0, 0))
    full = lambda a: pl.BlockSpec(a.shape, lambda b: (0,) * a.ndim)
    return pl.pallas_call(
        _dec_ssm_kernel,
        grid=(nb // SSM_GROUP,),
        in_specs=[per(1, D_XBC), per(1, 128), per(1, 256), per(256, 128), full(alog), full(dskip), full(dnorm)],
        out_specs=[per(1, 256), per(256, 128)],
        out_shape=[jax.ShapeDtypeStruct((nb, 1, 256), f32), jax.ShapeDtypeStruct((nb, 256, 128), f32)],
        compiler_params=_cparams(("arbitrary",)), name="dec_ssm",
    )(act, dtv, zg, h, alog, dskip, dnorm)


def _rope_tables(pos, head_dim):
    rot = head_dim // 4
    half = rot // 2
    inv = ROPE_THETA ** (-jnp.arange(half, dtype=f32) * 2.0 / rot)
    ang = pos.astype(f32)[:, None] * inv[None, :]
    cos, sin = jnp.cos(ang), jnp.sin(ang)
    n = pos.shape[0]
    pad = jnp.zeros((n, head_dim - rot), f32)
    c = jnp.concatenate([cos, cos, pad + 1.0], axis=1)
    sm = jnp.concatenate([-sin, jnp.zeros((n, half), f32), pad], axis=1)
    sp = jnp.concatenate([jnp.zeros((n, half), f32), sin, pad], axis=1)
    rep = 128 // head_dim
    return [jnp.tile(t, (1, rep)) for t in (c, sm, sp)]


def _pad_lanes(v, width=128):
    return jnp.pad(v.astype(f32), (0, width - v.shape[0]))[None, :]


def _layer_params(l, w_in, w_out, g_pre_mix, g_post_mix, g_pre_ffn, g_post_ffn, bconv_w, cconv_w, cconv_b, cln_g,
                  cln_b, dconv_w, dconv_b, dt_bias, a_log, d_skip, d_norm, ffn_gate, ffn_up, ffn_down):
    w = w_in[l]
    cuts = np.cumsum((0,) + IN_SIZES)
    col = lambda j: w[:, cuts[j]:cuts[j + 1]]
    zpad = lambda a: jnp.pad(a, ((0, 0), (0, 128 - a.shape[1])))
    row = lambda a: a[l][None, :].astype(f32)
    wih, wil = _split(jnp.concatenate([col(3), jnp.tile(col(4), (1, 128 // IDX_DIM)),
                                       zpad(jnp.concatenate([col(5), col(12)], axis=1))], axis=1))
    return dict(
        wm=jnp.concatenate([w[:, :768], w[:, cuts[6]:cuts[12]]], axis=1).astype(bf16), wih=wih, wil=wil,
        g_pre_mix=row(g_pre_mix), g_post_mix=row(g_post_mix), g_pre_ffn=row(g_pre_ffn), g_post_ffn=row(g_post_ffn),
        w_out=w_out[l].astype(bf16), wg=ffn_gate[l].astype(bf16), wu=ffn_up[l].astype(bf16), wd=ffn_down[l].astype(bf16),
        bw=jnp.pad(bconv_w[l], ((0, 8 - B_CONV), (0, 0))), cw=jnp.pad(cconv_w[l], ((0, 32 - C_CONV), (0, 0))),
        cb=row(cconv_b), lng=row(cln_g), lnb=row(cln_b),
        dw=jnp.pad(dconv_w[l], ((0, 8 - D_CONV), (0, 0))), db=row(dconv_b),
        dtb=_pad_lanes(dt_bias[l]), alog=_pad_lanes(a_log[l]),
        dskip=jnp.repeat(d_skip[l].astype(f32), HEAD_DIM)[None, :], dnorm=row(d_norm),
    )


def _finish(p, ya, yb, yc, yd, x, tm):
    x1, hf = _outproj(ya, yb, yc, yd, x, p['w_out'], p['g_post_mix'], p['g_pre_ffn'], tm)
    return _ffn(hf, x1, p['wg'], p['wu'], p['wd'], p['g_post_ffn'], tm)


def kernel(x_prompt, x_sample, cache_k, cache_v, cache_kidx, page_table, state_bconv, state_cconv, state_dconv, state_ssm, w_in, w_out, g_pre_mix, g_post_mix, g_pre_ffn, g_post_ffn, bconv_w, cconv_w, cconv_b, cln_g, cln_b, dconv_w, dconv_b, dt_bias, a_log, d_skip, d_norm, ffn_gate, ffn_up, ffn_down):
    bsz, seq, _ = x_prompt.shape
    nb, t_dec, _ = x_sample.shape
    depth = w_in.shape[0]
    n_phys = cache_k.shape[1]
    n_pages = page_table.shape[1]
    past = n_pages * PAGE
    assert t_dec == 1 and seq % QB == 0 and seq % TC == 0 and past + t_dec > TOPK_MAX * 4
    assert nb % SSM_GROUP == 0 and nb % SCORE_GROUP == 0
    weights = (w_in, w_out, g_pre_mix, g_post_mix, g_pre_ffn, g_post_ffn, bconv_w, cconv_w, cconv_b, cln_g, cln_b,
               dconv_w, dconv_b, dt_bias, a_log, d_skip, d_norm, ffn_gate, ffn_up, ffn_down)
    pos_p = jnp.arange(seq, dtype=jnp.int32)
    pos_s = jnp.full((nb,), past, jnp.int32)
    tabs_p = _rope_tables(pos_p, HEAD_DIM) + _rope_tables(pos_p, IDX_DIM)
    tabs_s = _rope_tables(pos_s, HEAD_DIM) + _rope_tables(pos_s, IDX_DIM)
    ck = cache_k.transpose(0, 1, 3, 4, 2).reshape(depth * n_phys, 256, PAGE)
    cv = cache_v.transpose(0, 1, 3, 4, 2).reshape(depth * n_phys, 256, PAGE)
    cki = cache_kidx.transpose(0, 1, 3, 2).reshape(depth * n_phys, IDX_DIM, PAGE)
    pt = page_table.reshape(-1).astype(jnp.int32)
    lane_head = (jnp.arange(256) // HEAD_DIM)[None, None, :] == jnp.arange(8)[None, :, None]

    hp = x_prompt.reshape(bsz * seq, D_MODEL)
    hs = x_sample.reshape(nb, D_MODEL)
    outs_p, outs_s = [], []
    for l in range(depth):
        p = _layer_params(l, *weights)
        conv_prm = (p['bw'], p['cw'], p['cb'], p['lng'], p['lnb'], p['dw'], p['db'], p['dtb'])
        ssm_prm = (p['alog'], p['dskip'], p['dnorm'])

        q, k, kb, v, vt, qi, ki4, kcat, wi, dt, bgc, glu, zg, xbc = _inproj(
            hp, p['g_pre_mix'], p['wm'], p['wih'], p['wil'], tabs_p, 256)
        ya = _attn_prompt_t(q, qi, wi, kcat, kb, vt, bsz, seq)
        yb, yc, yd, nbp, ncp, ndp, ssm_p = _mix_prompt(bgc, glu, zg, xbc, dt, conv_prm + ssm_prm, bsz, seq)
        hp = _finish(p, ya, yb, yc, yd, hp, 512)
        outs_p.append((k.reshape(bsz, seq, N_HEADS, HEAD_DIM), v.reshape(bsz, seq, N_HEADS, HEAD_DIM),
                       ki4[:, :IDX_DIM].reshape(bsz, seq, IDX_DIM), nbp, ncp, ndp,
                       ssm_p.reshape(bsz, N_HEADS, HEAD_DIM, D_STATE)))

        q, k, kb, v, vt, qi, ki4, kcat, wi, dt, bgc, glu, zg, xbc = _inproj(
            hs, p['g_pre_mix'], p['wm'], p['wih'], p['wil'], tabs_s, nb)
        topk = min(TOPK_MAX, (past + t_dec) // 4)
        qi_s = qi.reshape(nb, N_IDX_HEADS, IDX_DIM)
        wi_t = jnp.broadcast_to((wi[:, :N_IDX_HEADS] * (IDX_DIM ** -0.5 * N_IDX_HEADS ** -0.5))[:, :, None],
                                (nb, N_IDX_HEADS, 128))
        sc = _dec_scores(pt, qi_s, wi_t, cki, n_pages, l * n_phys).reshape(nb, past)
        bias, bnew = _dec_select(sc, qi, ki4, wi, topk)
        qm = jnp.where(lane_head, q[:, None, :], 0.0)
        ya = _dec_attn(pt, qm, k.reshape(nb, 1, 256), v.reshape(nb, 1, 256), bias.reshape(nb, 1, past),
                       bnew.reshape(nb, 1, 128), ck, cv, n_pages, l * n_phys).reshape(nb, 256)
        yb, yc, act, dtv, nbs, ncs, nds = _dec_conv(
            bgc, glu, xbc, dt, state_bconv[l].reshape(nb, -1), state_cconv[l].reshape(nb, -1),
            state_dconv[l].reshape(nb, -1), conv_prm)
        yd, ssm_s = _dec_ssm(act.reshape(nb, 1, D_XBC), dtv.reshape(nb, 1, 128), zg.reshape(nb, 1, 256),
                             state_ssm[l].reshape(nb, 256, D_STATE), *ssm_prm)
        hs = _finish(p, ya, yb, yc, yd.reshape(nb, 256), hs, nb)
        outs_s.append((k.reshape(nb, 1, N_HEADS, HEAD_DIM), v.reshape(nb, 1, N_HEADS, HEAD_DIM),
                       ki4[:, :IDX_DIM].reshape(nb, 1, IDX_DIM), nbs.reshape(nb, 2, 256), ncs.reshape(nb, 30, 256),
                       nds.reshape(nb, 3, D_XBC), ssm_s.reshape(nb, N_HEADS, HEAD_DIM, D_STATE)))

    k_p, v_p, kidx_p, bconv_p, cconv_p, dconv_p, ssm_p = [jnp.stack(a) for a in zip(*outs_p)]
    k_s, v_s, kidx_s, bconv_s, cconv_s, dconv_s, ssm_s = [jnp.stack(a) for a in zip(*outs_s)]
    return (hp.reshape(bsz, seq, D_MODEL), hs.reshape(nb, t_dec, D_MODEL), k_p, v_p, kidx_p, k_s, v_s, kidx_s,
            bconv_p, bconv_s, cconv_p, cconv_s, dconv_p, dconv_s, ssm_p, ssm_s)
```

```python
import functools
import math

import jax
import jax.numpy as jnp
import numpy as np
from jax import lax
from jax.experimental import pallas as pl
from jax.experimental.pallas import tpu as pltpu

f32, bf16, i32 = jnp.float32, jnp.bfloat16, jnp.int32
HI = lax.Precision.HIGHEST

D_MODEL = 1024
PAGE = 128
GW = 256
HEAD_DIM = 64
N_HEADS = 4
N_IDX_HEADS = 8
IDX_DIM = 32
TOPK_MAX = 256
ROPE_THETA = 500000.0
C_CONV = 31
B_CONV = 3
D_CONV = 4
D_XBC = 768
D_STATE = 128
SSD_CHUNK = 128
D_FF = 2816
RMS_EPS = 1e-6
LN_EPS = 1e-5
IN_SIZES = (256, 256, 256, 256, 32, 8, 256, 256, 256, 512, 256, 768, 4)
INT_MIN = -(2 ** 31)
KEY_NEG_INF = INT_MIN + 0x7FFFFF
NEG = -1e30
VMEM_LIMIT = 56 * 1024 * 1024


def _dot_nt(a, b, prec=None):
    return lax.dot_general(a, b, (((1,), (1,)), ((), ())), precision=prec, preferred_element_type=f32)


def _dot(a, b, prec=None):
    return jnp.dot(a, b, precision=prec, preferred_element_type=f32)


def _cparams(sem):
    return pltpu.CompilerParams(dimension_semantics=sem, vmem_limit_bytes=VMEM_LIMIT)


def _rms(x, g):
    return x * lax.rsqrt(jnp.mean(x * x, axis=-1, keepdims=True) + RMS_EPS) * g


def _silu(x):
    return x * jax.nn.sigmoid(x)


def _key_to_float(key):
    return lax.bitcast_convert_type(jnp.where(key < 0, key ^ jnp.int32(0x7FFFFFFF), key), f32)


def _split_f32(x):
    c = x * (2.0 ** 16 + 1.0)
    hi = c - (c - x)
    return hi, x - hi


def _split(x):
    hi, lo = _split_f32(x)
    return hi.astype(bf16), lo.astype(bf16)


def _kth_largest(count_ge, shape, topk):
    def bit_step(bi, key):
        cand = key + lax.shift_left(jnp.int32(1), 31 - bi)
        ok = (cand <= KEY_NEG_INF) | (count_ge(_key_to_float(cand)) >= topk)
        return jnp.where(ok, cand, key)

    key = lax.fori_loop(0, 32, bit_step, jnp.full(shape, INT_MIN, i32))
    return key, _key_to_float(key)


def _refine_between_floats(count_ge, lo, key, rows, topk, steps=32):
    hi = _key_to_float(key + 1)

    def step(_, carry):
        lo, hi = carry
        mid = lo + (hi - lo) * 0.5
        ge = count_ge(mid) >= topk
        return jnp.where(rows & ge, mid, lo), jnp.where(rows & jnp.logical_not(ge), mid, hi)

    return lax.fori_loop(0, steps, step, (lo, hi))[0]


def _rope(v, c, sm, sp, half):
    outs = []
    for s in range(v.shape[1] // 128):
        xs = v[:, s * 128:(s + 1) * 128]
        outs.append(xs * c + pltpu.roll(xs, 128 - half, 1) * sm + pltpu.roll(xs, half, 1) * sp)
    return jnp.concatenate(outs, axis=1)


def _inproj_kernel(x_ref, g_ref, wm_ref, wih_ref, wil_ref, c64_ref, sm64_ref, sp64_ref, c32_ref, sm32_ref, sp32_ref,
                   q_ref, k_ref, kb_ref, v_ref, vt_ref, qi_ref, ki_ref, kcat_ref, wi_o_ref, dt_o_ref,
                   bgc_ref, glu_ref, zg_ref, xbc_ref):
    u = _rms(x_ref[...], g_ref[...])
    ub, ul = _split(u)

    def mm(c0, c1):
        return _dot(ub, wm_ref[:, c0:c1])

    def mm3(c0, c1):
        wh = wih_ref[:, c0:c1]
        return _dot(ub, wh) + _dot(ul, wh) + _dot(ub, wil_ref[:, c0:c1])

    c64, sm64, sp64 = c64_ref[...], sm64_ref[...], sp64_ref[...]
    q_ref[...] = _rope(mm(0, 256), c64, sm64, sp64, 8)
    k = _rope(mm(256, 512), c64, sm64, sp64, 8)
    k_ref[...] = k
    kb_ref[...] = k.astype(bf16)
    v = mm(512, 768)
    v_ref[...] = v
    vt_ref[...] = v.T.astype(bf16)
    bgc_ref[...] = mm(768, 1536)
    glu_ref[...] = mm(1536, 2048)
    zg_ref[...] = mm(2048, 2304)
    xbc_ref[...] = mm(2304, 3072)
    c32, sm32, sp32 = c32_ref[...], sm32_ref[...], sp32_ref[...]
    qi_ref[...] = _rope(mm3(0, 256), c32, sm32, sp32, 4)
    ki4 = _rope(mm3(256, 384), c32, sm32, sp32, 4)
    ki_ref[...] = ki4
    kh, kl = _split(ki4)
    lane = lax.broadcasted_iota(i32, ki4.shape, 1)
    kcat_ref[...] = jnp.where((lane >= IDX_DIM) & (lane < 2 * IDX_DIM), kl, kh)
    small = mm3(384, 512)
    wi_o_ref[...] = small
    dt_o_ref[...] = pltpu.roll(small, 128 - N_IDX_HEADS, 1)


def _inproj(x, g, wm, wih, wil, tabs, tm):
    t = x.shape[0]
    nt = t // tm
    ntab = tabs[0].shape[0] // tm
    row = lambda w: pl.BlockSpec((tm, w), lambda i: (i, 0))
    full = lambda a: pl.BlockSpec(a.shape, lambda i: (0,) * a.ndim)
    tab = pl.BlockSpec((tm, 128), lambda i: (i % ntab, 0))
    widths = (256, 256, 256, 256, 256, 256, 128, 128, 128, 128, 768, 512, 256, 768)
    dtypes = (f32, f32, bf16, f32, bf16, f32, f32, bf16, f32, f32, f32, f32, f32, f32)
    out_specs = [row(w) for w in widths]
    out_shape = [jax.ShapeDtypeStruct((t, w), d) for w, d in zip(widths, dtypes)]
    out_specs[4] = pl.BlockSpec((256, tm), lambda i: (0, i))
    out_shape[4] = jax.ShapeDtypeStruct((256, t), bf16)
    return pl.pallas_call(
        _inproj_kernel,
        grid=(nt,),
        in_specs=[row(D_MODEL), full(g), full(wm), full(wih), full(wil)] + [tab] * 6,
        out_specs=out_specs,
        out_shape=out_shape,
        compiler_params=_cparams(("arbitrary",)),
        name="inproj",
    )(x, g, wm, wih, wil, *tabs)


QB = 256
SUB = 128


def _attn_t_kernel(q_ref, qi_ref, wi_ref, kcat_ref, k_ref, vt_ref, o_ref, sc_ref, cat_ref, qm_ref, acc_ref, *, topk):
    i = pl.program_id(1)
    nkb = i + 1
    n_sub = QB // SUB
    hi, lo = _split_f32(qi_ref[...])
    hi_t, lo_t = hi.T, lo.T
    for h in range(N_IDX_HEADS):
        rs = slice(h * IDX_DIM, (h + 1) * IDX_DIM)
        cat_ref[h] = jnp.concatenate([hi_t[rs], hi_t[rs], lo_t[rs], jnp.zeros((IDX_DIM, QB), f32)], axis=0).astype(bf16)
    q_t = (q_ref[...] * (HEAD_DIM ** -0.5)).T
    row = lax.broadcasted_iota(i32, (256, QB), 0)
    for h in range(N_HEADS):
        qm_ref[:, h * QB:(h + 1) * QB] = jnp.where(row // HEAD_DIM == h, q_t, 0.0).astype(bf16)
    w8 =(wi_ref[...] * (IDX_DIM ** -0.5 * N_IDX_HEADS ** -0.5)).T[0:N_IDX_HEADS]
    qpos = i * QB + lax.broadcasted_iota(i32, (1, QB), 1)
    kio = lax.broadcasted_iota(i32, (SUB, 1), 0)

    def tiles(kb):
        return [pl.ds(pl.multiple_of(kb * QB + j * SUB, SUB), SUB) for j in range(n_sub)]

    def score_block(kb, carry):
        for j, sl in enumerate(tiles(kb)):
            kc = kcat_ref[sl, :]
            acc = jnp.zeros((SUB, QB), f32)
            for h in range(N_IDX_HEADS):
                acc = acc + jnp.maximum(_dot(kc, cat_ref[h]), 0.0) * w8[h:h + 1, :]
            sc_ref[sl, :] = jnp.where(kb * QB + j * SUB + kio <= qpos, acc, -jnp.inf)
        return carry

    lax.fori_loop(0, nkb, score_block, 0)

    def count(pred):
        def body(kb, cnt):
            for sl in tiles(kb):
                cnt = cnt + jnp.where(pred(sc_ref[sl, :]), 1.0, 0.0).reshape(SUB // 8, 8, QB).sum(axis=0)
            return cnt

        def body2(kb2, cnt):
            return body(2 * kb2 + 1, body(2 * kb2, cnt))

        cnt = lax.fori_loop(0, nkb // 2, body2, jnp.zeros((8, QB), f32))
        cnt = lax.fori_loop(2 * (nkb // 2), nkb, body, cnt)
        return jnp.sum(cnt, axis=0, keepdims=True)

    def count_ge(t):
        return count(lambda s: s >= t)

    thr_key, thr = _kth_largest(count_ge, (1, QB), topk)
    real = thr > -jnp.inf
    tie = real & (count_ge(thr) > topk)
    any_tie = jnp.max(jnp.where(tie, 1.0, 0.0)) > 0.0

    @pl.when(jnp.logical_not(any_tie))
    def _():
        thr_fin = jnp.maximum(thr, jnp.finfo(f32).min)

        def mask_block(kb, carry):
            for sl in tiles(kb):
                sc_ref[sl, :] = jnp.where(sc_ref[sl, :] >= thr_fin, 0.0, NEG)
            return carry

        lax.fori_loop(0, nkb, mask_block, 0)

    @pl.when(any_tie)
    def _():
        thr2 = _refine_between_floats(count_ge, thr, thr_key, tie, topk)
        need = topk - count(lambda s: s > thr2)
        realf = jnp.where(real, 1.0, 0.0)
        r = lax.broadcasted_iota(i32, (SUB, SUB), 0)
        c = lax.broadcasted_iota(i32, (SUB, SUB), 1)
        tri = jnp.where(c <= r, 1.0, 0.0).astype(bf16)

        def tie_block(kb, carry):
            for sl in tiles(kb):
                s = sc_ref[sl, :]
                eqb = jnp.where(s == thr2, realf, 0.0).astype(bf16)
                cum = _dot(tri, eqb) + carry
                keep = jnp.where(s > thr2, 1.0, jnp.where(cum <= need, eqb.astype(f32), 0.0))
                sc_ref[sl, :] = jnp.where(keep > 0.0, 0.0, NEG)
                carry = cum[SUB - 1:SUB, :]
            return carry

        lax.fori_loop(0, nkb, tie_block, jnp.zeros((1, QB), f32))

    heads = [slice(h * HEAD_DIM, (h + 1) * HEAD_DIM) for h in range(N_HEADS)]

    def attn_blocks(kb0, carry, n):
        ms, ls = list(carry[0]), list(carry[1])
        sls = [pl.ds(pl.multiple_of((kb0 + j) * QB, QB), QB) for j in range(n)]
        s_alls = [_dot(k_ref[sl, :], qm_ref[...]) for sl in sls]
        alphas, pvs = [], []
        for j, sl in enumerate(sls):
            mask = sc_ref[sl, :]
            al, ps = [], []
            for h in range(N_HEADS):
                s = s_alls[j][:, h * QB:(h + 1) * QB] + mask
                m_new = jnp.maximum(ms[h], jnp.max(s, axis=0, keepdims=True))
                al.append(jnp.exp(ms[h] - m_new))
                p = jnp.exp(s - m_new)
                ls[h] = al[h] * ls[h] + jnp.sum(p, axis=0, keepdims=True)
                ps.append(p.astype(bf16))
                ms[h] = m_new
            alphas.append(al)
            pvs.append([_dot(vt_ref[heads[h], sl], ps[h]) for h in range(N_HEADS)])
        for j in range(n):
            for h in range(N_HEADS):
                acc_ref[heads[h], :] = alphas[j][h] * acc_ref[heads[h], :] + pvs[j][h]
        return tuple(ms), tuple(ls)

    acc_ref[...] = jnp.zeros_like(acc_ref)
    m0 = tuple(jnp.full((1, QB), NEG, f32) for _ in range(N_HEADS))
    l0 = tuple(jnp.zeros((1, QB), f32) for _ in range(N_HEADS))
    carry = lax.fori_loop(0, nkb // 2, lambda kb2, c: attn_blocks(2 * kb2, c, 2), (m0, l0))
    _, ls = lax.fori_loop(2 * (nkb // 2), nkb, lambda kb, c: attn_blocks(kb, c, 1), carry)
    out_t = jnp.concatenate([acc_ref[h * HEAD_DIM:(h + 1) * HEAD_DIM, :] / ls[h] for h in range(N_HEADS)], axis=0)
    o_ref[...] = out_t.T


def _attn_prompt_t(q, qi, wi, kcat, kb, vt, bsz, seq):
    nq = seq // QB
    topk = min(TOPK_MAX, seq // 4)
    qrow = lambda w: pl.BlockSpec((QB, w), lambda b, i: (b * nq + i, 0))
    seqblk = lambda w: pl.BlockSpec((seq, w), lambda b, i: (b, 0))
    return pl.pallas_call(
        functools.partial(_attn_t_kernel, topk=float(topk)),
        grid=(bsz, nq),
        in_specs=[qrow(256), qrow(256), qrow(128), seqblk(128), seqblk(256),
                  pl.BlockSpec((256, seq), lambda b, i: (0, b))],
        out_specs=qrow(256),
        out_shape=jax.ShapeDtypeStruct((bsz * seq, 256), f32),
        scratch_shapes=[pltpu.VMEM((seq, QB), f32), pltpu.VMEM((N_IDX_HEADS, 128, QB), bf16),
                        pltpu.VMEM((256, N_HEADS * QB), bf16), pltpu.VMEM((256, QB), f32)],
        compiler_params=_cparams(("arbitrary", "arbitrary")),
        name="attn_prompt",
    )(q, qi, wi, kcat, kb, vt)


TC = 256
HALO_B, HALO_C, HALO_D = 8, 32, 8


def _per_head_lanes(a):
    lane = lax.broadcasted_iota(i32, (a.shape[0], N_HEADS * HEAD_DIM), 1)
    out = jnp.broadcast_to(a[:, N_HEADS - 1:N_HEADS], lane.shape)
    for h in range(N_HEADS - 2, -1, -1):
        out = jnp.where(lane < (h + 1) * HEAD_DIM, a[:, h:h + 1], out)
    return out


def _cumsum_rows(x):
    n = x.shape[0]
    r = lax.broadcasted_iota(i32, (n, n), 0)
    c = lax.broadcasted_iota(i32, (n, n), 1)
    tril = jnp.where(c <= r, 1.0, 0.0).astype(bf16)
    p1, rest = _split_f32(x)
    p2, p3 = _split_f32(rest)
    return _dot(tril, p1.astype(bf16)) + _dot(tril, p2.astype(bf16)) + _dot(tril, p3.astype(bf16))


def _ssd_chunk(xs, bm, cm, dtv, a_row, hcat_ref):
    r = lax.broadcasted_iota(i32, (128, 128), 0)
    c = lax.broadcasted_iota(i32, (128, 128), 1)
    causal = c <= r
    lane = lax.broadcasted_iota(i32, (128, 256), 1)
    da = dtv * a_row
    cs = _cumsum_rows(da)
    cs_t = cs.T
    dt_x = _per_head_lanes(dtv)
    ecs_x = _per_head_lanes(jnp.exp(cs))
    cs_last = cs[127:128, :]
    wend_x = _per_head_lanes(jnp.exp(cs_last - cs) * dtv)
    xdt = (xs * dt_x).astype(bf16)
    bmb, cmb = bm.astype(bf16), cm.astype(bf16)
    hb = hcat_ref[...].astype(bf16)
    y = jnp.zeros((128, 256), f32)
    ystate = []
    for g in range(2):
        cg = cmb[:, g * 128:(g + 1) * 128]
        cb = _dot_nt(cg, bmb[:, g * 128:(g + 1) * 128])
        ystate.append(_dot_nt(cg, hb))
        for h in (2 * g, 2 * g + 1):
            seg = cs[:, h:h + 1] - cs_t[h:h + 1, :]
            dec = jnp.where(causal, jnp.exp(jnp.where(causal, seg, 0.0)), 0.0)
            yh = _dot((cb * dec).astype(bf16), xdt)
            y = jnp.where(lane // 64 == h, yh, y)
    y = y + jnp.where(lane < 128, ystate[0], ystate[1]) * ecs_x
    xw_t = (xs * wend_x).T.astype(bf16)
    upd = jnp.concatenate([_dot(xw_t[0:128], bmb[:, 0:128]), _dot(xw_t[128:256], bmb[:, 128:256])], axis=0)
    elast = jnp.exp(cs_last)
    dcol = jnp.concatenate([jnp.broadcast_to(elast[:, h:h + 1], (HEAD_DIM, D_STATE)) for h in range(N_HEADS)], axis=0)
    hcat_ref[...] = hcat_ref[...] * dcol + upd
    return y


def _gated_rms(ys, xs, zg, dskip, dnorm):
    yg = (ys + dskip * xs) * _silu(zg)
    return yg * lax.rsqrt(jnp.mean(yg * yg, axis=-1, keepdims=True) + RMS_EPS) * dnorm


def _layer_norm_silu(x, g, b):
    mu = jnp.mean(x, axis=-1, keepdims=True)
    var = jnp.mean(jnp.square(x - mu), axis=-1, keepdims=True)
    return _silu((x - mu) * lax.rsqrt(var + LN_EPS) * g + b)


def _mix_kernel(bgc_ref, glu_ref, zg_ref, xbc_ref, dt_ref,
                bw_ref, cw_ref, cb_ref, lng_ref, lnb_ref, dw_ref, db_ref, dtb_ref, alog_ref, dskip_ref, dnorm_ref,
                yb_ref, yc_ref, yd_ref, nb_ref, nc_ref, nd_ref, ssm_ref,
                eb_ref, ec_ref, ed_ref, hcat_ref, pc_ref):
    i = pl.program_id(1)

    @pl.when(i == 0)
    def _():
        eb_ref[0:HALO_B, :] = jnp.zeros((HALO_B, 256), f32)
        ec_ref[0:HALO_C, :] = jnp.zeros((HALO_C, 256), f32)
        ed_ref[0:HALO_D, :] = jnp.zeros((HALO_D, D_XBC), f32)
        hcat_ref[...] = jnp.zeros_like(hcat_ref)

    bgc = bgc_ref[...]
    eb_ref[HALO_B:HALO_B + TC, :] = bgc[:, 512:768] * bgc[:, 0:256]
    conv = jnp.zeros((TC, 256), f32)
    for k in range(B_CONV):
        conv = conv + bw_ref[k:k + 1, :] * eb_ref[pl.ds(HALO_B - (B_CONV - 1) + k, TC), :]
    yb_ref[...] = bgc[:, 256:512] * conv
    nb_ref[0] = eb_ref[HALO_B + TC - (B_CONV - 1):HALO_B + TC, :]
    eb_ref[0:HALO_B, :] = eb_ref[TC:TC + HALO_B, :]

    glu = glu_ref[...]
    ec_ref[HALO_C:HALO_C + TC, :] = glu[:, 0:256] * jax.nn.sigmoid(glu[:, 256:512])
    off = HALO_C - (C_CONV - 1)
    conv = jnp.zeros((TC, 256), f32)
    for r in range(8):
        taps = [j for j in range(r, HALO_C + 1, 8) if off <= j < off + C_CONV]
        rows = TC if r == 0 else TC + 8
        part = jnp.zeros((rows, 256), f32)
        for j in taps:
            part = part + cw_ref[j - off:j - off + 1, :] * ec_ref[j - r:j - r + rows, :]
        if r == 0:
            conv = conv + part
        else:
            pc_ref[...] = part
            conv = conv + pc_ref[r:r + TC, :]
    yc_ref[...] = _layer_norm_silu(conv + cb_ref[...], lng_ref[...], lnb_ref[...])
    nc_ref[0] = ec_ref[HALO_C + TC - (C_CONV - 1):HALO_C + TC, :]
    ec_ref[0:HALO_C, :] = ec_ref[TC:TC + HALO_C, :]

    ed_ref[HALO_D:HALO_D + TC, :] = xbc_ref[...]
    conv = jnp.zeros((TC, D_XBC), f32)
    for k in range(D_CONV):
        conv = conv + dw_ref[k:k + 1, :] * ed_ref[pl.ds(HALO_D - (D_CONV - 1) + k, TC), :]
    act = _silu(conv + db_ref[...])
    nd_ref[0] = ed_ref[HALO_D + TC - (D_CONV - 1):HALO_D + TC, :]
    ed_ref[0:HALO_D, :] = ed_ref[TC:TC + HALO_D, :]
    l128 = lax.broadcasted_iota(i32, (1, 128), 1)
    a_row = jnp.where(l128 < N_HEADS, -jnp.exp(alog_ref[...]), 0.0)
    dtv = jax.nn.softplus(dt_ref[...] + dtb_ref[...])
    zg = zg_ref[...]
    for j in range(TC // SSD_CHUNK):
        rs = slice(j * SSD_CHUNK, (j + 1) * SSD_CHUNK)
        xs = act[rs, 0:256]
        y = _ssd_chunk(xs, act[rs, 256:512], act[rs, 512:768], dtv[rs], a_row, hcat_ref)
        yd_ref[rs, :] = _gated_rms(y, xs, zg[rs], dskip_ref[...], dnorm_ref[...])
    ssm_ref[0] = hcat_ref[...]


def _mix_prompt(bgc, glu, zg, xbc, dt, prm, bsz, seq):
    nt = seq // TC
    row = lambda w: pl.BlockSpec((TC, w), lambda b, i: (b * nt + i, 0))
    full = lambda a: pl.BlockSpec(a.shape, lambda b, i: (0,) * a.ndim)
    st = lambda r, w: pl.BlockSpec((1, r, w), lambda b, i: (b, 0, 0))
    t = bsz * seq
    return pl.pallas_call(
        _mix_kernel,
        grid=(bsz, nt),
        in_specs=[row(768), row(512), row(256), row(768), row(128)] + [full(a) for a in prm],
        out_specs=[row(256), row(256), row(256), st(2, 256), st(30, 256), st(3, D_XBC), st(256, 128)],
        out_shape=[jax.ShapeDtypeStruct((t, 256), f32)] * 3 + [
            jax.ShapeDtypeStruct((bsz, 2, 256), f32), jax.ShapeDtypeStruct((bsz, 30, 256), f32),
            jax.ShapeDtypeStruct((bsz, 3, D_XBC), f32), jax.ShapeDtypeStruct((bsz, 256, 128), f32)],
        scratch_shapes=[pltpu.VMEM((HALO_B + TC, 256), f32), pltpu.VMEM((HALO_C + TC, 256), f32),
                        pltpu.VMEM((HALO_D + TC, D_XBC), f32), pltpu.VMEM((256, 128), f32),
                        pltpu.VMEM((TC + 8, 256), f32)],
        compiler_params=_cparams(("arbitrary", "arbitrary")),
        name="mix_prompt",
    )(bgc, glu, zg, xbc, dt, *prm)


def _outproj_kernel(ya_ref, yb_ref, yc_ref, yd_ref, x_ref, w_ref, gpost_ref, gpre_ref, x1_ref, hf_ref):
    mix = _dot(ya_ref[...].astype(bf16), w_ref[0:256, :])
    mix = mix + _dot(yb_ref[...].astype(bf16), w_ref[256:512, :])
    mix = mix + _dot(yc_ref[...].astype(bf16), w_ref[512:768, :])
    mix = mix + _dot(yd_ref[...].astype(bf16), w_ref[768:1024, :])
    x1 = x_ref[...] + _rms(mix, gpost_ref[...])
    x1_ref[...] = x1
    hf_ref[...] = _rms(x1, gpre_ref[...]).astype(bf16)


def _outproj(ya, yb, yc, yd, x, w, gpost, gpre, tm):
    t = x.shape[0]
    row = lambda w_: pl.BlockSpec((tm, w_), lambda i: (i, 0))
    full = lambda a: pl.BlockSpec(a.shape, lambda i: (0,) * a.ndim)
    return pl.pallas_call(
        _outproj_kernel,
        grid=(t // tm,),
        in_specs=[row(256)] * 4 + [row(D_MODEL), full(w), full(gpost), full(gpre)],
        out_specs=[row(D_MODEL), row(D_MODEL)],
        out_shape=[jax.ShapeDtypeStruct((t, D_MODEL), f32), jax.ShapeDtypeStruct((t, D_MODEL), bf16)],
        compiler_params=_cparams(("arbitrary",)),
        name="outproj",
    )(ya, yb, yc, yd, x, w, gpost, gpre)


FF_CHUNK = 1408


def _ffn_kernel(hf_ref, x1_ref, wg_ref, wu_ref, wd_ref, g_ref, o_ref, acc_ref):
    j = pl.program_id(1)
    hf = hf_ref[...]
    a = _silu(_dot(hf, wg_ref[...])) * _dot(hf, wu_ref[...])
    part = _dot(a.astype(bf16), wd_ref[...])

    @pl.when(j == 0)
    def _():
        acc_ref[...] = part

    @pl.when(j > 0)
    def _():
        acc_ref[...] = acc_ref[...] + part

    @pl.when(j == pl.num_programs(1) - 1)
    def _():
        o_ref[...] = x1_ref[...] + _rms(acc_ref[...], g_ref[...])


def _ffn(hf, x1, wg, wu, wd, g, tm):
    t = x1.shape[0]
    nj = D_FF // FF_CHUNK
    row = lambda: pl.BlockSpec((tm, D_MODEL), lambda i, j: (i, 0))
    return pl.pallas_call(
        _ffn_kernel,
        grid=(t // tm, nj),
        in_specs=[row(), row(), pl.BlockSpec((D_MODEL, FF_CHUNK), lambda i, j: (0, j)),
                  pl.BlockSpec((D_MODEL, FF_CHUNK), lambda i, j: (0, j)),
                  pl.BlockSpec((FF_CHUNK, D_MODEL), lambda i, j: (j, 0)),
                  pl.BlockSpec((1, D_MODEL), lambda i, j: (0, 0))],
        out_specs=row(),
        out_shape=jax.ShapeDtypeStruct((t, D_MODEL), f32),
        scratch_shapes=[pltpu.VMEM((tm, D_MODEL), f32)],
        compiler_params=_cparams(("arbitrary", "arbitrary")),
        name="ffn",
    )(hf, x1, wg, wu, wd, g)


def _page_copies(pt_ref, src_hbm, dst, sem, b, n_pages, base):
    return [pltpu.make_async_copy(src_hbm.at[base + pt_ref[b * n_pages + p]],
                                  dst.at[:, pl.ds(p * PAGE, PAGE)], sem) for p in range(n_pages)]


SCORE_GROUP = 8


def _dec_score_kernel(pt_ref, qi_ref, wi_ref, cki_hbm, o_ref, buf, sem, *, n_pages, base):
    g = pl.program_id(0)
    ng = pl.num_programs(0)

    def copies(gg, slot):
        return [c for j in range(SCORE_GROUP)
                for c in _page_copies(pt_ref, cki_hbm, buf.at[slot, j], sem.at[slot], gg * SCORE_GROUP + j, n_pages, base)]

    @pl.when(g == 0)
    def _():
        for c in copies(0, 0):
            c.start()

    @pl.when(g + 1 < ng)
    def _():
        for c in copies(g + 1, (g + 1) % 2):
            c.start()

    slot = g % 2
    for c in copies(g, slot):
        c.wait()
    for j in range(SCORE_GROUP):
        b = g * SCORE_GROUP + j
        s = _dot(qi_ref[b], buf[slot, j], HI)
        w = jnp.tile(wi_ref[b], (1, n_pages))
        o_ref[j] = jnp.sum(jnp.maximum(s, 0.0) * w, axis=0, keepdims=True)


def _dec_scores(pt, qi_s, wi_t, cki, n_pages, base):
    nb = qi_s.shape[0]
    past = n_pages * PAGE
    gs = pltpu.PrefetchScalarGridSpec(
        num_scalar_prefetch=1, grid=(nb // SCORE_GROUP,),
        in_specs=[pl.BlockSpec(qi_s.shape, lambda b, pt_: (0, 0, 0)), pl.BlockSpec(wi_t.shape, lambda b, pt_: (0, 0, 0)),
                  pl.BlockSpec(memory_space=pl.ANY)],
        out_specs=pl.BlockSpec((SCORE_GROUP, 1, past), lambda b, pt_: (b, 0, 0)),
        scratch_shapes=[pltpu.VMEM((2, SCORE_GROUP, IDX_DIM, past), f32), pltpu.SemaphoreType.DMA((2,))])
    return pl.pallas_call(
        functools.partial(_dec_score_kernel, n_pages=n_pages, base=base),
        grid_spec=gs, out_shape=jax.ShapeDtypeStruct((nb, 1, past), f32),
        compiler_params=_cparams(("arbitrary",)), name="dec_scores",
    )(pt, qi_s, wi_t, cki)


def _dec_select_kernel(sc_ref, qi_ref, ki_ref, wi_ref, bias_ref, bnew_ref, *, topk):
    nb, past = sc_ref.shape
    wi = wi_ref[...] * (IDX_DIM ** -0.5 * N_IDX_HEADS ** -0.5)
    gj = lax.broadcasted_iota(i32, (256, 128), 0)
    gh = lax.broadcasted_iota(i32, (256, 128), 1)
    seg = jnp.where(gj // IDX_DIM == gh, 1.0, 0.0)
    ki = jnp.concatenate([ki_ref[...], ki_ref[...]], axis=1)
    s_new = _dot(qi_ref[...] * ki, seg, HI)
    sc_new = jnp.broadcast_to(jnp.sum(jnp.maximum(s_new, 0.0) * wi, axis=-1, keepdims=True), (nb, 128))
    sc = sc_ref[...]
    ones = jnp.ones((past, 128), bf16)
    wide = lambda t: jnp.concatenate([t] * (past // 128), axis=1)

    def count(pred_past, pred_new):
        return _dot(jnp.where(pred_past, 1.0, 0.0).astype(bf16), ones) + jnp.where(pred_new, 1.0, 0.0)

    def count_ge(t):
        return count(sc >= wide(t), sc_new >= t)

    thr_key, thr = _kth_largest(count_ge, (nb, 128), topk)
    bias_ref[...] = jnp.where(sc >= wide(thr), 0.0, NEG)
    bnew_ref[...] = jnp.where(sc_new >= thr, 0.0, NEG)
    tie = count_ge(thr) > topk

    @pl.when(jnp.max(jnp.where(tie, 1.0, 0.0)) > 0.0)
    def _():
        thr2 = _refine_between_floats(count_ge, thr, thr_key, tie, topk)
        need = topk - count(sc > wide(thr2), sc_new > thr2)
        eqb = jnp.where(sc == wide(thr2), 1.0, 0.0).astype(bf16)
        r = lax.broadcasted_iota(i32, (PAGE, PAGE), 0)
        c = lax.broadcasted_iota(i32, (PAGE, PAGE), 1)
        tri = jnp.where(r <= c, 1.0, 0.0).astype(bf16)
        carry = jnp.zeros((nb, 128), f32)
        for p in range(past // PAGE):
            sl = slice(p * PAGE, (p + 1) * PAGE)
            cum = _dot(eqb[:, sl], tri) + carry
            keep = jnp.where(sc[:, sl] > thr2, 1.0, jnp.where(cum <= need, eqb[:, sl].astype(f32), 0.0))
            bias_ref[:, sl] = jnp.where(keep > 0.0, 0.0, NEG)
            carry = carry + _dot(eqb[:, sl], ones[0:PAGE])
        keep_new = jnp.where(sc_new > thr2, 1.0, jnp.where((sc_new == thr2) & (carry + 1.0 <= need), 1.0, 0.0))
        bnew_ref[...] = jnp.where(keep_new > 0.0, 0.0, NEG)


def _dec_select(sc, qi, ki4, wi, topk):
    nb, past = sc.shape
    return pl.pallas_call(
        functools.partial(_dec_select_kernel, topk=float(topk)),
        out_shape=[jax.ShapeDtypeStruct((nb, past), f32), jax.ShapeDtypeStruct((nb, 128), f32)],
        compiler_params=pltpu.CompilerParams(vmem_limit_bytes=VMEM_LIMIT), name="dec_select",
    )(sc, qi, ki4, wi)


def _dec_attn_kernel(pt_ref, qm_ref, kn_ref, vn_ref, bias_ref, bnew_ref, ck_hbm, cv_hbm, o_ref,
                     kbuf, vbuf, sem, *, n_pages, base):
    b = pl.program_id(0)
    nb = pl.num_programs(0)

    def copies(bb, slot):
        return (_page_copies(pt_ref, ck_hbm, kbuf.at[slot], sem.at[0, slot], bb, n_pages, base)
                + _page_copies(pt_ref, cv_hbm, vbuf.at[slot], sem.at[1, slot], bb, n_pages, base))

    @pl.when(b == 0)
    def _():
        for c in copies(0, 0):
            c.start()

    @pl.when(b + 1 < nb)
    def _():
        for c in copies(b + 1, (b + 1) % 2):
            c.start()

    slot = b % 2
    for c in copies(b, slot):
        c.wait()
    qm = qm_ref[0] * (HEAD_DIM ** -0.5)
    s = _dot(qm.astype(bf16), kbuf[slot].astype(bf16)) + bias_ref[0]
    s_new = jnp.sum(qm * kn_ref[0], axis=-1, keepdims=True) + bnew_ref[0][:, 0:1]
    m = jnp.maximum(jnp.max(s, axis=-1, keepdims=True), s_new)
    p = jnp.exp(s - m)
    p_new = jnp.exp(s_new - m)
    den = jnp.sum(p, axis=-1, keepdims=True) + p_new
    out8 = (_dot_nt(p.astype(bf16), vbuf[slot].astype(bf16)) + p_new * vn_ref[0]) / den
    row = lax.broadcasted_iota(i32, (8, 256), 0)
    lane = lax.broadcasted_iota(i32, (8, 256), 1)
    o_ref[0] = jnp.sum(jnp.where(lane // HEAD_DIM == row, out8, 0.0), axis=0, keepdims=True)


def _dec_attn(pt, qm, kn, vn, bias, bnew, ck, cv, n_pages, base):
    nb = qm.shape[0]
    past = n_pages * PAGE
    per = lambda r, w: pl.BlockSpec((1, r, w), lambda b, pt_: (b, 0, 0))
    gs = pltpu.PrefetchScalarGridSpec(
        num_scalar_prefetch=1, grid=(nb,),
        in_specs=[per(8, 256), per(1, 256), per(1, 256), per(1, past), per(1, 128),
                  pl.BlockSpec(memory_space=pl.ANY), pl.BlockSpec(memory_space=pl.ANY)],
        out_specs=per(1, 256),
        scratch_shapes=[pltpu.VMEM((2, 256, past), f32), pltpu.VMEM((2, 256, past), f32),
                        pltpu.SemaphoreType.DMA((2, 2))])
    return pl.pallas_call(
        functools.partial(_dec_attn_kernel, n_pages=n_pages, base=base),
        grid_spec=gs, out_shape=jax.ShapeDtypeStruct((nb, 1, 256), f32),
        compiler_params=_cparams(("arbitrary",)), name="dec_attn",
    )(pt, qm, kn, vn, bias, bnew, ck, cv)


def _dec_conv_kernel(bgc_ref, glu_ref, xbc_ref, dt_ref, sb_ref, sc_ref, sd_ref,
                     bw_ref, cw_ref, cb_ref, lng_ref, lnb_ref, dw_ref, db_ref, dtb_ref,
                     yb_ref, yc_ref, act_ref, dtv_ref, nb_ref, nc_ref, nd_ref):
    bgc = bgc_ref[...]
    ub = bgc[:, 512:768] * bgc[:, 0:256]
    sb = sb_ref[...]
    conv = bw_ref[0:1, :] * sb[:, 0:256] + bw_ref[1:2, :] * sb[:, 256:512] + bw_ref[2:3, :] * ub
    yb_ref[...] = bgc[:, 256:512] * conv
    nb_ref[...] = jnp.concatenate([sb[:, 256:512], ub], axis=1)

    glu = glu_ref[...]
    uc = glu[:, 0:256] * jax.nn.sigmoid(glu[:, 256:512])
    conv = cw_ref[C_CONV - 1:C_CONV, :] * uc
    for k in range(C_CONV - 1):
        conv = conv + cw_ref[k:k + 1, :] * sc_ref[:, k * 256:(k + 1) * 256]
    yc_ref[...] = _layer_norm_silu(conv + cb_ref[...], lng_ref[...], lnb_ref[...])
    nc_ref[:, 0:(C_CONV - 2) * 256] = sc_ref[:, 256:(C_CONV - 1) * 256]
    nc_ref[:, (C_CONV - 2) * 256:(C_CONV - 1) * 256] = uc

    xbc = xbc_ref[...]
    conv = dw_ref[D_CONV - 1:D_CONV, :] * xbc
    for k in range(D_CONV - 1):
        conv = conv + dw_ref[k:k + 1, :] * sd_ref[:, k * D_XBC:(k + 1) * D_XBC]
    act_ref[...] = _silu(conv + db_ref[...])
    nd_ref[:, 0:(D_CONV - 2) * D_XBC] = sd_ref[:, D_XBC:(D_CONV - 1) * D_XBC]
    nd_ref[:, (D_CONV - 2) * D_XBC:(D_CONV - 1) * D_XBC] = xbc
    dtv_ref[...] = jax.nn.softplus(dt_ref[...] + dtb_ref[...])


def _dec_conv(bgc, glu, xbc, dt, sb, sc, sd, prm):
    nb = bgc.shape[0]
    shp = lambda w: jax.ShapeDtypeStruct((nb, w), f32)
    return pl.pallas_call(
        _dec_conv_kernel,
        out_shape=[shp(256), shp(256), shp(D_XBC), shp(128), shp(2 * 256), shp(30 * 256), shp(3 * D_XBC)],
        compiler_params=pltpu.CompilerParams(vmem_limit_bytes=VMEM_LIMIT), name="dec_conv",
    )(bgc, glu, xbc, dt, sb, sc, sd, *prm)


SSM_GROUP = 8


def _dec_ssm_kernel(act_ref, dtv_ref, zg_ref, h_ref, alog_ref, dskip_ref, dnorm_ref, yd_ref, hn_ref):
    l128 = lax.broadcasted_iota(i32, (1, 128), 1)
    a_row = jnp.where(l128 < N_HEADS, -jnp.exp(alog_ref[...]), 0.0)
    r = lax.broadcasted_iota(i32, (128, 256), 0)
    lane = lax.broadcasted_iota(i32, (128, 256), 1)
    r8 = lax.broadcasted_iota(i32, (8, 128), 0)
    l256 = lax.broadcasted_iota(i32, (1, 256), 1)
    zrow = jnp.zeros((1, 128), f32)
    group = range(act_ref.shape[0])
    xs_, cm_, lt_, rmat_ = [], [], [], []
    for g in group:
        act = act_ref[g]
        xs, bm, cm = act[:, 0:256], act[:, 256:512], act[:, 512:768]
        dtv = dtv_ref[g]
        dec_x = _per_head_lanes(jnp.exp(dtv * a_row))
        xdt = xs * _per_head_lanes(dtv)
        lrows = jnp.where((r == 0) & (lane < 128), xdt, 0.0) + jnp.where((r == 1) & (lane >= 128), xdt, 0.0)
        lt_.append((lrows + jnp.where(r == 2, dec_x, 0.0)).T)
        b0 = jnp.concatenate([bm[:, 0:128], zrow], axis=1)
        b1 = jnp.concatenate([bm[:, 128:256], zrow], axis=1)
        rmat_.append(jnp.where(r == 0, b0, 0.0) + jnp.where(r == 1, b1, 0.0)
                     + jnp.where((r == 2) & (lane >= 128), 1.0, 0.0))
        xs_.append(xs)
        cm_.append(cm)
    res_ = [_dot(lt_[g], rmat_[g], HI) for g in group]
    hn_ = [h_ref[g] * res_[g][:, 128:256] + res_[g][:, 0:128] for g in group]
    for g in group:
        hn_ref[g] = hn_[g]
    crows_ = [jnp.where(r8 == 0, cm_[g][:, 0:128], 0.0) + jnp.where(r8 == 1, cm_[g][:, 128:256], 0.0) for g in group]
    y8_ = [_dot_nt(crows_[g], hn_[g], HI) for g in group]
    for g in group:
        y = jnp.where(l256 < 128, y8_[g][0:1, :], y8_[g][1:2, :])
        yd_ref[g] = _gated_rms(y, xs_[g], zg_ref[g], dskip_ref[...], dnorm_ref[...])


def _dec_ssm(act, dtv, zg, h, alog, dskip, dnorm):
    nb = act.shape[0]
    per = lambda r, w: pl.BlockSpec((SSM_GROUP, r, w), lambda b: (b, 0, 0))
    full = lambda a: pl.BlockSpec(a.shape, lambda b: (0,) * a.ndim)
    return pl.pallas_call(
        _dec_ssm_kernel,
        grid=(nb // SSM_GROUP,),
        in_specs=[per(1, D_XBC), per(1, 128), per(1, 256), per(256, 128), full(alog), full(dskip), full(dnorm)],
        out_specs=[per(1, 256), per(256, 128)],
        out_shape=[jax.ShapeDtypeStruct((nb, 1, 256), f32), jax.ShapeDtypeStruct((nb, 256, 128), f32)],
        compiler_params=_cparams(("arbitrary",)), name="dec_ssm",
    )(act, dtv, zg, h, alog, dskip, dnorm)


def _rope_tables(pos, head_dim):
    rot = head_dim // 4
    half = rot // 2
    inv = ROPE_THETA ** (-jnp.arange(half, dtype=f32) * 2.0 / rot)
    ang = pos.astype(f32)[:, None] * inv[None, :]
    cos, sin = jnp.cos(ang), jnp.sin(ang)
    n = pos.shape[0]
    pad = jnp.zeros((n, head_dim - rot), f32)
    c = jnp.concatenate([cos, cos, pad + 1.0], axis=1)
    sm = jnp.concatenate([-sin, jnp.zeros((n, half), f32), pad], axis=1)
    sp = jnp.concatenate([jnp.zeros((n, half), f32), sin, pad], axis=1)
    rep = 128 // head_dim
    return [jnp.tile(t, (1, rep)) for t in (c, sm, sp)]


def _pad_lanes(v, width=128):
    return jnp.pad(v.astype(f32), (0, width - v.shape[0]))[None, :]


def _layer_params(l, w_in, w_out, g_pre_mix, g_post_mix, g_pre_ffn, g_post_ffn, bconv_w, cconv_w, cconv_b, cln_g,
                  cln_b, dconv_w, dconv_b, dt_bias, a_log, d_skip, d_norm, ffn_gate, ffn_up, ffn_down):
    w = w_in[l]
    cuts = np.cumsum((0,) + IN_SIZES)
    col = lambda j: w[:, cuts[j]:cuts[j + 1]]
    zpad = lambda a: jnp.pad(a, ((0, 0), (0, 128 - a.shape[1])))
    row = lambda a: a[l][None, :].astype(f32)
    wih, wil = _split(jnp.concatenate([col(3), jnp.tile(col(4), (1, 128 // IDX_DIM)),
                                       zpad(jnp.concatenate([col(5), col(12)], axis=1))], axis=1))
    return dict(
        wm=jnp.concatenate([w[:, :768], w[:, cuts[6]:cuts[12]]], axis=1).astype(bf16), wih=wih, wil=wil,
        g_pre_mix=row(g_pre_mix), g_post_mix=row(g_post_mix), g_pre_ffn=row(g_pre_ffn), g_post_ffn=row(g_post_ffn),
        w_out=w_out[l].astype(bf16), wg=ffn_gate[l].astype(bf16), wu=ffn_up[l].astype(bf16), wd=ffn_down[l].astype(bf16),
        bw=jnp.pad(bconv_w[l], ((0, 8 - B_CONV), (0, 0))), cw=jnp.pad(cconv_w[l], ((0, 32 - C_CONV), (0, 0))),
        cb=row(cconv_b), lng=row(cln_g), lnb=row(cln_b),
        dw=jnp.pad(dconv_w[l], ((0, 8 - D_CONV), (0, 0))), db=row(dconv_b),
        dtb=_pad_lanes(dt_bias[l]), alog=_pad_lanes(a_log[l]),
        dskip=jnp.repeat(d_skip[l].astype(f32), HEAD_DIM)[None, :], dnorm=row(d_norm),
    )


def _finish(p, ya, yb, yc, yd, x, tm):
    x1, hf = _outproj(ya, yb, yc, yd, x, p['w_out'], p['g_post_mix'], p['g_pre_ffn'], tm)
    return _ffn(hf, x1, p['wg'], p['wu'], p['wd'], p['g_post_ffn'], tm)


def kernel(x_prompt, x_sample, cache_k, cache_v, cache_kidx, page_table, state_bconv, state_cconv, state_dconv, state_ssm, w_in, w_out, g_pre_mix, g_post_mix, g_pre_ffn, g_post_ffn, bconv_w, cconv_w, cconv_b, cln_g, cln_b, dconv_w, dconv_b, dt_bias, a_log, d_skip, d_norm, ffn_gate, ffn_up, ffn_down):
    bsz, seq, _ = x_prompt.shape
    nb, t_dec, _ = x_sample.shape
    depth = w_in.shape[0]
    n_phys = cache_k.shape[1]
    n_pages = page_table.shape[1]
    past = n_pages * PAGE
    assert t_dec == 1 and seq % QB == 0 and seq % TC == 0 and past + t_dec > TOPK_MAX * 4
    assert nb % SSM_GROUP == 0 and nb % SCORE_GROUP == 0
    weights = (w_in, w_out, g_pre_mix, g_post_mix, g_pre_ffn, g_post_ffn, bconv_w, cconv_w, cconv_b, cln_g, cln_b,
               dconv_w, dconv_b, dt_bias, a_log, d_skip, d_norm, ffn_gate, ffn_up, ffn_down)
    pos_p = jnp.arange(seq, dtype=jnp.int32)
    pos_s = jnp.full((nb,), past, jnp.int32)
    tabs_p = _rope_tables(pos_p, HEAD_DIM) + _rope_tables(pos_p, IDX_DIM)
    tabs_s = _rope_tables(pos_s, HEAD_DIM) + _rope_tables(pos_s, IDX_DIM)
    ck = cache_k.transpose(0, 1, 3, 4, 2).reshape(depth * n_phys, 256, PAGE)
    cv = cache_v.transpose(0, 1, 3, 4, 2).reshape(depth * n_phys, 256, PAGE)
    cki = cache_kidx.transpose(0, 1, 3, 2).reshape(depth * n_phys, IDX_DIM, PAGE)
    pt = page_table.reshape(-1).astype(jnp.int32)
    lane_head = (jnp.arange(256) // HEAD_DIM)[None, None, :] == jnp.arange(8)[None, :, None]

    hp = x_prompt.reshape(bsz * seq, D_MODEL)
    hs = x_sample.reshape(nb, D_MODEL)
    outs_p, outs_s = [], []
    for l in range(depth):
        p = _layer_params(l, *weights)
        conv_prm = (p['bw'], p['cw'], p['cb'], p['lng'], p['lnb'], p['dw'], p['db'], p['dtb'])
        ssm_prm = (p['alog'], p['dskip'], p['dnorm'])

        q, k, kb, v, vt, qi, ki4, kcat, wi, dt, bgc, glu, zg, xbc = _inproj(
            hp, p['g_pre_mix'], p['wm'], p['wih'], p['wil'], tabs_p, 256)
        ya = _attn_prompt_t(q, qi, wi, kcat, kb, vt, bsz, seq)
        yb, yc, yd, nbp, ncp, ndp, ssm_p = _mix_prompt(bgc, glu, zg, xbc, dt, conv_prm + ssm_prm, bsz, seq)
        hp = _finish(p, ya, yb, yc, yd, hp, 512)
        outs_p.append((k.reshape(bsz, seq, N_HEADS, HEAD_DIM), v.reshape(bsz, seq, N_HEADS, HEAD_DIM),
                       ki4[:, :IDX_DIM].reshape(bsz, seq, IDX_DIM), nbp, ncp, ndp,
                       ssm_p.reshape(bsz, N_HEADS, HEAD_DIM, D_STATE)))

        q, k, kb, v, vt, qi, ki4, kcat, wi, dt, bgc, glu, zg, xbc = _inproj(
            hs, p['g_pre_mix'], p['wm'], p['wih'], p['wil'], tabs_s, nb)
        topk = min(TOPK_MAX, (past + t_dec) // 4)
        qi_s = qi.reshape(nb, N_IDX_HEADS, IDX_DIM)
        wi_t = jnp.broadcast_to((wi[:, :N_IDX_HEADS] * (IDX_DIM ** -0.5 * N_IDX_HEADS ** -0.5))[:, :, None],
                                (nb, N_IDX_HEADS, 128))
        sc = _dec_scores(pt, qi_s, wi_t, cki, n_pages, l * n_phys).reshape(nb, past)
        bias, bnew = _dec_select(sc, qi, ki4, wi, topk)
        qm = jnp.where(lane_head, q[:, None, :], 0.0)
        ya = _dec_attn(pt, qm, k.reshape(nb, 1, 256), v.reshape(nb, 1, 256), bias.reshape(nb, 1, past),
                       bnew.reshape(nb, 1, 128), ck, cv, n_pages, l * n_phys).reshape(nb, 256)
        yb, yc, act, dtv, nbs, ncs, nds = _dec_conv(
            bgc, glu, xbc, dt, state_bconv[l].reshape(nb, -1), state_cconv[l].reshape(nb, -1),
            state_dconv[l].reshape(nb, -1), conv_prm)
        yd, ssm_s = _dec_ssm(act.reshape(nb, 1, D_XBC), dtv.reshape(nb, 1, 128), zg.reshape(nb, 1, 256),
                             state_ssm[l].reshape(nb, 256, D_STATE), *ssm_prm)
        hs = _finish(p, ya, yb, yc, yd.reshape(nb, 256), hs, nb)
        outs_s.append((k.reshape(nb, 1, N_HEADS, HEAD_DIM), v.reshape(nb, 1, N_HEADS, HEAD_DIM),
                       ki4[:, :IDX_DIM].reshape(nb, 1, IDX_DIM), nbs.reshape(nb, 2, 256), ncs.reshape(nb, 30, 256),
                       nds.reshape(nb, 3, D_XBC), ssm_s.reshape(nb, N_HEADS, HEAD_DIM, D_STATE)))

    k_p, v_p, kidx_p, bconv_p, cconv_p, dconv_p, ssm_p = [jnp.stack(a) for a in zip(*outs_p)]
    k_s, v_s, kidx_s, bconv_s, cconv_s, dconv_s, ssm_s = [jnp.stack(a) for a in zip(*outs_s)]
    return (hp.reshape(bsz, seq, D_MODEL), hs.reshape(nb, t_dec, D_MODEL), k_p, v_p, kidx_p, k_s, v_s, kidx_s,
            bconv_p, bconv_s, cconv_p, cconv_s, dconv_p, dconv_s, ssm_p, ssm_s)
```

```python
import functools
import math

import jax
import jax.numpy as jnp
import numpy as np
from jax import lax
from jax.experimental import pallas as pl
from jax.experimental.pallas import tpu as pltpu

f32, bf16, i32 = jnp.float32, jnp.bfloat16, jnp.int32
HI = lax.Precision.HIGHEST

D_MODEL = 1024
PAGE = 128
GW = 256
HEAD_DIM = 64
N_HEADS = 4
N_IDX_HEADS = 8
IDX_DIM = 32
TOPK_MAX = 256
ROPE_THETA = 500000.0
C_CONV = 31
B_CONV = 3
D_CONV = 4
D_XBC = 768
D_STATE = 128
SSD_CHUNK = 128
D_FF = 2816
RMS_EPS = 1e-6
LN_EPS = 1e-5
IN_SIZES = (256, 256, 256, 256, 32, 8, 256, 256, 256, 512, 256, 768, 4)
INT_MIN = -(2 ** 31)
KEY_NEG_INF = INT_MIN + 0x7FFFFF
NEG = -1e30
VMEM_LIMIT = 56 * 1024 * 1024


def _dot_nt(a, b, prec=None):
    return lax.dot_general(a, b, (((1,), (1,)), ((), ())), precision=prec, preferred_element_type=f32)


def _dot(a, b, prec=None):
    return jnp.dot(a, b, precision=prec, preferred_element_type=f32)


def _cparams(sem):
    return pltpu.CompilerParams(dimension_semantics=sem, vmem_limit_bytes=VMEM_LIMIT)


def _rms(x, g):
    return x * lax.rsqrt(jnp.mean(x * x, axis=-1, keepdims=True) + RMS_EPS) * g


def _silu(x):
    return x * jax.nn.sigmoid(x)


def _key_to_float(key):
    return lax.bitcast_convert_type(jnp.where(key < 0, key ^ jnp.int32(0x7FFFFFFF), key), f32)


def _split_f32(x):
    c = x * (2.0 ** 16 + 1.0)
    hi = c - (c - x)
    return hi, x - hi


def _split(x):
    hi, lo = _split_f32(x)
    return hi.astype(bf16), lo.astype(bf16)


def _kth_largest(count_ge, shape, topk):
    def bit_step(bi, key):
        cand = key + lax.shift_left(jnp.int32(1), 31 - bi)
        ok = (cand <= KEY_NEG_INF) | (count_ge(_key_to_float(cand)) >= topk)
        return jnp.where(ok, cand, key)

    key = lax.fori_loop(0, 32, bit_step, jnp.full(shape, INT_MIN, i32))
    return key, _key_to_float(key)


def _refine_between_floats(count_ge, lo, key, rows, topk, steps=32):
    hi = _key_to_float(key + 1)

    def step(_, carry):
        lo, hi = carry
        mid = lo + (hi - lo) * 0.5
        ge = count_ge(mid) >= topk
        return jnp.where(rows & ge, mid, lo), jnp.where(rows & jnp.logical_not(ge), mid, hi)

    return lax.fori_loop(0, steps, step, (lo, hi))[0]


def _rope(v, c, sm, sp, half):
    outs = []
    for s in range(v.shape[1] // 128):
        xs = v[:, s * 128:(s + 1) * 128]
        outs.append(xs * c + pltpu.roll(xs, 128 - half, 1) * sm + pltpu.roll(xs, half, 1) * sp)
    return jnp.concatenate(outs, axis=1)


def _inproj_kernel(x_ref, g_ref, wm_ref, wih_ref, wil_ref, c64_ref, sm64_ref, sp64_ref, c32_ref, sm32_ref, sp32_ref,
                   q_ref, k_ref, kb_ref, v_ref, vt_ref, qi_ref, ki_ref, kcat_ref, wi_o_ref, dt_o_ref,
                   bgc_ref, glu_ref, zg_ref, xbc_ref):
    u = _rms(x_ref[...], g_ref[...])
    ub, ul = _split(u)

    def mm(c0, c1):
        return _dot(ub, wm_ref[:, c0:c1])

    def mm3(c0, c1):
        wh = wih_ref[:, c0:c1]
        return _dot(ub, wh) + _dot(ul, wh) + _dot(ub, wil_ref[:, c0:c1])

    c64, sm64, sp64 = c64_ref[...], sm64_ref[...], sp64_ref[...]
    q_ref[...] = _rope(mm(0, 256), c64, sm64, sp64, 8)
    k = _rope(mm(256, 512), c64, sm64, sp64, 8)
    k_ref[...] = k
    kb_ref[...] = k.astype(bf16)
    v = mm(512, 768)
    v_ref[...] = v
    vt_ref[...] = v.T.astype(bf16)
    bgc_ref[...] = mm(768, 1536)
    glu_ref[...] = mm(1536, 2048)
    zg_ref[...] = mm(2048, 2304)
    xbc_ref[...] = mm(2304, 3072)
    c32, sm32, sp32 = c32_ref[...], sm32_ref[...], sp32_ref[...]
    qi_ref[...] = _rope(mm3(0, 256), c32, sm32, sp32, 4)
    ki4 = _rope(mm3(256, 384), c32, sm32, sp32, 4)
    ki_ref[...] = ki4
    kh, kl = _split(ki4)
    lane = lax.broadcasted_iota(i32, ki4.shape, 1)
    kcat_ref[...] = jnp.where((lane >= IDX_DIM) & (lane < 2 * IDX_DIM), kl, kh)
    small = mm3(384, 512)
    wi_o_ref[...] = small
    dt_o_ref[...] = pltpu.roll(small, 128 - N_IDX_HEADS, 1)


def _inproj(x, g, wm, wih, wil, tabs, tm):
    t = x.shape[0]
    nt = t // tm
    ntab = tabs[0].shape[0] // tm
    row = lambda w: pl.BlockSpec((tm, w), lambda i: (i, 0))
    full = lambda a: pl.BlockSpec(a.shape, lambda i: (0,) * a.ndim)
    tab = pl.BlockSpec((tm, 128), lambda i: (i % ntab, 0))
    widths = (256, 256, 256, 256, 256, 256, 128, 128, 128, 128, 768, 512, 256, 768)
    dtypes = (f32, f32, bf16, f32, bf16, f32, f32, bf16, f32, f32, f32, f32, f32, f32)
    out_specs = [row(w) for w in widths]
    out_shape = [jax.ShapeDtypeStruct((t, w), d) for w, d in zip(widths, dtypes)]
    out_specs[4] = pl.BlockSpec((256, tm), lambda i: (0, i))
    out_shape[4] = jax.ShapeDtypeStruct((256, t), bf16)
    return pl.pallas_call(
        _inproj_kernel,
        grid=(nt,),
        in_specs=[row(D_MODEL), full(g), full(wm), full(wih), full(wil)] + [tab] * 6,
        out_specs=out_specs,
        out_shape=out_shape,
        compiler_params=_cparams(("arbitrary",)),
        name="inproj",
    )(x, g, wm, wih, wil, *tabs)


QB = 256
SUB = 128


def _attn_t_kernel(q_ref, qi_ref, wi_ref, kcat_ref, k_ref, vt_ref, o_ref, sc_ref, cat_ref, qm_ref, acc_ref, *, topk):
    i = pl.program_id(1)
    nkb = i + 1
    n_sub = QB // SUB
    hi, lo = _split_f32(qi_ref[...])
    hi_t, lo_t = hi.T, lo.T
    for h in range(N_IDX_HEADS):
        rs = slice(h * IDX_DIM, (h + 1) * IDX_DIM)
        cat_ref[h] = jnp.concatenate([hi_t[rs], hi_t[rs], lo_t[rs], jnp.zeros((IDX_DIM, QB), f32)], axis=0).astype(bf16)
    q_t = (q_ref[...] * (HEAD_DIM ** -0.5)).T
    row = lax.broadcasted_iota(i32, (256, QB), 0)
    for h in range(N_HEADS):
        qm_ref[:, h * QB:(h + 1) * QB] = jnp.where(row // HEAD_DIM == h, q_t, 0.0).astype(bf16)
    w8 =(wi_ref[...] * (IDX_DIM ** -0.5 * N_IDX_HEADS ** -0.5)).T[0:N_IDX_HEADS]
    qpos = i * QB + lax.broadcasted_iota(i32, (1, QB), 1)
    kio = lax.broadcasted_iota(i32, (SUB, 1), 0)

    def tiles(kb):
        return [pl.ds(pl.multiple_of(kb * QB + j * SUB, SUB), SUB) for j in range(n_sub)]

    def score_block(kb, carry):
        for j, sl in enumerate(tiles(kb)):
            kc = kcat_ref[sl, :]
            acc = jnp.zeros((SUB, QB), f32)
            for h in range(N_IDX_HEADS):
                acc = acc + jnp.maximum(_dot(kc, cat_ref[h]), 0.0) * w8[h:h + 1, :]
            sc_ref[sl, :] = jnp.where(kb * QB + j * SUB + kio <= qpos, acc, -jnp.inf)
        return carry

    lax.fori_loop(0, nkb, score_block, 0)

    def count(pred):
        def body(kb, cnt):
            for sl in tiles(kb):
                cnt = cnt + jnp.where(pred(sc_ref[sl, :]), 1.0, 0.0).reshape(SUB // 8, 8, QB).sum(axis=0)
            return cnt

        def body2(kb2, cnt):
            return body(2 * kb2 + 1, body(2 * kb2, cnt))

        cnt = lax.fori_loop(0, nkb // 2, body2, jnp.zeros((8, QB), f32))
        cnt = lax.fori_loop(2 * (nkb // 2), nkb, body, cnt)
        return jnp.sum(cnt, axis=0, keepdims=True)

    def count_ge(t):
        return count(lambda s: s >= t)

    thr_key, thr = _kth_largest(count_ge, (1, QB), topk)
    real = thr > -jnp.inf
    tie = real & (count_ge(thr) > topk)
    any_tie = jnp.max(jnp.where(tie, 1.0, 0.0)) > 0.0

    @pl.when(jnp.logical_not(any_tie))
    def _():
        thr_fin = jnp.maximum(thr, jnp.finfo(f32).min)

        def mask_block(kb, carry):
            for sl in tiles(kb):
                sc_ref[sl, :] = jnp.where(sc_ref[sl, :] >= thr_fin, 0.0, NEG)
            return carry

        lax.fori_loop(0, nkb, mask_block, 0)

    @pl.when(any_tie)
    def _():
        thr2 = _refine_between_floats(count_ge, thr, thr_key, tie, topk)
        need = topk - count(lambda s: s > thr2)
        realf = jnp.where(real, 1.0, 0.0)
        r = lax.broadcasted_iota(i32, (SUB, SUB), 0)
        c = lax.broadcasted_iota(i32, (SUB, SUB), 1)
        tri = jnp.where(c <= r, 1.0, 0.0).astype(bf16)

        def tie_block(kb, carry):
            for sl in tiles(kb):
                s = sc_ref[sl, :]
                eqb = jnp.where(s == thr2, realf, 0.0).astype(bf16)
                cum = _dot(tri, eqb) + carry
                keep = jnp.where(s > thr2, 1.0, jnp.where(cum <= need, eqb.astype(f32), 0.0))
                sc_ref[sl, :] = jnp.where(keep > 0.0, 0.0, NEG)
                carry = cum[SUB - 1:SUB, :]
            return carry

        lax.fori_loop(0, nkb, tie_block, jnp.zeros((1, QB), f32))

    heads = [slice(h * HEAD_DIM, (h + 1) * HEAD_DIM) for h in range(N_HEADS)]

    def attn_blocks(kb0, carry, n):
        ms, ls = list(carry[0]), list(carry[1])
        sls = [pl.ds(pl.multiple_of((kb0 + j) * QB, QB), QB) for j in range(n)]
        s_alls = [_dot(k_ref[sl, :], qm_ref[...]) for sl in sls]
        alphas, pvs = [], []
        for j, sl in enumerate(sls):
            mask = sc_ref[sl, :]
            al, ps = [], []
            for h in range(N_HEADS):
                s = s_alls[j][:, h * QB:(h + 1) * QB] + mask
                m_new = jnp.maximum(ms[h], jnp.max(s, axis=0, keepdims=True))
                al.append(jnp.exp(ms[h] - m_new))
                p = jnp.exp(s - m_new)
                ls[h] = al[h] * ls[h] + jnp.sum(p, axis=0, keepdims=True)
                ps.append(p.astype(bf16))
                ms[h] = m_new
            alphas.append(al)
            pvs.append([_dot(vt_ref[heads[h], sl], ps[h]) for h in range(N_HEADS)])
        for j in range(n):
            for h in range(N_HEADS):
                acc_ref[heads[h], :] = alphas[j][h] * acc_ref[heads[h], :] + pvs[j][h]
        return tuple(ms), tuple(ls)

    acc_ref[...] = jnp.zeros_like(acc_ref)
    m0 = tuple(jnp.full((1, QB), NEG, f32) for _ in range(N_HEADS))
    l0 = tuple(jnp.zeros((1, QB), f32) for _ in range(N_HEADS))
    carry = lax.fori_loop(0, nkb // 2, lambda kb2, c: attn_blocks(2 * kb2, c, 2), (m0, l0))
    _, ls = lax.fori_loop(2 * (nkb // 2), nkb, lambda kb, c: attn_blocks(kb, c, 1), carry)
    out_t = jnp.concatenate([acc_ref[h * HEAD_DIM:(h + 1) * HEAD_DIM, :] / ls[h] for h in range(N_HEADS)], axis=0)
    o_ref[...] = out_t.T


def _attn_prompt_t(q, qi, wi, kcat, kb, vt, bsz, seq):
    nq = seq // QB
    topk = min(TOPK_MAX, seq // 4)
    qrow = lambda w: pl.BlockSpec((QB, w), lambda b, i: (b * nq + i, 0))
    seqblk = lambda w: pl.BlockSpec((seq, w), lambda b, i: (b, 0))
    return pl.pallas_call(
        functools.partial(_attn_t_kernel, topk=float(topk)),
        grid=(bsz, nq),
        in_specs=[qrow(256), qrow(256), qrow(128), seqblk(128), seqblk(256),
                  pl.BlockSpec((256, seq), lambda b, i: (0, b))],
        out_specs=qrow(256),
        out_shape=jax.ShapeDtypeStruct((bsz * seq, 256), f32),
        scratch_shapes=[pltpu.VMEM((seq, QB), f32), pltpu.VMEM((N_IDX_HEADS, 128, QB), bf16),
                        pltpu.VMEM((256, N_HEADS * QB), bf16), pltpu.VMEM((256, QB), f32)],
        compiler_params=_cparams(("arbitrary", "arbitrary")),
        name="attn_prompt",
    )(q, qi, wi, kcat, kb, vt)


TC = 256
HALO_B, HALO_C, HALO_D = 8, 32, 8


def _per_head_lanes(a):
    lane = lax.broadcasted_iota(i32, (a.shape[0], N_HEADS * HEAD_DIM), 1)
    out = jnp.broadcast_to(a[:, N_HEADS - 1:N_HEADS], lane.shape)
    for h in range(N_HEADS - 2, -1, -1):
        out = jnp.where(lane < (h + 1) * HEAD_DIM, a[:, h:h + 1], out)
    return out


def _cumsum_rows(x):
    n = x.shape[0]
    r = lax.broadcasted_iota(i32, (n, n), 0)
    c = lax.broadcasted_iota(i32, (n, n), 1)
    tril = jnp.where(c <= r, 1.0, 0.0).astype(bf16)
    p1, rest = _split_f32(x)
    p2, p3 = _split_f32(rest)
    return _dot(tril, p1.astype(bf16)) + _dot(tril, p2.astype(bf16)) + _dot(tril, p3.astype(bf16))


def _ssd_chunk(xs, bm, cm, dtv, a_row, hcat_ref):
    r = lax.broadcasted_iota(i32, (128, 128), 0)
    c = lax.broadcasted_iota(i32, (128, 128), 1)
    causal = c <= r
    lane = lax.broadcasted_iota(i32, (128, 256), 1)
    da = dtv * a_row
    cs = _cumsum_rows(da)
    cs_t = cs.T
    dt_x = _per_head_lanes(dtv)
    ecs_x = _per_head_lanes(jnp.exp(cs))
    cs_last = cs[127:128, :]
    wend_x = _per_head_lanes(jnp.exp(cs_last - cs) * dtv)
    xdt = (xs * dt_x).astype(bf16)
    bmb, cmb = bm.astype(bf16), cm.astype(bf16)
    hb = hcat_ref[...].astype(bf16)
    y = jnp.zeros((128, 256), f32)
    ystate = []
    for g in range(2):
        cg = cmb[:, g * 128:(g + 1) * 128]
        cb = _dot_nt(cg, bmb[:, g * 128:(g + 1) * 128])
        ystate.append(_dot_nt(cg, hb))
        for h in (2 * g, 2 * g + 1):
            seg = cs[:, h:h + 1] - cs_t[h:h + 1, :]
            dec = jnp.where(causal, jnp.exp(jnp.where(causal, seg, 0.0)), 0.0)
            yh = _dot((cb * dec).astype(bf16), xdt)
            y = jnp.where(lane // 64 == h, yh, y)
    y = y + jnp.where(lane < 128, ystate[0], ystate[1]) * ecs_x
    xw_t = (xs * wend_x).T.astype(bf16)
    upd = jnp.concatenate([_dot(xw_t[0:128], bmb[:, 0:128]), _dot(xw_t[128:256], bmb[:, 128:256])], axis=0)
    elast = jnp.exp(cs_last)
    dcol = jnp.concatenate([jnp.broadcast_to(elast[:, h:h + 1], (HEAD_DIM, D_STATE)) for h in range(N_HEADS)], axis=0)
    hcat_ref[...] = hcat_ref[...] * dcol + upd
    return y


def _gated_rms(ys, xs, zg, dskip, dnorm):
    yg = (ys + dskip * xs) * _silu(zg)
    return yg * lax.rsqrt(jnp.mean(yg * yg, axis=-1, keepdims=True) + RMS_EPS) * dnorm


def _layer_norm_silu(x, g, b):
    mu = jnp.mean(x, axis=-1, keepdims=True)
    var = jnp.mean(jnp.square(x - mu), axis=-1, keepdims=True)
    return _silu((x - mu) * lax.rsqrt(var + LN_EPS) * g + b)


def _mix_kernel(bgc_ref, glu_ref, zg_ref, xbc_ref, dt_ref,
                bw_ref, cw_ref, cb_ref, lng_ref, lnb_ref, dw_ref, db_ref, dtb_ref, alog_ref, dskip_ref, dnorm_ref,
                yb_ref, yc_ref, yd_ref, nb_ref, nc_ref, nd_ref, ssm_ref,
                eb_ref, ec_ref, ed_ref, hcat_ref, pc_ref):
    i = pl.program_id(1)

    @pl.when(i == 0)
    def _():
        eb_ref[0:HALO_B, :] = jnp.zeros((HALO_B, 256), f32)
        ec_ref[0:HALO_C, :] = jnp.zeros((HALO_C, 256), f32)
        ed_ref[0:HALO_D, :] = jnp.zeros((HALO_D, D_XBC), f32)
        hcat_ref[...] = jnp.zeros_like(hcat_ref)

    bgc = bgc_ref[...]
    eb_ref[HALO_B:HALO_B + TC, :] = bgc[:, 512:768] * bgc[:, 0:256]
    conv = jnp.zeros((TC, 256), f32)
    for k in range(B_CONV):
        conv = conv + bw_ref[k:k + 1, :] * eb_ref[pl.ds(HALO_B - (B_CONV - 1) + k, TC), :]
    yb_ref[...] = bgc[:, 256:512] * conv
    nb_ref[0] = eb_ref[HALO_B + TC - (B_CONV - 1):HALO_B + TC, :]
    eb_ref[0:HALO_B, :] = eb_ref[TC:TC + HALO_B, :]

    glu = glu_ref[...]
    ec_ref[HALO_C:HALO_C + TC, :] = glu[:, 0:256] * jax.nn.sigmoid(glu[:, 256:512])
    off = HALO_C - (C_CONV - 1)
    conv = jnp.zeros((TC, 256), f32)
    for r in range(8):
        taps = [j for j in range(r, HALO_C + 1, 8) if off <= j < off + C_CONV]
        rows = TC if r == 0 else TC + 8
        part = jnp.zeros((rows, 256), f32)
        for j in taps:
            part = part + cw_ref[j - off:j - off + 1, :] * ec_ref[j - r:j - r + rows, :]
        if r == 0:
            conv = conv + part
        else:
            pc_ref[...] = part
            conv = conv + pc_ref[r:r + TC, :]
    yc_ref[...] = _layer_norm_silu(conv + cb_ref[...], lng_ref[...], lnb_ref[...])
    nc_ref[0] = ec_ref[HALO_C + TC - (C_CONV - 1):HALO_C + TC, :]
    ec_ref[0:HALO_C, :] = ec_ref[TC:TC + HALO_C, :]

    ed_ref[HALO_D:HALO_D + TC, :] = xbc_ref[...]
    conv = jnp.zeros((TC, D_XBC), f32)
    for k in range(D_CONV):
        conv = conv + dw_ref[k:k + 1, :] * ed_ref[pl.ds(HALO_D - (D_CONV - 1) + k, TC), :]
    act = _silu(conv + db_ref[...])
    nd_ref[0] = ed_ref[HALO_D + TC - (D_CONV - 1):HALO_D + TC, :]
    ed_ref[0:HALO_D, :] = ed_ref[TC:TC + HALO_D, :]
    l128 = lax.broadcasted_iota(i32, (1, 128), 1)
    a_row = jnp.where(l128 < N_HEADS, -jnp.exp(alog_ref[...]), 0.0)
    dtv = jax.nn.softplus(dt_ref[...] + dtb_ref[...])
    zg = zg_ref[...]
    for j in range(TC // SSD_CHUNK):
        rs = slice(j * SSD_CHUNK, (j + 1) * SSD_CHUNK)
        xs = act[rs, 0:256]
        y = _ssd_chunk(xs, act[rs, 256:512], act[rs, 512:768], dtv[rs], a_row, hcat_ref)
        yd_ref[rs, :] = _gated_rms(y, xs, zg[rs], dskip_ref[...], dnorm_ref[...])
    ssm_ref[0] = hcat_ref[...]


def _mix_prompt(bgc, glu, zg, xbc, dt, prm, bsz, seq):
    nt = seq // TC
    row = lambda w: pl.BlockSpec((TC, w), lambda b, i: (b * nt + i, 0))
    full = lambda a: pl.BlockSpec(a.shape, lambda b, i: (0,) * a.ndim)
    st = lambda r, w: pl.BlockSpec((1, r, w), lambda b, i: (b, 0, 0))
    t = bsz * seq
    return pl.pallas_call(
        _mix_kernel,
        grid=(bsz, nt),
        in_specs=[row(768), row(512), row(256), row(768), row(128)] + [full(a) for a in prm],
        out_specs=[row(256), row(256), row(256), st(2, 256), st(30, 256), st(3, D_XBC), st(256, 128)],
        out_shape=[jax.ShapeDtypeStruct((t, 256), f32)] * 3 + [
            jax.ShapeDtypeStruct((bsz, 2, 256), f32), jax.ShapeDtypeStruct((bsz, 30, 256), f32),
            jax.ShapeDtypeStruct((bsz, 3, D_XBC), f32), jax.ShapeDtypeStruct((bsz, 256, 128), f32)],
        scratch_shapes=[pltpu.VMEM((HALO_B + TC, 256), f32), pltpu.VMEM((HALO_C + TC, 256), f32),
                        pltpu.VMEM((HALO_D + TC, D_XBC), f32), pltpu.VMEM((256, 128), f32),
                        pltpu.VMEM((TC + 8, 256), f32)],
        compiler_params=_cparams(("arbitrary", "arbitrary")),
        name="mix_prompt",
    )(bgc, glu, zg, xbc, dt, *prm)


FF_CHUNK = 1408


def _outffn_kernel(ya_ref, yb_ref, yc_ref, yd_ref, x_ref, wo_ref, gpost_ref, gpre_ref, wg_ref, wu_ref, wd_ref, g_ref,
                   o_ref, acc_ref, x1_ref, hf_ref):
    j = pl.program_id(1)

    @pl.when(j == 0)
    def _():
        mix = _dot(ya_ref[...].astype(bf16), wo_ref[0:256, :])
        mix = mix + _dot(yb_ref[...].astype(bf16), wo_ref[256:512, :])
        mix = mix + _dot(yc_ref[...].astype(bf16), wo_ref[512:768, :])
        mix = mix + _dot(yd_ref[...].astype(bf16), wo_ref[768:1024, :])
        x1 = x_ref[...] + _rms(mix, gpost_ref[...])
        x1_ref[...] = x1
        hf_ref[...] = _rms(x1, gpre_ref[...]).astype(bf16)

    hf = hf_ref[...]
    a = _silu(_dot(hf, wg_ref[...])) * _dot(hf, wu_ref[...])
    part = _dot(a.astype(bf16), wd_ref[...])

    @pl.when(j == 0)
    def _():
        acc_ref[...] = part

    @pl.when(j > 0)
    def _():
        acc_ref[...] = acc_ref[...] + part

    @pl.when(j == pl.num_programs(1) - 1)
    def _():
        o_ref[...] = x1_ref[...] + _rms(acc_ref[...], g_ref[...])


def _outffn(ya, yb, yc, yd, x, wo, gpost, gpre, wg, wu, wd, g, tm):
    t = x.shape[0]
    nj = D_FF // FF_CHUNK
    row = lambda w_: pl.BlockSpec((tm, w_), lambda i, j: (i, 0))
    full = lambda a: pl.BlockSpec(a.shape, lambda i, j: (0,) * a.ndim)
    return pl.pallas_call(
        _outffn_kernel,
        grid=(t // tm, nj),
        in_specs=[row(256)] * 4 + [row(D_MODEL), full(wo), full(gpost), full(gpre),
                                   pl.BlockSpec((D_MODEL, FF_CHUNK), lambda i, j: (0, j)),
                                   pl.BlockSpec((D_MODEL, FF_CHUNK), lambda i, j: (0, j)),
                                   pl.BlockSpec((FF_CHUNK, D_MODEL), lambda i, j: (j, 0)), full(g)],
        out_specs=row(D_MODEL),
        out_shape=jax.ShapeDtypeStruct((t, D_MODEL), f32),
        scratch_shapes=[pltpu.VMEM((tm, D_MODEL), f32), pltpu.VMEM((tm, D_MODEL), f32),
                        pltpu.VMEM((tm, D_MODEL), bf16)],
        compiler_params=_cparams(("arbitrary", "arbitrary")),
        name="outffn",
    )(ya, yb, yc, yd, x, wo, gpost, gpre, wg, wu, wd, g)


def _page_copies(pt_ref, src_hbm, dst, sem, b, n_pages, base):
    return [pltpu.make_async_copy(src_hbm.at[base + pt_ref[b * n_pages + p]],
                                  dst.at[:, pl.ds(p * PAGE, PAGE)], sem) for p in range(n_pages)]


SCORE_GROUP = 8


def _dec_score_kernel(pt_ref, qi_ref, wi_ref, cki_hbm, o_ref, buf, sem, *, n_pages, base):
    g = pl.program_id(0)
    ng = pl.num_programs(0)

    def copies(gg, slot):
        return [c for j in range(SCORE_GROUP)
                for c in _page_copies(pt_ref, cki_hbm, buf.at[slot, j], sem.at[slot], gg * SCORE_GROUP + j, n_pages, base)]

    @pl.when(g == 0)
    def _():
        for c in copies(0, 0):
            c.start()

    @pl.when(g + 1 < ng)
    def _():
        for c in copies(g + 1, (g + 1) % 2):
            c.start()

    slot = g % 2
    for c in copies(g, slot):
        c.wait()
    for j in range(SCORE_GROUP):
        b = g * SCORE_GROUP + j
        s = _dot(qi_ref[b], buf[slot, j], HI)
        w = jnp.tile(wi_ref[b], (1, n_pages))
        o_ref[j] = jnp.sum(jnp.maximum(s, 0.0) * w, axis=0, keepdims=True)


def _dec_scores(pt, qi_s, wi_t, cki, n_pages, base):
    nb = qi_s.shape[0]
    past = n_pages * PAGE
    gs = pltpu.PrefetchScalarGridSpec(
        num_scalar_prefetch=1, grid=(nb // SCORE_GROUP,),
        in_specs=[pl.BlockSpec(qi_s.shape, lambda b, pt_: (0, 0, 0)), pl.BlockSpec(wi_t.shape, lambda b, pt_: (0, 0, 0)),
                  pl.BlockSpec(memory_space=pl.ANY)],
        out_specs=pl.BlockSpec((SCORE_GROUP, 1, past), lambda b, pt_: (b, 0, 0)),
        scratch_shapes=[pltpu.VMEM((2, SCORE_GROUP, IDX_DIM, past), f32), pltpu.SemaphoreType.DMA((2,))])
    return pl.pallas_call(
        functools.partial(_dec_score_kernel, n_pages=n_pages, base=base),
        grid_spec=gs, out_shape=jax.ShapeDtypeStruct((nb, 1, past), f32),
        compiler_params=_cparams(("arbitrary",)), name="dec_scores",
    )(pt, qi_s, wi_t, cki)


def _dec_select_kernel(sc_ref, qi_ref, ki_ref, wi_ref, bias_ref, bnew_ref, *, topk):
    nb, past = sc_ref.shape
    wi = wi_ref[...] * (IDX_DIM ** -0.5 * N_IDX_HEADS ** -0.5)
    gj = lax.broadcasted_iota(i32, (256, 128), 0)
    gh = lax.broadcasted_iota(i32, (256, 128), 1)
    seg = jnp.where(gj // IDX_DIM == gh, 1.0, 0.0)
    ki = jnp.concatenate([ki_ref[...], ki_ref[...]], axis=1)
    s_new = _dot(qi_ref[...] * ki, seg, HI)
    sc_new = jnp.broadcast_to(jnp.sum(jnp.maximum(s_new, 0.0) * wi, axis=-1, keepdims=True), (nb, 128))
    sc = sc_ref[...]
    ones = jnp.ones((past, 128), bf16)
    wide = lambda t: jnp.concatenate([t] * (past // 128), axis=1)

    def count(pred_past, pred_new):
        return _dot(jnp.where(pred_past, 1.0, 0.0).astype(bf16), ones) + jnp.where(pred_new, 1.0, 0.0)

    def count_ge(t):
        return count(sc >= wide(t), sc_new >= t)

    thr_key, thr = _kth_largest(count_ge, (nb, 128), topk)
    bias_ref[...] = jnp.where(sc >= wide(thr), 0.0, NEG)
    bnew_ref[...] = jnp.where(sc_new >= thr, 0.0, NEG)
    tie = count_ge(thr) > topk

    @pl.when(jnp.max(jnp.where(tie, 1.0, 0.0)) > 0.0)
    def _():
        thr2 = _refine_between_floats(count_ge, thr, thr_key, tie, topk)
        need = topk - count(sc > wide(thr2), sc_new > thr2)
        eqb = jnp.where(sc == wide(thr2), 1.0, 0.0).astype(bf16)
        r = lax.broadcasted_iota(i32, (PAGE, PAGE), 0)
        c = lax.broadcasted_iota(i32, (PAGE, PAGE), 1)
        tri = jnp.where(r <= c, 1.0, 0.0).astype(bf16)
        carry = jnp.zeros((nb, 128), f32)
        for p in range(past // PAGE):
            sl = slice(p * PAGE, (p + 1) * PAGE)
            cum = _dot(eqb[:, sl], tri) + carry
            keep = jnp.where(sc[:, sl] > thr2, 1.0, jnp.where(cum <= need, eqb[:, sl].astype(f32), 0.0))
            bias_ref[:, sl] = jnp.where(keep > 0.0, 0.0, NEG)
            carry = carry + _dot(eqb[:, sl], ones[0:PAGE])
        keep_new = jnp.where(sc_new > thr2, 1.0, jnp.where((sc_new == thr2) & (carry + 1.0 <= need), 1.0, 0.0))
        bnew_ref[...] = jnp.where(keep_new > 0.0, 0.0, NEG)


def _dec_select(sc, qi, ki4, wi, topk):
    nb, past = sc.shape
    return pl.pallas_call(
        functools.partial(_dec_select_kernel, topk=float(topk)),
        out_shape=[jax.ShapeDtypeStruct((nb, past), f32), jax.ShapeDtypeStruct((nb, 128), f32)],
        compiler_params=pltpu.CompilerParams(vmem_limit_bytes=VMEM_LIMIT), name="dec_select",
    )(sc, qi, ki4, wi)


def _dec_attn_kernel(pt_ref, qm_ref, kn_ref, vn_ref, bias_ref, bnew_ref, ck_hbm, cv_hbm, o_ref,
                     kbuf, vbuf, sem, *, n_pages, base):
    b = pl.program_id(0)
    nb = pl.num_programs(0)

    def copies(bb, slot):
        return (_page_copies(pt_ref, ck_hbm, kbuf.at[slot], sem.at[0, slot], bb, n_pages, base)
                + _page_copies(pt_ref, cv_hbm, vbuf.at[slot], sem.at[1, slot], bb, n_pages, base))

    @pl.when(b == 0)
    def _():
        for c in copies(0, 0):
            c.start()

    @pl.when(b + 1 < nb)
    def _():
        for c in copies(b + 1, (b + 1) % 2):
            c.start()

    slot = b % 2
    for c in copies(b, slot):
        c.wait()
    qm = qm_ref[0] * (HEAD_DIM ** -0.5)
    s = _dot(qm.astype(bf16), kbuf[slot].astype(bf16)) + bias_ref[0]
    s_new = jnp.sum(qm * kn_ref[0], axis=-1, keepdims=True) + bnew_ref[0][:, 0:1]
    m = jnp.maximum(jnp.max(s, axis=-1, keepdims=True), s_new)
    p = jnp.exp(s - m)
    p_new = jnp.exp(s_new - m)
    den = jnp.sum(p, axis=-1, keepdims=True) + p_new
    out8 = (_dot_nt(p.astype(bf16), vbuf[slot].astype(bf16)) + p_new * vn_ref[0]) / den
    row = lax.broadcasted_iota(i32, (8, 256), 0)
    lane = lax.broadcasted_iota(i32, (8, 256), 1)
    o_ref[0] = jnp.sum(jnp.where(lane // HEAD_DIM == row, out8, 0.0), axis=0, keepdims=True)


def _dec_attn(pt, qm, kn, vn, bias, bnew, ck, cv, n_pages, base):
    nb = qm.shape[0]
    past = n_pages * PAGE
    per = lambda r, w: pl.BlockSpec((1, r, w), lambda b, pt_: (b, 0, 0))
    gs = pltpu.PrefetchScalarGridSpec(
        num_scalar_prefetch=1, grid=(nb,),
        in_specs=[per(8, 256), per(1, 256), per(1, 256), per(1, past), per(1, 128),
                  pl.BlockSpec(memory_space=pl.ANY), pl.BlockSpec(memory_space=pl.ANY)],
        out_specs=per(1, 256),
        scratch_shapes=[pltpu.VMEM((2, 256, past), f32), pltpu.VMEM((2, 256, past), f32),
                        pltpu.SemaphoreType.DMA((2, 2))])
    return pl.pallas_call(
        functools.partial(_dec_attn_kernel, n_pages=n_pages, base=base),
        grid_spec=gs, out_shape=jax.ShapeDtypeStruct((nb, 1, 256), f32),
        compiler_params=_cparams(("arbitrary",)), name="dec_attn",
    )(pt, qm, kn, vn, bias, bnew, ck, cv)


def _dec_conv_kernel(bgc_ref, glu_ref, xbc_ref, dt_ref, sb_ref, sc_ref, sd_ref,
                     bw_ref, cw_ref, cb_ref, lng_ref, lnb_ref, dw_ref, db_ref, dtb_ref,
                     yb_ref, yc_ref, act_ref, dtv_ref, nb_ref, nc_ref, nd_ref):
    bgc = bgc_ref[...]
    ub = bgc[:, 512:768] * bgc[:, 0:256]
    sb = sb_ref[...]
    conv = bw_ref[0:1, :] * sb[:, 0:256] + bw_ref[1:2, :] * sb[:, 256:512] + bw_ref[2:3, :] * ub
    yb_ref[...] = bgc[:, 256:512] * conv
    nb_ref[...] = jnp.concatenate([sb[:, 256:512], ub], axis=1)

    glu = glu_ref[...]
    uc = glu[:, 0:256] * jax.nn.sigmoid(glu[:, 256:512])
    conv = cw_ref[C_CONV - 1:C_CONV, :] * uc
    for k in range(C_CONV - 1):
        conv = conv + cw_ref[k:k + 1, :] * sc_ref[:, k * 256:(k + 1) * 256]
    yc_ref[...] = _layer_norm_silu(conv + cb_ref[...], lng_ref[...], lnb_ref[...])
    nc_ref[:, 0:(C_CONV - 2) * 256] = sc_ref[:, 256:(C_CONV - 1) * 256]
    nc_ref[:, (C_CONV - 2) * 256:(C_CONV - 1) * 256] = uc

    xbc = xbc_ref[...]
    conv = dw_ref[D_CONV - 1:D_CONV, :] * xbc
    for k in range(D_CONV - 1):
        conv = conv + dw_ref[k:k + 1, :] * sd_ref[:, k * D_XBC:(k + 1) * D_XBC]
    act_ref[...] = _silu(conv + db_ref[...])
    nd_ref[:, 0:(D_CONV - 2) * D_XBC] = sd_ref[:, D_XBC:(D_CONV - 1) * D_XBC]
    nd_ref[:, (D_CONV - 2) * D_XBC:(D_CONV - 1) * D_XBC] = xbc
    dtv_ref[...] = jax.nn.softplus(dt_ref[...] + dtb_ref[...])


def _dec_conv(bgc, glu, xbc, dt, sb, sc, sd, prm):
    nb = bgc.shape[0]
    shp = lambda w: jax.ShapeDtypeStruct((nb, w), f32)
    return pl.pallas_call(
        _dec_conv_kernel,
        out_shape=[shp(256), shp(256), shp(D_XBC), shp(128), shp(2 * 256), shp(30 * 256), shp(3 * D_XBC)],
        compiler_params=pltpu.CompilerParams(vmem_limit_bytes=VMEM_LIMIT), name="dec_conv",
    )(bgc, glu, xbc, dt, sb, sc, sd, *prm)


SSM_GROUP = 8


def _dec_ssm_kernel(act_ref, dtv_ref, zg_ref, h_ref, alog_ref, dskip_ref, dnorm_ref, yd_ref, hn_ref):
    l128 = lax.broadcasted_iota(i32, (1, 128), 1)
    a_row = jnp.where(l128 < N_HEADS, -jnp.exp(alog_ref[...]), 0.0)
    r = lax.broadcasted_iota(i32, (128, 256), 0)
    lane = lax.broadcasted_iota(i32, (128, 256), 1)
    r8 = lax.broadcasted_iota(i32, (8, 128), 0)
    l256 = lax.broadcasted_iota(i32, (1, 256), 1)
    zrow = jnp.zeros((1, 128), f32)
    group = range(act_ref.shape[0])
    xs_, cm_, lt_, rmat_ = [], [], [], []
    for g in group:
        act = act_ref[g]
        xs, bm, cm = act[:, 0:256], act[:, 256:512], act[:, 512:768]
        dtv = dtv_ref[g]
        dec_x = _per_head_lanes(jnp.exp(dtv * a_row))
        xdt = xs * _per_head_lanes(dtv)
        lrows = jnp.where((r == 0) & (lane < 128), xdt, 0.0) + jnp.where((r == 1) & (lane >= 128), xdt, 0.0)
        lt_.append((lrows + jnp.where(r == 2, dec_x, 0.0)).T)
        b0 = jnp.concatenate([bm[:, 0:128], zrow], axis=1)
        b1 = jnp.concatenate([bm[:, 128:256], zrow], axis=1)
        rmat_.append(jnp.where(r == 0, b0, 0.0) + jnp.where(r == 1, b1, 0.0)
                     + jnp.where((r == 2) & (lane >= 128), 1.0, 0.0))
        xs_.append(xs)
        cm_.append(cm)
    res_ = [_dot(lt_[g], rmat_[g], HI) for g in group]
    hn_ = [h_ref[g] * res_[g][:, 128:256] + res_[g][:, 0:128] for g in group]
    for g in group:
        hn_ref[g] = hn_[g]
    crows_ = [jnp.where(r8 == 0, cm_[g][:, 0:128], 0.0) + jnp.where(r8 == 1, cm_[g][:, 128:256], 0.0) for g in group]
    y8_ = [_dot_nt(crows_[g], hn_[g], HI) for g in group]
    for g in group:
        y = jnp.where(l256 < 128, y8_[g][0:1, :], y8_[g][1:2, :])
        yd_ref[g] = _gated_rms(y, xs_[g], zg_ref[g], dskip_ref[...], dnorm_ref[...])


def _dec_ssm(act, dtv, zg, h, alog, dskip, dnorm):
    nb = act.shape[0]
    per = lambda r, w: pl.BlockSpec((SSM_GROUP, r, w), lambda b: (b, 0, 0))
    full = lambda a: pl.BlockSpec(a.shape, lambda b: (0,) * a.ndim)
    return pl.pallas_call(
        _dec_ssm_kernel,
        grid=(nb // SSM_GROUP,),
        in_specs=[per(1, D_XBC), per(1, 128), per(1, 256), per(256, 128), full(alog), full(dskip), full(dnorm)],
        out_specs=[per(1, 256), per(256, 128)],
        out_shape=[jax.ShapeDtypeStruct((nb, 1, 256), f32), jax.ShapeDtypeStruct((nb, 256, 128), f32)],
        compiler_params=_cparams(("arbitrary",)), name="dec_ssm",
    )(act, dtv, zg, h, alog, dskip, dnorm)


def _rope_tables(pos, head_dim):
    rot = head_dim // 4
    half = rot // 2
    inv = ROPE_THETA ** (-jnp.arange(half, dtype=f32) * 2.0 / rot)
    ang = pos.astype(f32)[:, None] * inv[None, :]
    cos, sin = jnp.cos(ang), jnp.sin(ang)
    n = pos.shape[0]
    pad = jnp.zeros((n, head_dim - rot), f32)
    c = jnp.concatenate([cos, cos, pad + 1.0], axis=1)
    sm = jnp.concatenate([-sin, jnp.zeros((n, half), f32), pad], axis=1)
    sp = jnp.concatenate([jnp.zeros((n, half), f32), sin, pad], axis=1)
    rep = 128 // head_dim
    return [jnp.tile(t, (1, rep)) for t in (c, sm, sp)]


def _pad_lanes(v, width=128):
    return jnp.pad(v.astype(f32), (0, width - v.shape[0]))[None, :]


def _layer_params(l, w_in, w_out, g_pre_mix, g_post_mix, g_pre_ffn, g_post_ffn, bconv_w, cconv_w, cconv_b, cln_g,
                  cln_b, dconv_w, dconv_b, dt_bias, a_log, d_skip, d_norm, ffn_gate, ffn_up, ffn_down):
    w = w_in[l]
    cuts = np.cumsum((0,) + IN_SIZES)
    col = lambda j: w[:, cuts[j]:cuts[j + 1]]
    zpad = lambda a: jnp.pad(a, ((0, 0), (0, 128 - a.shape[1])))
    row = lambda a: a[l][None, :].astype(f32)
    wih, wil = _split(jnp.concatenate([col(3), jnp.tile(col(4), (1, 128 // IDX_DIM)),
                                       zpad(jnp.concatenate([col(5), col(12)], axis=1))], axis=1))
    return dict(
        wm=jnp.concatenate([w[:, :768], w[:, cuts[6]:cuts[12]]], axis=1).astype(bf16), wih=wih, wil=wil,
        g_pre_mix=row(g_pre_mix), g_post_mix=row(g_post_mix), g_pre_ffn=row(g_pre_ffn), g_post_ffn=row(g_post_ffn),
        w_out=w_out[l].astype(bf16), wg=ffn_gate[l].astype(bf16), wu=ffn_up[l].astype(bf16), wd=ffn_down[l].astype(bf16),
        bw=jnp.pad(bconv_w[l], ((0, 8 - B_CONV), (0, 0))), cw=jnp.pad(cconv_w[l], ((0, 32 - C_CONV), (0, 0))),
        cb=row(cconv_b), lng=row(cln_g), lnb=row(cln_b),
        dw=jnp.pad(dconv_w[l], ((0, 8 - D_CONV), (0, 0))), db=row(dconv_b),
        dtb=_pad_lanes(dt_bias[l]), alog=_pad_lanes(a_log[l]),
        dskip=jnp.repeat(d_skip[l].astype(f32), HEAD_DIM)[None, :], dnorm=row(d_norm),
    )


def _finish(p, ya, yb, yc, yd, x, tm):
    return _outffn(ya, yb, yc, yd, x, p['w_out'], p['g_post_mix'], p['g_pre_ffn'], p['wg'], p['wu'], p['wd'],
                   p['g_post_ffn'], tm)


def kernel(x_prompt, x_sample, cache_k, cache_v, cache_kidx, page_table, state_bconv, state_cconv, state_dconv, state_ssm, w_in, w_out, g_pre_mix, g_post_mix, g_pre_ffn, g_post_ffn, bconv_w, cconv_w, cconv_b, cln_g, cln_b, dconv_w, dconv_b, dt_bias, a_log, d_skip, d_norm, ffn_gate, ffn_up, ffn_down):
    bsz, seq, _ = x_prompt.shape
    nb, t_dec, _ = x_sample.shape
    depth = w_in.shape[0]
    n_phys = cache_k.shape[1]
    n_pages = page_table.shape[1]
    past = n_pages * PAGE
    assert t_dec == 1 and seq % QB == 0 and seq % TC == 0 and past + t_dec > TOPK_MAX * 4
    assert nb % SSM_GROUP == 0 and nb % SCORE_GROUP == 0
    weights = (w_in, w_out, g_pre_mix, g_post_mix, g_pre_ffn, g_post_ffn, bconv_w, cconv_w, cconv_b, cln_g, cln_b,
               dconv_w, dconv_b, dt_bias, a_log, d_skip, d_norm, ffn_gate, ffn_up, ffn_down)
    pos_p = jnp.arange(seq, dtype=jnp.int32)
    pos_s = jnp.full((nb,), past, jnp.int32)
    tabs_p = _rope_tables(pos_p, HEAD_DIM) + _rope_tables(pos_p, IDX_DIM)
    tabs_s = _rope_tables(pos_s, HEAD_DIM) + _rope_tables(pos_s, IDX_DIM)
    ck = cache_k.transpose(0, 1, 3, 4, 2).reshape(depth * n_phys, 256, PAGE)
    cv = cache_v.transpose(0, 1, 3, 4, 2).reshape(depth * n_phys, 256, PAGE)
    cki = cache_kidx.transpose(0, 1, 3, 2).reshape(depth * n_phys, IDX_DIM, PAGE)
    pt = page_table.reshape(-1).astype(jnp.int32)
    lane_head = (jnp.arange(256) // HEAD_DIM)[None, None, :] == jnp.arange(8)[None, :, None]

    hp = x_prompt.reshape(bsz * seq, D_MODEL)
    hs = x_sample.reshape(nb, D_MODEL)
    outs_p, outs_s = [], []
    for l in range(depth):
        p = _layer_params(l, *weights)
        conv_prm = (p['bw'], p['cw'], p['cb'], p['lng'], p['lnb'], p['dw'], p['db'], p['dtb'])
        ssm_prm = (p['alog'], p['dskip'], p['dnorm'])

        q, k, kb, v, vt, qi, ki4, kcat, wi, dt, bgc, glu, zg, xbc = _inproj(
            hp, p['g_pre_mix'], p['wm'], p['wih'], p['wil'], tabs_p, 256)
        ya = _attn_prompt_t(q, qi, wi, kcat, kb, vt, bsz, seq)
        yb, yc, yd, nbp, ncp, ndp, ssm_p = _mix_prompt(bgc, glu, zg, xbc, dt, conv_prm + ssm_prm, bsz, seq)
        hp = _finish(p, ya, yb, yc, yd, hp, 512)
        outs_p.append((k.reshape(bsz, seq, N_HEADS, HEAD_DIM), v.reshape(bsz, seq, N_HEADS, HEAD_DIM),
                       ki4[:, :IDX_DIM].reshape(bsz, seq, IDX_DIM), nbp, ncp, ndp,
                       ssm_p.reshape(bsz, N_HEADS, HEAD_DIM, D_STATE)))

        q, k, kb, v, vt, qi, ki4, kcat, wi, dt, bgc, glu, zg, xbc = _inproj(
            hs, p['g_pre_mix'], p['wm'], p['wih'], p['wil'], tabs_s, nb)
        topk = min(TOPK_MAX, (past + t_dec) // 4)
        qi_s = qi.reshape(nb, N_IDX_HEADS, IDX_DIM)
        wi_t = jnp.broadcast_to((wi[:, :N_IDX_HEADS] * (IDX_DIM ** -0.5 * N_IDX_HEADS ** -0.5))[:, :, None],
                                (nb, N_IDX_HEADS, 128))
        sc = _dec_scores(pt, qi_s, wi_t, cki, n_pages, l * n_phys).reshape(nb, past)
        bias, bnew = _dec_select(sc, qi, ki4, wi, topk)
        qm = jnp.where(lane_head, q[:, None, :], 0.0)
        ya = _dec_attn(pt, qm, k.reshape(nb, 1, 256), v.reshape(nb, 1, 256), bias.reshape(nb, 1, past),
                       bnew.reshape(nb, 1, 128), ck, cv, n_pages, l * n_phys).reshape(nb, 256)
        yb, yc, act, dtv, nbs, ncs, nds = _dec_conv(
            bgc, glu, xbc, dt, state_bconv[l].reshape(nb, -1), state_cconv[l].reshape(nb, -1),
            state_dconv[l].reshape(nb, -1), conv_prm)
        yd, ssm_s = _dec_ssm(act.reshape(nb, 1, D_XBC), dtv.reshape(nb, 1, 128), zg.reshape(nb, 1, 256),
                             state_ssm[l].reshape(nb, 256, D_STATE), *ssm_prm)
        hs = _finish(p, ya, yb, yc, yd.reshape(nb, 256), hs, nb)
        outs_s.append((k.reshape(nb, 1, N_HEADS, HEAD_DIM), v.reshape(nb, 1, N_HEADS, HEAD_DIM),
                       ki4[:, :IDX_DIM].reshape(nb, 1, IDX_DIM), nbs.reshape(nb, 2, 256), ncs.reshape(nb, 30, 256),
                       nds.reshape(nb, 3, D_XBC), ssm_s.reshape(nb, N_HEADS, HEAD_DIM, D_STATE)))

    k_p, v_p, kidx_p, bconv_p, cconv_p, dconv_p, ssm_p = [jnp.stack(a) for a in zip(*outs_p)]
    k_s, v_s, kidx_s, bconv_s, cconv_s, dconv_s, ssm_s = [jnp.stack(a) for a in zip(*outs_s)]
    return (hp.reshape(bsz, seq, D_MODEL), hs.reshape(nb, t_dec, D_MODEL), k_p, v_p, kidx_p, k_s, v_s, kidx_s,
            bconv_p, bconv_s, cconv_p, cconv_s, dconv_p, dconv_s, ssm_p, ssm_s)
```

```python
import functools
import math

import jax
import jax.numpy as jnp
import numpy as np
from jax import lax
from jax.experimental import pallas as pl
from jax.experimental.pallas import tpu as pltpu

f32, bf16, i32 = jnp.float32, jnp.bfloat16, jnp.int32
HI = lax.Precision.HIGHEST

D_MODEL = 1024
PAGE = 128
GW = 256
HEAD_DIM = 64
N_HEADS = 4
N_IDX_HEADS = 8
IDX_DIM = 32
TOPK_MAX = 256
ROPE_THETA = 500000.0
C_CONV = 31
B_CONV = 3
D_CONV = 4
D_XBC = 768
D_STATE = 128
SSD_CHUNK = 128
D_FF = 2816
RMS_EPS = 1e-6
LN_EPS = 1e-5
IN_SIZES = (256, 256, 256, 256, 32, 8, 256, 256, 256, 512, 256, 768, 4)
INT_MIN = -(2 ** 31)
KEY_NEG_INF = INT_MIN + 0x7FFFFF
NEG = -1e30
VMEM_LIMIT = 56 * 1024 * 1024


def _dot_nt(a, b, prec=None):
    return lax.dot_general(a, b, (((1,), (1,)), ((), ())), precision=prec, preferred_element_type=f32)


def _dot(a, b, prec=None):
    return jnp.dot(a, b, precision=prec, preferred_element_type=f32)


def _cparams(sem):
    return pltpu.CompilerParams(dimension_semantics=sem, vmem_limit_bytes=VMEM_LIMIT)


def _rms(x, g):
    return x * lax.rsqrt(jnp.mean(x * x, axis=-1, keepdims=True) + RMS_EPS) * g


def _silu(x):
    return x * jax.nn.sigmoid(x)


def _key_to_float(key):
    return lax.bitcast_convert_type(jnp.where(key < 0, key ^ jnp.int32(0x7FFFFFFF), key), f32)


def _split_f32(x):
    c = x * (2.0 ** 16 + 1.0)
    hi = c - (c - x)
    return hi, x - hi


def _split(x):
    hi, lo = _split_f32(x)
    return hi.astype(bf16), lo.astype(bf16)


def _kth_largest(count_ge, shape, topk):
    def bit_step(bi, key):
        cand = key + lax.shift_left(jnp.int32(1), 31 - bi)
        ok = (cand <= KEY_NEG_INF) | (count_ge(_key_to_float(cand)) >= topk)
        return jnp.where(ok, cand, key)

    key = lax.fori_loop(0, 32, bit_step, jnp.full(shape, INT_MIN, i32))
    return key, _key_to_float(key)


def _refine_between_floats(count_ge, lo, key, rows, topk, steps=32):
    hi = _key_to_float(key + 1)

    def split(lo, hi):
        mid = lo + (hi - lo) * 0.5
        return mid, rows & (mid > lo) & (mid < hi)

    def any_open(lo, hi):
        return jnp.max(jnp.where(split(lo, hi)[1], 1.0, 0.0))

    def cond(c):
        return (c[2] > 0.0) & (c[3] < steps)

    def body(c):
        lo, hi, _, it = c
        mid, is_open = split(lo, hi)
        ge = count_ge(mid) >= topk
        lo = jnp.where(is_open & ge, mid, lo)
        hi = jnp.where(is_open & jnp.logical_not(ge), mid, hi)
        return lo, hi, any_open(lo, hi), it + 1

    return lax.while_loop(cond, body, (lo, hi, any_open(lo, hi), jnp.int32(0)))[0]


def _rope(v, c, sm, sp, half):
    outs = []
    for s in range(v.shape[1] // 128):
        xs = v[:, s * 128:(s + 1) * 128]
        outs.append(xs * c + pltpu.roll(xs, 128 - half, 1) * sm + pltpu.roll(xs, half, 1) * sp)
    return jnp.concatenate(outs, axis=1)


def _inproj_kernel(x_ref, g_ref, wm_ref, wih_ref, wil_ref, c64_ref, sm64_ref, sp64_ref, c32_ref, sm32_ref, sp32_ref,
                   q_ref, k_ref, kb_ref, v_ref, vt_ref, qi_ref, ki_ref, kcat_ref, wi_o_ref, dt_o_ref,
                   bgc_ref, glu_ref, zg_ref, xbc_ref):
    u = _rms(x_ref[...], g_ref[...])
    ub, ul = _split(u)

    def mm(c0, c1):
        return _dot(ub, wm_ref[:, c0:c1])

    def mm3(c0, c1):
        wh = wih_ref[:, c0:c1]
        return _dot(ub, wh) + _dot(ul, wh) + _dot(ub, wil_ref[:, c0:c1])

    c64, sm64, sp64 = c64_ref[...], sm64_ref[...], sp64_ref[...]
    q_ref[...] = _rope(mm(0, 256), c64, sm64, sp64, 8)
    k = _rope(mm(256, 512), c64, sm64, sp64, 8)
    k_ref[...] = k
    kb_ref[...] = k.astype(bf16)
    v = mm(512, 768)
    v_ref[...] = v
    vt_ref[...] = v.T.astype(bf16)
    bgc_ref[...] = mm(768, 1536)
    glu_ref[...] = mm(1536, 2048)
    zg_ref[...] = mm(2048, 2304)
    xbc_ref[...] = mm(2304, 3072)
    c32, sm32, sp32 = c32_ref[...], sm32_ref[...], sp32_ref[...]
    qi_ref[...] = _rope(mm3(0, 256), c32, sm32, sp32, 4)
    ki4 = _rope(mm3(256, 384), c32, sm32, sp32, 4)
    ki_ref[...] = ki4
    kh, kl = _split(ki4)
    lane = lax.broadcasted_iota(i32, ki4.shape, 1)
    kcat_ref[...] = jnp.where((lane >= IDX_DIM) & (lane < 2 * IDX_DIM), kl, kh)
    small = mm3(384, 512)
    wi_o_ref[...] = small
    dt_o_ref[...] = pltpu.roll(small, 128 - N_IDX_HEADS, 1)


def _inproj(x, g, wm, wih, wil, tabs, tm):
    t = x.shape[0]
    nt = t // tm
    ntab = tabs[0].shape[0] // tm
    row = lambda w: pl.BlockSpec((tm, w), lambda i: (i, 0))
    full = lambda a: pl.BlockSpec(a.shape, lambda i: (0,) * a.ndim)
    tab = pl.BlockSpec((tm, 128), lambda i: (i % ntab, 0))
    widths = (256, 256, 256, 256, 256, 256, 128, 128, 128, 128, 768, 512, 256, 768)
    dtypes = (f32, f32, bf16, f32, bf16, f32, f32, bf16, f32, f32, f32, f32, f32, f32)
    out_specs = [row(w) for w in widths]
    out_shape = [jax.ShapeDtypeStruct((t, w), d) for w, d in zip(widths, dtypes)]
    out_specs[4] = pl.BlockSpec((256, tm), lambda i: (0, i))
    out_shape[4] = jax.ShapeDtypeStruct((256, t), bf16)
    return pl.pallas_call(
        _inproj_kernel,
        grid=(nt,),
        in_specs=[row(D_MODEL), full(g), full(wm), full(wih), full(wil)] + [tab] * 6,
        out_specs=out_specs,
        out_shape=out_shape,
        compiler_params=_cparams(("arbitrary",)),
        name="inproj",
    )(x, g, wm, wih, wil, *tabs)


QB = 256
SUB = 128


def _attn_t_kernel(q_ref, qi_ref, wi_ref, kcat_ref, k_ref, vt_ref, o_ref, sc_ref, cat_ref, qm_ref, acc_ref, *, topk):
    i = pl.program_id(1)
    nkb = i + 1
    n_sub = QB // SUB
    hi, lo = _split_f32(qi_ref[...])
    hi_t, lo_t = hi.T, lo.T
    for h in range(N_IDX_HEADS):
        rs = slice(h * IDX_DIM, (h + 1) * IDX_DIM)
        cat_ref[h] = jnp.concatenate([hi_t[rs], hi_t[rs], lo_t[rs], jnp.zeros((IDX_DIM, QB), f32)], axis=0).astype(bf16)
    q_t = (q_ref[...] * (HEAD_DIM ** -0.5)).T
    row = lax.broadcasted_iota(i32, (256, QB), 0)
    for h in range(N_HEADS):
        qm_ref[:, h * QB:(h + 1) * QB] = jnp.where(row // HEAD_DIM == h, q_t, 0.0).astype(bf16)
    w8 =(wi_ref[...] * (IDX_DIM ** -0.5 * N_IDX_HEADS ** -0.5)).T[0:N_IDX_HEADS]
    qpos = i * QB + lax.broadcasted_iota(i32, (1, QB), 1)
    kio = lax.broadcasted_iota(i32, (SUB, 1), 0)

    def tiles(kb):
        return [pl.ds(pl.multiple_of(kb * QB + j * SUB, SUB), SUB) for j in range(n_sub)]

    def score_block(kb, carry):
        for j, sl in enumerate(tiles(kb)):
            kc = kcat_ref[sl, :]
            acc = jnp.zeros((SUB, QB), f32)
            for h in range(N_IDX_HEADS):
                acc = acc + jnp.maximum(_dot(kc, cat_ref[h]), 0.0) * w8[h:h + 1, :]
            sc_ref[sl, :] = jnp.where(kb * QB + j * SUB + kio <= qpos, acc, -jnp.inf)
        return carry

    lax.fori_loop(0, nkb, score_block, 0)

    def count(pred):
        def body(kb, cnt):
            for sl in tiles(kb):
                cnt = cnt + jnp.where(pred(sc_ref[sl, :]), 1.0, 0.0).reshape(SUB // 8, 8, QB).sum(axis=0)
            return cnt

        def body2(kb2, cnt):
            return body(2 * kb2 + 1, body(2 * kb2, cnt))

        cnt = lax.fori_loop(0, nkb // 2, body2, jnp.zeros((8, QB), f32))
        cnt = lax.fori_loop(2 * (nkb // 2), nkb, body, cnt)
        return jnp.sum(cnt, axis=0, keepdims=True)

    def count_ge(t):
        return count(lambda s: s >= t)

    thr_key, thr = _kth_largest(count_ge, (1, QB), topk)
    real = thr > -jnp.inf
    tie = real & (count_ge(thr) > topk)
    any_tie = jnp.max(jnp.where(tie, 1.0, 0.0)) > 0.0

    @pl.when(jnp.logical_not(any_tie))
    def _():
        thr_fin = jnp.maximum(thr, jnp.finfo(f32).min)

        def mask_block(kb, carry):
            for sl in tiles(kb):
                sc_ref[sl, :] = jnp.where(sc_ref[sl, :] >= thr_fin, 0.0, NEG)
            return carry

        lax.fori_loop(0, nkb, mask_block, 0)

    @pl.when(any_tie)
    def _():
        thr2 = _refine_between_floats(count_ge, thr, thr_key, tie, topk)
        need = topk - count(lambda s: s > thr2)
        realf = jnp.where(real, 1.0, 0.0)
        r = lax.broadcasted_iota(i32, (SUB, SUB), 0)
        c = lax.broadcasted_iota(i32, (SUB, SUB), 1)
        tri = jnp.where(c <= r, 1.0, 0.0).astype(bf16)

        def tie_block(kb, carry):
            for sl in tiles(kb):
                s = sc_ref[sl, :]
                eqb = jnp.where(s == thr2, realf, 0.0).astype(bf16)
                cum = _dot(tri, eqb) + carry
                keep = jnp.where(s > thr2, 1.0, jnp.where(cum <= need, eqb.astype(f32), 0.0))
                sc_ref[sl, :] = jnp.where(keep > 0.0, 0.0, NEG)
                carry = cum[SUB - 1:SUB, :]
            return carry

        lax.fori_loop(0, nkb, tie_block, jnp.zeros((1, QB), f32))

    heads = [slice(h * HEAD_DIM, (h + 1) * HEAD_DIM) for h in range(N_HEADS)]

    def attn_blocks(kb0, carry, n):
        ms, ls = list(carry[0]), list(carry[1])
        sls = [pl.ds(pl.multiple_of((kb0 + j) * QB, QB), QB) for j in range(n)]
        s_alls = [_dot(k_ref[sl, :], qm_ref[...]) for sl in sls]
        alphas, pvs = [], []
        for j, sl in enumerate(sls):
            mask = sc_ref[sl, :]
            al, ps = [], []
            for h in range(N_HEADS):
                s = s_alls[j][:, h * QB:(h + 1) * QB] + mask
                m_new = jnp.maximum(ms[h], jnp.max(s, axis=0, keepdims=True))
                al.append(jnp.exp(ms[h] - m_new))
                p = jnp.exp(s - m_new)
                ls[h] = al[h] * ls[h] + jnp.sum(p, axis=0, keepdims=True)
                ps.append(p.astype(bf16))
                ms[h] = m_new
            alphas.append(al)
            pvs.append([_dot(vt_ref[heads[h], sl], ps[h]) for h in range(N_HEADS)])
        for j in range(n):
            for h in range(N_HEADS):
                acc_ref[heads[h], :] = alphas[j][h] * acc_ref[heads[h], :] + pvs[j][h]
        return tuple(ms), tuple(ls)

    acc_ref[...] = jnp.zeros_like(acc_ref)
    m0 = tuple(jnp.full((1, QB), NEG, f32) for _ in range(N_HEADS))
    l0 = tuple(jnp.zeros((1, QB), f32) for _ in range(N_HEADS))
    carry = lax.fori_loop(0, nkb // 2, lambda kb2, c: attn_blocks(2 * kb2, c, 2), (m0, l0))
    _, ls = lax.fori_loop(2 * (nkb // 2), nkb, lambda kb, c: attn_blocks(kb, c, 1), carry)
    out_t = jnp.concatenate([acc_ref[h * HEAD_DIM:(h + 1) * HEAD_DIM, :] / ls[h] for h in range(N_HEADS)], axis=0)
    o_ref[...] = out_t.T


def _attn_prompt_t(q, qi, wi, kcat, kb, vt, bsz, seq):
    nq = seq // QB
    topk = min(TOPK_MAX, seq // 4)
    qrow = lambda w: pl.BlockSpec((QB, w), lambda b, i: (b * nq + i, 0))
    seqblk = lambda w: pl.BlockSpec((seq, w), lambda b, i: (b, 0))
    return pl.pallas_call(
        functools.partial(_attn_t_kernel, topk=float(topk)),
        grid=(bsz, nq),
        in_specs=[qrow(256), qrow(256), qrow(128), seqblk(128), seqblk(256),
                  pl.BlockSpec((256, seq), lambda b, i: (0, b))],
        out_specs=qrow(256),
        out_shape=jax.ShapeDtypeStruct((bsz * seq, 256), f32),
        scratch_shapes=[pltpu.VMEM((seq, QB), f32), pltpu.VMEM((N_IDX_HEADS, 128, QB), bf16),
                        pltpu.VMEM((256, N_HEADS * QB), bf16), pltpu.VMEM((256, QB), f32)],
        compiler_params=_cparams(("arbitrary", "arbitrary")),
        name="attn_prompt",
    )(q, qi, wi, kcat, kb, vt)


TC = 256
HALO_B, HALO_C, HALO_D = 8, 32, 8


def _per_head_lanes(a):
    lane = lax.broadcasted_iota(i32, (a.shape[0], N_HEADS * HEAD_DIM), 1)
    out = jnp.broadcast_to(a[:, N_HEADS - 1:N_HEADS], lane.shape)
    for h in range(N_HEADS - 2, -1, -1):
        out = jnp.where(lane < (h + 1) * HEAD_DIM, a[:, h:h + 1], out)
    return out


def _cumsum_rows(x):
    n = x.shape[0]
    r = lax.broadcasted_iota(i32, (n, n), 0)
    c = lax.broadcasted_iota(i32, (n, n), 1)
    tril = jnp.where(c <= r, 1.0, 0.0).astype(bf16)
    p1, rest = _split_f32(x)
    p2, p3 = _split_f32(rest)
    return _dot(tril, p1.astype(bf16)) + _dot(tril, p2.astype(bf16)) + _dot(tril, p3.astype(bf16))


def _ssd_chunk(xs, bm, cm, dtv, a_row, hcat_ref):
    r = lax.broadcasted_iota(i32, (128, 128), 0)
    c = lax.broadcasted_iota(i32, (128, 128), 1)
    causal = c <= r
    lane = lax.broadcasted_iota(i32, (128, 256), 1)
    da = dtv * a_row
    cs = _cumsum_rows(da)
    cs_t = cs.T
    dt_x = _per_head_lanes(dtv)
    ecs_x = _per_head_lanes(jnp.exp(cs))
    cs_last = cs[127:128, :]
    wend_x = _per_head_lanes(jnp.exp(cs_last - cs) * dtv)
    xdt = (xs * dt_x).astype(bf16)
    bmb, cmb = bm.astype(bf16), cm.astype(bf16)
    hb = hcat_ref[...].astype(bf16)
    y = jnp.zeros((128, 256), f32)
    ystate = []
    for g in range(2):
        cg = cmb[:, g * 128:(g + 1) * 128]
        cb = _dot_nt(cg, bmb[:, g * 128:(g + 1) * 128])
        ystate.append(_dot_nt(cg, hb))
        for h in (2 * g, 2 * g + 1):
            seg = cs[:, h:h + 1] - cs_t[h:h + 1, :]
            dec = jnp.where(causal, jnp.exp(jnp.where(causal, seg, 0.0)), 0.0)
            yh = _dot((cb * dec).astype(bf16), xdt)
            y = jnp.where(lane // 64 == h, yh, y)
    y = y + jnp.where(lane < 128, ystate[0], ystate[1]) * ecs_x
    xw_t = (xs * wend_x).T.astype(bf16)
    upd = jnp.concatenate([_dot(xw_t[0:128], bmb[:, 0:128]), _dot(xw_t[128:256], bmb[:, 128:256])], axis=0)
    elast = jnp.exp(cs_last)
    dcol = jnp.concatenate([jnp.broadcast_to(elast[:, h:h + 1], (HEAD_DIM, D_STATE)) for h in range(N_HEADS)], axis=0)
    hcat_ref[...] = hcat_ref[...] * dcol + upd
    return y


def _gated_rms(ys, xs, zg, dskip, dnorm):
    yg = (ys + dskip * xs) * _silu(zg)
    return yg * lax.rsqrt(jnp.mean(yg * yg, axis=-1, keepdims=True) + RMS_EPS) * dnorm


def _layer_norm_silu(x, g, b):
    mu = jnp.mean(x, axis=-1, keepdims=True)
    var = jnp.mean(jnp.square(x - mu), axis=-1, keepdims=True)
    return _silu((x - mu) * lax.rsqrt(var + LN_EPS) * g + b)


def _mix_kernel(bgc_ref, glu_ref, zg_ref, xbc_ref, dt_ref,
                bw_ref, cw_ref, cb_ref, lng_ref, lnb_ref, dw_ref, db_ref, dtb_ref, alog_ref, dskip_ref, dnorm_ref,
                yb_ref, yc_ref, yd_ref, nb_ref, nc_ref, nd_ref, ssm_ref,
                eb_ref, ec_ref, ed_ref, hcat_ref, pc_ref):
    i = pl.program_id(1)

    @pl.when(i == 0)
    def _():
        eb_ref[0:HALO_B, :] = jnp.zeros((HALO_B, 256), f32)
        ec_ref[0:HALO_C, :] = jnp.zeros((HALO_C, 256), f32)
        ed_ref[0:HALO_D, :] = jnp.zeros((HALO_D, D_XBC), f32)
        hcat_ref[...] = jnp.zeros_like(hcat_ref)

    bgc = bgc_ref[...]
    eb_ref[HALO_B:HALO_B + TC, :] = bgc[:, 512:768] * bgc[:, 0:256]
    conv = jnp.zeros((TC, 256), f32)
    for k in range(B_CONV):
        conv = conv + bw_ref[k:k + 1, :] * eb_ref[pl.ds(HALO_B - (B_CONV - 1) + k, TC), :]
    yb_ref[...] = bgc[:, 256:512] * conv
    nb_ref[0] = eb_ref[HALO_B + TC - (B_CONV - 1):HALO_B + TC, :]
    eb_ref[0:HALO_B, :] = eb_ref[TC:TC + HALO_B, :]

    glu = glu_ref[...]
    ec_ref[HALO_C:HALO_C + TC, :] = glu[:, 0:256] * jax.nn.sigmoid(glu[:, 256:512])
    off = HALO_C - (C_CONV - 1)
    conv = jnp.zeros((TC, 256), f32)
    for r in range(8):
        taps = [j for j in range(r, HALO_C + 1, 8) if off <= j < off + C_CONV]
        rows = TC if r == 0 else TC + 8
        part = jnp.zeros((rows, 256), f32)
        for j in taps:
            part = part + cw_ref[j - off:j - off + 1, :] * ec_ref[j - r:j - r + rows, :]
        if r == 0:
            conv = conv + part
        else:
            pc_ref[...] = part
            conv = conv + pc_ref[r:r + TC, :]
    yc_ref[...] = _layer_norm_silu(conv + cb_ref[...], lng_ref[...], lnb_ref[...])
    nc_ref[0] = ec_ref[HALO_C + TC - (C_CONV - 1):HALO_C + TC, :]
    ec_ref[0:HALO_C, :] = ec_ref[TC:TC + HALO_C, :]

    ed_ref[HALO_D:HALO_D + TC, :] = xbc_ref[...]
    conv = jnp.zeros((TC, D_XBC), f32)
    for k in range(D_CONV):
        conv = conv + dw_ref[k:k + 1, :] * ed_ref[pl.ds(HALO_D - (D_CONV - 1) + k, TC), :]
    act = _silu(conv + db_ref[...])
    nd_ref[0] = ed_ref[HALO_D + TC - (D_CONV - 1):HALO_D + TC, :]
    ed_ref[0:HALO_D, :] = ed_ref[TC:TC + HALO_D, :]
    l128 = lax.broadcasted_iota(i32, (1, 128), 1)
    a_row = jnp.where(l128 < N_HEADS, -jnp.exp(alog_ref[...]), 0.0)
    dtv = jax.nn.softplus(dt_ref[...] + dtb_ref[...])
    zg = zg_ref[...]
    for j in range(TC // SSD_CHUNK):
        rs = slice(j * SSD_CHUNK, (j + 1) * SSD_CHUNK)
        xs = act[rs, 0:256]
        y = _ssd_chunk(xs, act[rs, 256:512], act[rs, 512:768], dtv[rs], a_row, hcat_ref)
        yd_ref[rs, :] = _gated_rms(y, xs, zg[rs], dskip_ref[...], dnorm_ref[...])
    ssm_ref[0] = hcat_ref[...]


def _mix_prompt(bgc, glu, zg, xbc, dt, prm, bsz, seq):
    nt = seq // TC
    row = lambda w: pl.BlockSpec((TC, w), lambda b, i: (b * nt + i, 0))
    full = lambda a: pl.BlockSpec(a.shape, lambda b, i: (0,) * a.ndim)
    st = lambda r, w: pl.BlockSpec((1, r, w), lambda b, i: (b, 0, 0))
    t = bsz * seq
    return pl.pallas_call(
        _mix_kernel,
        grid=(bsz, nt),
        in_specs=[row(768), row(512), row(256), row(768), row(128)] + [full(a) for a in prm],
        out_specs=[row(256), row(256), row(256), st(2, 256), st(30, 256), st(3, D_XBC), st(256, 128)],
        out_shape=[jax.ShapeDtypeStruct((t, 256), f32)] * 3 + [
            jax.ShapeDtypeStruct((bsz, 2, 256), f32), jax.ShapeDtypeStruct((bsz, 30, 256), f32),
            jax.ShapeDtypeStruct((bsz, 3, D_XBC), f32), jax.ShapeDtypeStruct((bsz, 256, 128), f32)],
        scratch_shapes=[pltpu.VMEM((HALO_B + TC, 256), f32), pltpu.VMEM((HALO_C + TC, 256), f32),
                        pltpu.VMEM((HALO_D + TC, D_XBC), f32), pltpu.VMEM((256, 128), f32),
                        pltpu.VMEM((TC + 8, 256), f32)],
        compiler_params=_cparams(("arbitrary", "arbitrary")),
        name="mix_prompt",
    )(bgc, glu, zg, xbc, dt, *prm)


FF_CHUNK = 1408


def _outffn_kernel(ya_ref, yb_ref, yc_ref, yd_ref, x_ref, wo_ref, gpost_ref, gpre_ref, wg_ref, wu_ref, wd_ref, g_ref,
                   o_ref, acc_ref, x1_ref, hf_ref):
    j = pl.program_id(1)

    @pl.when(j == 0)
    def _():
        mix = _dot(ya_ref[...].astype(bf16), wo_ref[0:256, :])
        mix = mix + _dot(yb_ref[...].astype(bf16), wo_ref[256:512, :])
        mix = mix + _dot(yc_ref[...].astype(bf16), wo_ref[512:768, :])
        mix = mix + _dot(yd_ref[...].astype(bf16), wo_ref[768:1024, :])
        x1 = x_ref[...] + _rms(mix, gpost_ref[...])
        x1_ref[...] = x1
        hf_ref[...] = _rms(x1, gpre_ref[...]).astype(bf16)

    hf = hf_ref[...]
    a = _silu(_dot(hf, wg_ref[...])) * _dot(hf, wu_ref[...])
    part = _dot(a.astype(bf16), wd_ref[...])

    @pl.when(j == 0)
    def _():
        acc_ref[...] = part

    @pl.when(j > 0)
    def _():
        acc_ref[...] = acc_ref[...] + part

    @pl.when(j == pl.num_programs(1) - 1)
    def _():
        o_ref[...] = x1_ref[...] + _rms(acc_ref[...], g_ref[...])


def _outffn(ya, yb, yc, yd, x, wo, gpost, gpre, wg, wu, wd, g, tm):
    t = x.shape[0]
    nj = D_FF // FF_CHUNK
    row = lambda w_: pl.BlockSpec((tm, w_), lambda i, j: (i, 0))
    full = lambda a: pl.BlockSpec(a.shape, lambda i, j: (0,) * a.ndim)
    return pl.pallas_call(
        _outffn_kernel,
        grid=(t // tm, nj),
        in_specs=[row(256)] * 4 + [row(D_MODEL), full(wo), full(gpost), full(gpre),
                                   pl.BlockSpec((D_MODEL, FF_CHUNK), lambda i, j: (0, j)),
                                   pl.BlockSpec((D_MODEL, FF_CHUNK), lambda i, j: (0, j)),
                                   pl.BlockSpec((FF_CHUNK, D_MODEL), lambda i, j: (j, 0)), full(g)],
        out_specs=row(D_MODEL),
        out_shape=jax.ShapeDtypeStruct((t, D_MODEL), f32),
        scratch_shapes=[pltpu.VMEM((tm, D_MODEL), f32), pltpu.VMEM((tm, D_MODEL), f32),
                        pltpu.VMEM((tm, D_MODEL), bf16)],
        compiler_params=_cparams(("arbitrary", "arbitrary")),
        name="outffn",
    )(ya, yb, yc, yd, x, wo, gpost, gpre, wg, wu, wd, g)


def _page_copies(pt_ref, src_hbm, dst, sem, b, n_pages, base):
    return [pltpu.make_async_copy(src_hbm.at[base + pt_ref[b * n_pages + p]],
                                  dst.at[:, pl.ds(p * PAGE, PAGE)], sem) for p in range(n_pages)]


SCORE_GROUP = 8


def _dec_score_kernel(pt_ref, qi_ref, wi_ref, cki_hbm, o_ref, buf, sem, *, n_pages, base):
    g = pl.program_id(0)
    ng = pl.num_programs(0)

    def copies(gg, slot):
        return [c for j in range(SCORE_GROUP)
                for c in _page_copies(pt_ref, cki_hbm, buf.at[slot, j], sem.at[slot], gg * SCORE_GROUP + j, n_pages, base)]

    @pl.when(g == 0)
    def _():
        for c in copies(0, 0):
            c.start()

    @pl.when(g + 1 < ng)
    def _():
        for c in copies(g + 1, (g + 1) % 2):
            c.start()

    slot = g % 2
    for c in copies(g, slot):
        c.wait()
    for j in range(SCORE_GROUP):
        b = g * SCORE_GROUP + j
        s = _dot(qi_ref[b], buf[slot, j], HI)
        w = jnp.tile(wi_ref[b], (1, n_pages))
        o_ref[j] = jnp.sum(jnp.maximum(s, 0.0) * w, axis=0, keepdims=True)


def _dec_scores(pt, qi_s, wi_t, cki, n_pages, base):
    nb = qi_s.shape[0]
    past = n_pages * PAGE
    gs = pltpu.PrefetchScalarGridSpec(
        num_scalar_prefetch=1, grid=(nb // SCORE_GROUP,),
        in_specs=[pl.BlockSpec(qi_s.shape, lambda b, pt_: (0, 0, 0)), pl.BlockSpec(wi_t.shape, lambda b, pt_: (0, 0, 0)),
                  pl.BlockSpec(memory_space=pl.ANY)],
        out_specs=pl.BlockSpec((SCORE_GROUP, 1, past), lambda b, pt_: (b, 0, 0)),
        scratch_shapes=[pltpu.VMEM((2, SCORE_GROUP, IDX_DIM, past), f32), pltpu.SemaphoreType.DMA((2,))])
    return pl.pallas_call(
        functools.partial(_dec_score_kernel, n_pages=n_pages, base=base),
        grid_spec=gs, out_shape=jax.ShapeDtypeStruct((nb, 1, past), f32),
        compiler_params=_cparams(("arbitrary",)), name="dec_scores",
    )(pt, qi_s, wi_t, cki)


def _dec_select_kernel(sc_ref, qi_ref, ki_ref, wi_ref, bias_ref, bnew_ref, *, topk):
    nb, past = sc_ref.shape
    wi = wi_ref[...] * (IDX_DIM ** -0.5 * N_IDX_HEADS ** -0.5)
    gj = lax.broadcasted_iota(i32, (256, 128), 0)
    gh = lax.broadcasted_iota(i32, (256, 128), 1)
    seg = jnp.where(gj // IDX_DIM == gh, 1.0, 0.0)
    ki = jnp.concatenate([ki_ref[...], ki_ref[...]], axis=1)
    s_new = _dot(qi_ref[...] * ki, seg, HI)
    sc_new = jnp.broadcast_to(jnp.sum(jnp.maximum(s_new, 0.0) * wi, axis=-1, keepdims=True), (nb, 128))
    sc = sc_ref[...]
    ones = jnp.ones((past, 128), bf16)
    wide = lambda t: jnp.concatenate([t] * (past // 128), axis=1)

    def count(pred_past, pred_new):
        return _dot(jnp.where(pred_past, 1.0, 0.0).astype(bf16), ones) + jnp.where(pred_new, 1.0, 0.0)

    def count_ge(t):
        return count(sc >= wide(t), sc_new >= t)

    thr_key, thr = _kth_largest(count_ge, (nb, 128), topk)
    bias_ref[...] = jnp.where(sc >= wide(thr), 0.0, NEG)
    bnew_ref[...] = jnp.where(sc_new >= thr, 0.0, NEG)
    tie = count_ge(thr) > topk

    @pl.when(jnp.max(jnp.where(tie, 1.0, 0.0)) > 0.0)
    def _():
        thr2 = _refine_between_floats(count_ge, thr, thr_key, tie, topk)
        need = topk - count(sc > wide(thr2), sc_new > thr2)
        eqb = jnp.where(sc == wide(thr2), 1.0, 0.0).astype(bf16)
        r = lax.broadcasted_iota(i32, (PAGE, PAGE), 0)
        c = lax.broadcasted_iota(i32, (PAGE, PAGE), 1)
        tri = jnp.where(r <= c, 1.0, 0.0).astype(bf16)
        carry = jnp.zeros((nb, 128), f32)
        for p in range(past // PAGE):
            sl = slice(p * PAGE, (p + 1) * PAGE)
            cum = _dot(eqb[:, sl], tri) + carry
            keep = jnp.where(sc[:, sl] > thr2, 1.0, jnp.where(cum <= need, eqb[:, sl].astype(f32), 0.0))
            bias_ref[:, sl] = jnp.where(keep > 0.0, 0.0, NEG)
            carry = carry + _dot(eqb[:, sl], ones[0:PAGE])
        keep_new = jnp.where(sc_new > thr2, 1.0, jnp.where((sc_new == thr2) & (carry + 1.0 <= need), 1.0, 0.0))
        bnew_ref[...] = jnp.where(keep_new > 0.0, 0.0, NEG)


def _dec_select(sc, qi, ki4, wi, topk):
    nb, past = sc.shape
    return pl.pallas_call(
        functools.partial(_dec_select_kernel, topk=float(topk)),
        out_shape=[jax.ShapeDtypeStruct((nb, past), f32), jax.ShapeDtypeStruct((nb, 128), f32)],
        compiler_params=pltpu.CompilerParams(vmem_limit_bytes=VMEM_LIMIT), name="dec_select",
    )(sc, qi, ki4, wi)


def _dec_attn_kernel(pt_ref, qm_ref, kn_ref, vn_ref, bias_ref, bnew_ref, ck_hbm, cv_hbm, o_ref,
                     kbuf, vbuf, sem, *, n_pages, base):
    b = pl.program_id(0)
    nb = pl.num_programs(0)

    def copies(bb, slot):
        return (_page_copies(pt_ref, ck_hbm, kbuf.at[slot], sem.at[0, slot], bb, n_pages, base)
                + _page_copies(pt_ref, cv_hbm, vbuf.at[slot], sem.at[1, slot], bb, n_pages, base))

    @pl.when(b == 0)
    def _():
        for c in copies(0, 0):
            c.start()

    @pl.when(b + 1 < nb)
    def _():
        for c in copies(b + 1, (b + 1) % 2):
            c.start()

    slot = b % 2
    for c in copies(b, slot):
        c.wait()
    qm = qm_ref[0] * (HEAD_DIM ** -0.5)
    s = _dot(qm.astype(bf16), kbuf[slot].astype(bf16)) + bias_ref[0]
    s_new = jnp.sum(qm * kn_ref[0], axis=-1, keepdims=True) + bnew_ref[0][:, 0:1]
    m = jnp.maximum(jnp.max(s, axis=-1, keepdims=True), s_new)
    p = jnp.exp(s - m)
    p_new = jnp.exp(s_new - m)
    den = jnp.sum(p, axis=-1, keepdims=True) + p_new
    out8 = (_dot_nt(p.astype(bf16), vbuf[slot].astype(bf16)) + p_new * vn_ref[0]) / den
    row = lax.broadcasted_iota(i32, (8, 256), 0)
    lane = lax.broadcasted_iota(i32, (8, 256), 1)
    o_ref[0] = jnp.sum(jnp.where(lane // HEAD_DIM == row, out8, 0.0), axis=0, keepdims=True)


def _dec_attn(pt, qm, kn, vn, bias, bnew, ck, cv, n_pages, base):
    nb = qm.shape[0]
    past = n_pages * PAGE
    per = lambda r, w: pl.BlockSpec((1, r, w), lambda b, pt_: (b, 0, 0))
    gs = pltpu.PrefetchScalarGridSpec(
        num_scalar_prefetch=1, grid=(nb,),
        in_specs=[per(8, 256), per(1, 256), per(1, 256), per(1, past), per(1, 128),
                  pl.BlockSpec(memory_space=pl.ANY), pl.BlockSpec(memory_space=pl.ANY)],
        out_specs=per(1, 256),
        scratch_shapes=[pltpu.VMEM((2, 256, past), f32), pltpu.VMEM((2, 256, past), f32),
                        pltpu.SemaphoreType.DMA((2, 2))])
    return pl.pallas_call(
        functools.partial(_dec_attn_kernel, n_pages=n_pages, base=base),
        grid_spec=gs, out_shape=jax.ShapeDtypeStruct((nb, 1, 256), f32),
        compiler_params=_cparams(("arbitrary",)), name="dec_attn",
    )(pt, qm, kn, vn, bias, bnew, ck, cv)


def _dec_conv_kernel(bgc_ref, glu_ref, xbc_ref, dt_ref, sb_ref, sc_ref, sd_ref,
                     bw_ref, cw_ref, cb_ref, lng_ref, lnb_ref, dw_ref, db_ref, dtb_ref,
                     yb_ref, yc_ref, act_ref, dtv_ref, nb_ref, nc_ref, nd_ref):
    bgc = bgc_ref[...]
    ub = bgc[:, 512:768] * bgc[:, 0:256]
    sb = sb_ref[...]
    conv = bw_ref[0:1, :] * sb[:, 0:256] + bw_ref[1:2, :] * sb[:, 256:512] + bw_ref[2:3, :] * ub
    yb_ref[...] = bgc[:, 256:512] * conv
    nb_ref[...] = jnp.concatenate([sb[:, 256:512], ub], axis=1)

    glu = glu_ref[...]
    uc = glu[:, 0:256] * jax.nn.sigmoid(glu[:, 256:512])
    conv = cw_ref[C_CONV - 1:C_CONV, :] * uc
    for k in range(C_CONV - 1):
        conv = conv + cw_ref[k:k + 1, :] * sc_ref[:, k * 256:(k + 1) * 256]
    yc_ref[...] = _layer_norm_silu(conv + cb_ref[...], lng_ref[...], lnb_ref[...])
    nc_ref[:, 0:(C_CONV - 2) * 256] = sc_ref[:, 256:(C_CONV - 1) * 256]
    nc_ref[:, (C_CONV - 2) * 256:(C_CONV - 1) * 256] = uc

    xbc = xbc_ref[...]
    conv = dw_ref[D_CONV - 1:D_CONV, :] * xbc
    for k in range(D_CONV - 1):
        conv = conv + dw_ref[k:k + 1, :] * sd_ref[:, k * D_XBC:(k + 1) * D_XBC]
    act_ref[...] = _silu(conv + db_ref[...])
    nd_ref[:, 0:(D_CONV - 2) * D_XBC] = sd_ref[:, D_XBC:(D_CONV - 1) * D_XBC]
    nd_ref[:, (D_CONV - 2) * D_XBC:(D_CONV - 1) * D_XBC] = xbc
    dtv_ref[...] = jax.nn.softplus(dt_ref[...] + dtb_ref[...])


def _dec_conv(bgc, glu, xbc, dt, sb, sc, sd, prm):
    nb = bgc.shape[0]
    shp = lambda w: jax.ShapeDtypeStruct((nb, w), f32)
    return pl.pallas_call(
        _dec_conv_kernel,
        out_shape=[shp(256), shp(256), shp(D_XBC), shp(128), shp(2 * 256), shp(30 * 256), shp(3 * D_XBC)],
        compiler_params=pltpu.CompilerParams(vmem_limit_bytes=VMEM_LIMIT), name="dec_conv",
    )(bgc, glu, xbc, dt, sb, sc, sd, *prm)


SSM_GROUP = 8


def _dec_ssm_kernel(act_ref, dtv_ref, zg_ref, h_ref, alog_ref, dskip_ref, dnorm_ref, yd_ref, hn_ref):
    l128 = lax.broadcasted_iota(i32, (1, 128), 1)
    a_row = jnp.where(l128 < N_HEADS, -jnp.exp(alog_ref[...]), 0.0)
    r = lax.broadcasted_iota(i32, (128, 256), 0)
    lane = lax.broadcasted_iota(i32, (128, 256), 1)
    r8 = lax.broadcasted_iota(i32, (8, 128), 0)
    l256 = lax.broadcasted_iota(i32, (1, 256), 1)
    zrow = jnp.zeros((1, 128), f32)
    group = range(act_ref.shape[0])
    xs_, cm_, lt_, rmat_ = [], [], [], []
    for g in group:
        act = act_ref[g]
        xs, bm, cm = act[:, 0:256], act[:, 256:512], act[:, 512:768]
        dtv = dtv_ref[g]
        dec_x = _per_head_lanes(jnp.exp(dtv * a_row))
        xdt = xs * _per_head_lanes(dtv)
        lrows = jnp.where((r == 0) & (lane < 128), xdt, 0.0) + jnp.where((r == 1) & (lane >= 128), xdt, 0.0)
        lt_.append((lrows + jnp.where(r == 2, dec_x, 0.0)).T)
        b0 = jnp.concatenate([bm[:, 0:128], zrow], axis=1)
        b1 = jnp.concatenate([bm[:, 128:256], zrow], axis=1)
        rmat_.append(jnp.where(r == 0, b0, 0.0) + jnp.where(r == 1, b1, 0.0)
                     + jnp.where((r == 2) & (lane >= 128), 1.0, 0.0))
        xs_.append(xs)
        cm_.append(cm)
    res_ = [_dot(lt_[g], rmat_[g], HI) for g in group]
    hn_ = [h_ref[g] * res_[g][:, 128:256] + res_[g][:, 0:128] for g in group]
    for g in group:
        hn_ref[g] = hn_[g]
    crows_ = [jnp.where(r8 == 0, cm_[g][:, 0:128], 0.0) + jnp.where(r8 == 1, cm_[g][:, 128:256], 0.0) for g in group]
    y8_ = [_dot_nt(crows_[g], hn_[g], HI) for g in group]
    for g in group:
        y = jnp.where(l256 < 128, y8_[g][0:1, :], y8_[g][1:2, :])
        yd_ref[g] = _gated_rms(y, xs_[g], zg_ref[g], dskip_ref[...], dnorm_ref[...])


def _dec_ssm(act, dtv, zg, h, alog, dskip, dnorm):
    nb = act.shape[0]
    per = lambda r, w: pl.BlockSpec((SSM_GROUP, r, w), lambda b: (b, 0, 0))
    full = lambda a: pl.BlockSpec(a.shape, lambda b: (0,) * a.ndim)
    return pl.pallas_call(
        _dec_ssm_kernel,
        grid=(nb // SSM_GROUP,),
        in_specs=[per(1, D_XBC), per(1, 128), per(1, 256), per(256, 128), full(alog), full(dskip), full(dnorm)],
        out_specs=[per(1, 256), per(256, 128)],
        out_shape=[jax.ShapeDtypeStruct((nb, 1, 256), f32), jax.ShapeDtypeStruct((nb, 256, 128), f32)],
        compiler_params=_cparams(("arbitrary",)), name="dec_ssm",
    )(act, dtv, zg, h, alog, dskip, dnorm)


def _rope_tables(pos, head_dim):
    rot = head_dim // 4
    half = rot // 2
    inv = ROPE_THETA ** (-jnp.arange(half, dtype=f32) * 2.0 / rot)
    ang = pos.astype(f32)[:, None] * inv[None, :]
    cos, sin = jnp.cos(ang), jnp.sin(ang)
    n = pos.shape[0]
    pad = jnp.zeros((n, head_dim - rot), f32)
    c = jnp.concatenate([cos, cos, pad + 1.0], axis=1)
    sm = jnp.concatenate([-sin, jnp.zeros((n, half), f32), pad], axis=1)
    sp = jnp.concatenate([jnp.zeros((n, half), f32), sin, pad], axis=1)
    rep = 128 // head_dim
    return [jnp.tile(t, (1, rep)) for t in (c, sm, sp)]


def _pad_lanes(v, width=128):
    return jnp.pad(v.astype(f32), (0, width - v.shape[0]))[None, :]


def _layer_params(l, w_in, w_out, g_pre_mix, g_post_mix, g_pre_ffn, g_post_ffn, bconv_w, cconv_w, cconv_b, cln_g,
                  cln_b, dconv_w, dconv_b, dt_bias, a_log, d_skip, d_norm, ffn_gate, ffn_up, ffn_down):
    w = w_in[l]
    cuts = np.cumsum((0,) + IN_SIZES)
    col = lambda j: w[:, cuts[j]:cuts[j + 1]]
    zpad = lambda a: jnp.pad(a, ((0, 0), (0, 128 - a.shape[1])))
    row = lambda a: a[l][None, :].astype(f32)
    wih, wil = _split(jnp.concatenate([col(3), jnp.tile(col(4), (1, 128 // IDX_DIM)),
                                       zpad(jnp.concatenate([col(5), col(12)], axis=1))], axis=1))
    return dict(
        wm=jnp.concatenate([w[:, :768], w[:, cuts[6]:cuts[12]]], axis=1).astype(bf16), wih=wih, wil=wil,
        g_pre_mix=row(g_pre_mix), g_post_mix=row(g_post_mix), g_pre_ffn=row(g_pre_ffn), g_post_ffn=row(g_post_ffn),
        w_out=w_out[l].astype(bf16), wg=ffn_gate[l].astype(bf16), wu=ffn_up[l].astype(bf16), wd=ffn_down[l].astype(bf16),
        bw=jnp.pad(bconv_w[l], ((0, 8 - B_CONV), (0, 0))), cw=jnp.pad(cconv_w[l], ((0, 32 - C_CONV), (0, 0))),
        cb=row(cconv_b), lng=row(cln_g), lnb=row(cln_b),
        dw=jnp.pad(dconv_w[l], ((0, 8 - D_CONV), (0, 0))), db=row(dconv_b),
        dtb=_pad_lanes(dt_bias[l]), alog=_pad_lanes(a_log[l]),
        dskip=jnp.repeat(d_skip[l].astype(f32), HEAD_DIM)[None, :], dnorm=row(d_norm),
    )


def _finish(p, ya, yb, yc, yd, x, tm):
    return _outffn(ya, yb, yc, yd, x, p['w_out'], p['g_post_mix'], p['g_pre_ffn'], p['wg'], p['wu'], p['wd'],
                   p['g_post_ffn'], tm)


def kernel(x_prompt, x_sample, cache_k, cache_v, cache_kidx, page_table, state_bconv, state_cconv, state_dconv, state_ssm, w_in, w_out, g_pre_mix, g_post_mix, g_pre_ffn, g_post_ffn, bconv_w, cconv_w, cconv_b, cln_g, cln_b, dconv_w, dconv_b, dt_bias, a_log, d_skip, d_norm, ffn_gate, ffn_up, ffn_down):
    bsz, seq, _ = x_prompt.shape
    nb, t_dec, _ = x_sample.shape
    depth = w_in.shape[0]
    n_phys = cache_k.shape[1]
    n_pages = page_table.shape[1]
    past = n_pages * PAGE
    assert t_dec == 1 and seq % QB == 0 and seq % TC == 0 and past + t_dec > TOPK_MAX * 4
    assert nb % SSM_GROUP == 0 and nb % SCORE_GROUP == 0
    weights = (w_in, w_out, g_pre_mix, g_post_mix, g_pre_ffn, g_post_ffn, bconv_w, cconv_w, cconv_b, cln_g, cln_b,
               dconv_w, dconv_b, dt_bias, a_log, d_skip, d_norm, ffn_gate, ffn_up, ffn_down)
    pos_p = jnp.arange(seq, dtype=jnp.int32)
    pos_s = jnp.full((nb,), past, jnp.int32)
    tabs_p = _rope_tables(pos_p, HEAD_DIM) + _rope_tables(pos_p, IDX_DIM)
    tabs_s = _rope_tables(pos_s, HEAD_DIM) + _rope_tables(pos_s, IDX_DIM)
    ck = cache_k.transpose(0, 1, 3, 4, 2).reshape(depth * n_phys, 256, PAGE)
    cv = cache_v.transpose(0, 1, 3, 4, 2).reshape(depth * n_phys, 256, PAGE)
    cki = cache_kidx.transpose(0, 1, 3, 2).reshape(depth * n_phys, IDX_DIM, PAGE)
    pt = page_table.reshape(-1).astype(jnp.int32)
    lane_head = (jnp.arange(256) // HEAD_DIM)[None, None, :] == jnp.arange(8)[None, :, None]

    hp = x_prompt.reshape(bsz * seq, D_MODEL)
    hs = x_sample.reshape(nb, D_MODEL)
    outs_p, outs_s = [], []
    for l in range(depth):
        p = _layer_params(l, *weights)
        conv_prm = (p['bw'], p['cw'], p['cb'], p['lng'], p['lnb'], p['dw'], p['db'], p['dtb'])
        ssm_prm = (p['alog'], p['dskip'], p['dnorm'])

        q, k, kb, v, vt, qi, ki4, kcat, wi, dt, bgc, glu, zg, xbc = _inproj(
            hp, p['g_pre_mix'], p['wm'], p['wih'], p['wil'], tabs_p, 256)
        ya = _attn_prompt_t(q, qi, wi, kcat, kb, vt, bsz, seq)
        yb, yc, yd, nbp, ncp, ndp, ssm_p = _mix_prompt(bgc, glu, zg, xbc, dt, conv_prm + ssm_prm, bsz, seq)
        hp = _finish(p, ya, yb, yc, yd, hp, 512)
        outs_p.append((k.reshape(bsz, seq, N_HEADS, HEAD_DIM), v.reshape(bsz, seq, N_HEADS, HEAD_DIM),
                       ki4[:, :IDX_DIM].reshape(bsz, seq, IDX_DIM), nbp, ncp, ndp,
                       ssm_p.reshape(bsz, N_HEADS, HEAD_DIM, D_STATE)))

        q, k, kb, v, vt, qi, ki4, kcat, wi, dt, bgc, glu, zg, xbc = _inproj(
            hs, p['g_pre_mix'], p['wm'], p['wih'], p['wil'], tabs_s, nb)
        topk = min(TOPK_MAX, (past + t_dec) // 4)
        qi_s = qi.reshape(nb, N_IDX_HEADS, IDX_DIM)
        wi_t = jnp.broadcast_to((wi[:, :N_IDX_HEADS] * (IDX_DIM ** -0.5 * N_IDX_HEADS ** -0.5))[:, :, None],
                                (nb, N_IDX_HEADS, 128))
        sc = _dec_scores(pt, qi_s, wi_t, cki, n_pages, l * n_phys).reshape(nb, past)
        bias, bnew = _dec_select(sc, qi, ki4, wi, topk)
        qm = jnp.where(lane_head, q[:, None, :], 0.0)
        ya = _dec_attn(pt, qm, k.reshape(nb, 1, 256), v.reshape(nb, 1, 256), bias.reshape(nb, 1, past),
                       bnew.reshape(nb, 1, 128), ck, cv, n_pages, l * n_phys).reshape(nb, 256)
        yb, yc, act, dtv, nbs, ncs, nds = _dec_conv(
            bgc, glu, xbc, dt, state_bconv[l].reshape(nb, -1), state_cconv[l].reshape(nb, -1),
            state_dconv[l].reshape(nb, -1), conv_prm)
        yd, ssm_s = _dec_ssm(act.reshape(nb, 1, D_XBC), dtv.reshape(nb, 1, 128), zg.reshape(nb, 1, 256),
                             state_ssm[l].reshape(nb, 256, D_STATE), *ssm_prm)
        hs = _finish(p, ya, yb, yc, yd.reshape(nb, 256), hs, nb)
        outs_s.append((k.reshape(nb, 1, N_HEADS, HEAD_DIM), v.reshape(nb, 1, N_HEADS, HEAD_DIM),
                       ki4[:, :IDX_DIM].reshape(nb, 1, IDX_DIM), nbs.reshape(nb, 2, 256), ncs.reshape(nb, 30, 256),
                       nds.reshape(nb, 3, D_XBC), ssm_s.reshape(nb, N_HEADS, HEAD_DIM, D_STATE)))

    k_p, v_p, kidx_p, bconv_p, cconv_p, dconv_p, ssm_p = [jnp.stack(a) for a in zip(*outs_p)]
    k_s, v_s, kidx_s, bconv_s, cconv_s, dconv_s, ssm_s = [jnp.stack(a) for a in zip(*outs_s)]
    return (hp.reshape(bsz, seq, D_MODEL), hs.reshape(nb, t_dec, D_MODEL), k_p, v_p, kidx_p, k_s, v_s, kidx_s,
            bconv_p, bconv_s, cconv_p, cconv_s, dconv_p, dconv_s, ssm_p, ssm_s)
```

```python
import functools
import math

import jax
import jax.numpy as jnp
import numpy as np
from jax import lax
from jax.experimental import pallas as pl
from jax.experimental.pallas import tpu as pltpu

f32, bf16, i32 = jnp.float32, jnp.bfloat16, jnp.int32
HI = lax.Precision.HIGHEST

D_MODEL = 1024
PAGE = 128
GW = 256
HEAD_DIM = 64
N_HEADS = 4
N_IDX_HEADS = 8
IDX_DIM = 32
TOPK_MAX = 256
ROPE_THETA = 500000.0
C_CONV = 31
B_CONV = 3
D_CONV = 4
D_XBC = 768
D_STATE = 128
SSD_CHUNK = 128
D_FF = 2816
RMS_EPS = 1e-6
LN_EPS = 1e-5
IN_SIZES = (256, 256, 256, 256, 32, 8, 256, 256, 256, 512, 256, 768, 4)
INT_MIN = -(2 ** 31)
KEY_NEG_INF = INT_MIN + 0x7FFFFF
NEG = -1e30
VMEM_LIMIT = 56 * 1024 * 1024


def _dot_nt(a, b, prec=None):
    return lax.dot_general(a, b, (((1,), (1,)), ((), ())), precision=prec, preferred_element_type=f32)


def _dot(a, b, prec=None):
    return jnp.dot(a, b, precision=prec, preferred_element_type=f32)


def _cparams(sem):
    return pltpu.CompilerParams(dimension_semantics=sem, vmem_limit_bytes=VMEM_LIMIT)


def _rms(x, g):
    return x * lax.rsqrt(jnp.mean(x * x, axis=-1, keepdims=True) + RMS_EPS) * g


def _silu(x):
    return x * jax.nn.sigmoid(x)


def _key_to_float(key):
    return lax.bitcast_convert_type(jnp.where(key < 0, key ^ jnp.int32(0x7FFFFFFF), key), f32)


def _split_f32(x):
    c = x * (2.0 ** 16 + 1.0)
    hi = c - (c - x)
    return hi, x - hi


def _split(x):
    hi, lo = _split_f32(x)
    return hi.astype(bf16), lo.astype(bf16)


def _kth_largest(count_ge, shape, topk):
    def bit_step(bi, key):
        cand = key + lax.shift_left(jnp.int32(1), 31 - bi)
        ok = (cand <= KEY_NEG_INF) | (count_ge(_key_to_float(cand)) >= topk)
        return jnp.where(ok, cand, key)

    key = lax.fori_loop(0, 32, bit_step, jnp.full(shape, INT_MIN, i32))
    return key, _key_to_float(key)


def _refine_between_floats(count_ge, lo, key, rows, topk, steps=32):
    hi = _key_to_float(key + 1)

    def split(lo, hi):
        mid = lo + (hi - lo) * 0.5
        return mid, rows & (mid > lo) & (mid < hi)

    def any_open(lo, hi):
        return jnp.max(jnp.where(split(lo, hi)[1], 1.0, 0.0))

    def cond(c):
        return (c[2] > 0.0) & (c[3] < steps)

    def body(c):
        lo, hi, _, it = c
        mid, is_open = split(lo, hi)
        ge = count_ge(mid) >= topk
        lo = jnp.where(is_open & ge, mid, lo)
        hi = jnp.where(is_open & jnp.logical_not(ge), mid, hi)
        return lo, hi, any_open(lo, hi), it + 1

    return lax.while_loop(cond, body, (lo, hi, any_open(lo, hi), jnp.int32(0)))[0]


def _rope(v, c, sm, sp, half):
    outs = []
    for s in range(v.shape[1] // 128):
        xs = v[:, s * 128:(s + 1) * 128]
        outs.append(xs * c + pltpu.roll(xs, 128 - half, 1) * sm + pltpu.roll(xs, half, 1) * sp)
    return jnp.concatenate(outs, axis=1)


def _inproj_kernel(x_ref, g_ref, wm_ref, wih_ref, wil_ref, c64_ref, sm64_ref, sp64_ref, c32_ref, sm32_ref, sp32_ref,
                   q_ref, k_ref, kb_ref, v_ref, vt_ref, qi_ref, ki_ref, kcat_ref, wi_o_ref, dt_o_ref,
                   bgc_ref, glu_ref, zg_ref, xbc_ref):
    u = _rms(x_ref[...], g_ref[...])
    ub, ul = _split(u)

    def mm(c0, c1):
        return _dot(ub, wm_ref[:, c0:c1])

    def mm3(c0, c1):
        wh = wih_ref[:, c0:c1]
        return _dot(ub, wh) + _dot(ul, wh) + _dot(ub, wil_ref[:, c0:c1])

    c64, sm64, sp64 = c64_ref[...], sm64_ref[...], sp64_ref[...]
    q_ref[...] = _rope(mm(0, 256), c64, sm64, sp64, 8)
    k = _rope(mm(256, 512), c64, sm64, sp64, 8)
    k_ref[...] = k
    kb_ref[...] = k.astype(bf16)
    v = mm(512, 768)
    v_ref[...] = v
    vt_ref[...] = v.T.astype(bf16)
    bgc_ref[...] = mm(768, 1536)
    glu_ref[...] = mm(1536, 2048)
    zg_ref[...] = mm(2048, 2304)
    xbc_ref[...] = mm(2304, 3072)
    c32, sm32, sp32 = c32_ref[...], sm32_ref[...], sp32_ref[...]
    qi_ref[...] = _rope(mm3(0, 256), c32, sm32, sp32, 4)
    ki4 = _rope(mm3(256, 384), c32, sm32, sp32, 4)
    ki_ref[...] = ki4
    kh, kl = _split(ki4)
    lane = lax.broadcasted_iota(i32, ki4.shape, 1)
    kcat_ref[...] = jnp.where((lane >= IDX_DIM) & (lane < 2 * IDX_DIM), kl, kh)
    small = mm3(384, 512)
    wi_o_ref[...] = small
    dt_o_ref[...] = pltpu.roll(small, 128 - N_IDX_HEADS, 1)


def _inproj(x, g, wm, wih, wil, tabs, tm):
    t = x.shape[0]
    nt = t // tm
    ntab = tabs[0].shape[0] // tm
    row = lambda w: pl.BlockSpec((tm, w), lambda i: (i, 0))
    full = lambda a: pl.BlockSpec(a.shape, lambda i: (0,) * a.ndim)
    tab = pl.BlockSpec((tm, 128), lambda i: (i % ntab, 0))
    widths = (256, 256, 256, 256, 256, 256, 128, 128, 128, 128, 768, 512, 256, 768)
    dtypes = (f32, f32, bf16, f32, bf16, f32, f32, bf16, f32, f32, f32, f32, f32, f32)
    out_specs = [row(w) for w in widths]
    out_shape = [jax.ShapeDtypeStruct((t, w), d) for w, d in zip(widths, dtypes)]
    out_specs[4] = pl.BlockSpec((256, tm), lambda i: (0, i))
    out_shape[4] = jax.ShapeDtypeStruct((256, t), bf16)
    return pl.pallas_call(
        _inproj_kernel,
        grid=(nt,),
        in_specs=[row(D_MODEL), full(g), full(wm), full(wih), full(wil)] + [tab] * 6,
        out_specs=out_specs,
        out_shape=out_shape,
        compiler_params=_cparams(("arbitrary",)),
        name="inproj",
    )(x, g, wm, wih, wil, *tabs)


QB = 256
SUB = 128


def _attn_t_kernel(q_ref, qi_ref, wi_ref, kcat_ref, k_ref, vt_ref, o_ref, sc_ref, cat_ref, qm_ref, acc_ref, *, topk):
    i = pl.program_id(1)
    nkb = i + 1
    n_sub = QB // SUB
    hi, lo = _split_f32(qi_ref[...])
    hi_t, lo_t = hi.T, lo.T
    for h in range(N_IDX_HEADS):
        rs = slice(h * IDX_DIM, (h + 1) * IDX_DIM)
        cat_ref[h] = jnp.concatenate([hi_t[rs], hi_t[rs], lo_t[rs], jnp.zeros((IDX_DIM, QB), f32)], axis=0).astype(bf16)
    q_t = (q_ref[...] * (HEAD_DIM ** -0.5)).T
    row = lax.broadcasted_iota(i32, (256, QB), 0)
    for h in range(N_HEADS):
        qm_ref[:, h * QB:(h + 1) * QB] = jnp.where(row // HEAD_DIM == h, q_t, 0.0).astype(bf16)
    w8 =(wi_ref[...] * (IDX_DIM ** -0.5 * N_IDX_HEADS ** -0.5)).T[0:N_IDX_HEADS]
    qpos = i * QB + lax.broadcasted_iota(i32, (1, QB), 1)
    kio = lax.broadcasted_iota(i32, (SUB, 1), 0)

    def tiles(kb):
        return [pl.ds(pl.multiple_of(kb * QB + j * SUB, SUB), SUB) for j in range(n_sub)]

    def score_block(kb, carry):
        for j, sl in enumerate(tiles(kb)):
            kc = kcat_ref[sl, :]
            acc = jnp.zeros((SUB, QB), f32)
            for h in range(N_IDX_HEADS):
                acc = acc + jnp.maximum(_dot(kc, cat_ref[h]), 0.0) * w8[h:h + 1, :]
            sc_ref[sl, :] = jnp.where(kb * QB + j * SUB + kio <= qpos, acc, -jnp.inf)
        return carry

    lax.fori_loop(0, nkb, score_block, 0)

    def count(pred):
        def body(kb, cnt):
            for sl in tiles(kb):
                cnt = cnt + jnp.where(pred(sc_ref[sl, :]), 1.0, 0.0).reshape(SUB // 8, 8, QB).sum(axis=0)
            return cnt

        def body2(kb2, cnt):
            return body(2 * kb2 + 1, body(2 * kb2, cnt))

        cnt = lax.fori_loop(0, nkb // 2, body2, jnp.zeros((8, QB), f32))
        cnt = lax.fori_loop(2 * (nkb // 2), nkb, body, cnt)
        return jnp.sum(cnt, axis=0, keepdims=True)

    def count_ge(t):
        return count(lambda s: s >= t)

    thr_key, thr = _kth_largest(count_ge, (1, QB), topk)
    real = thr > -jnp.inf
    tie = real & (count_ge(thr) > topk)
    any_tie = jnp.max(jnp.where(tie, 1.0, 0.0)) > 0.0

    @pl.when(jnp.logical_not(any_tie))
    def _():
        thr_fin = jnp.maximum(thr, jnp.finfo(f32).min)

        def mask_block(kb, carry):
            for sl in tiles(kb):
                sc_ref[sl, :] = jnp.where(sc_ref[sl, :] >= thr_fin, 0.0, NEG)
            return carry

        lax.fori_loop(0, nkb, mask_block, 0)

    @pl.when(any_tie)
    def _():
        thr2 = _refine_between_floats(count_ge, thr, thr_key, tie, topk)
        need = topk - count(lambda s: s > thr2)
        realf = jnp.where(real, 1.0, 0.0)
        r = lax.broadcasted_iota(i32, (SUB, SUB), 0)
        c = lax.broadcasted_iota(i32, (SUB, SUB), 1)
        tri = jnp.where(c <= r, 1.0, 0.0).astype(bf16)

        def tie_blocks(kb0, carry, n):
            sls = [sl for j in range(n) for sl in tiles(kb0 + j)]
            eqbs = [jnp.where(sc_ref[sl, :] == thr2, realf, 0.0).astype(bf16) for sl in sls]
            locs = [_dot(tri, e) for e in eqbs]
            for sl, eqb, loc in zip(sls, eqbs, locs):
                cum = loc + carry
                keep = jnp.where(sc_ref[sl, :] > thr2, 1.0, jnp.where(cum <= need, eqb.astype(f32), 0.0))
                sc_ref[sl, :] = jnp.where(keep > 0.0, 0.0, NEG)
                carry = cum[SUB - 1:SUB, :]
            return carry

        carry = lax.fori_loop(0, nkb // 2, lambda kb2, c: tie_blocks(2 * kb2, c, 2), jnp.zeros((1, QB), f32))
        lax.fori_loop(2 * (nkb // 2), nkb, lambda kb, c: tie_blocks(kb, c, 1), carry)

    heads = [slice(h * HEAD_DIM, (h + 1) * HEAD_DIM) for h in range(N_HEADS)]

    def attn_blocks(kb0, carry, n):
        ms, ls = list(carry[0]), list(carry[1])
        sls = [pl.ds(pl.multiple_of((kb0 + j) * QB, QB), QB) for j in range(n)]
        s_alls = [_dot(k_ref[sl, :], qm_ref[...]) for sl in sls]
        alphas, pvs = [], []
        for j, sl in enumerate(sls):
            mask = sc_ref[sl, :]
            al, ps = [], []
            for h in range(N_HEADS):
                s = s_alls[j][:, h * QB:(h + 1) * QB] + mask
                m_new = jnp.maximum(ms[h], jnp.max(s, axis=0, keepdims=True))
                al.append(jnp.exp(ms[h] - m_new))
                p = jnp.exp(s - m_new)
                ls[h] = al[h] * ls[h] + jnp.sum(p, axis=0, keepdims=True)
                ps.append(p.astype(bf16))
                ms[h] = m_new
            alphas.append(al)
            pvs.append([_dot(vt_ref[heads[h], sl], ps[h]) for h in range(N_HEADS)])
        for j in range(n):
            for h in range(N_HEADS):
                acc_ref[heads[h], :] = alphas[j][h] * acc_ref[heads[h], :] + pvs[j][h]
        return tuple(ms), tuple(ls)

    acc_ref[...] = jnp.zeros_like(acc_ref)
    m0 = tuple(jnp.full((1, QB), NEG, f32) for _ in range(N_HEADS))
    l0 = tuple(jnp.zeros((1, QB), f32) for _ in range(N_HEADS))
    carry = lax.fori_loop(0, nkb // 2, lambda kb2, c: attn_blocks(2 * kb2, c, 2), (m0, l0))
    _, ls = lax.fori_loop(2 * (nkb // 2), nkb, lambda kb, c: attn_blocks(kb, c, 1), carry)
    out_t = jnp.concatenate([acc_ref[h * HEAD_DIM:(h + 1) * HEAD_DIM, :] / ls[h] for h in range(N_HEADS)], axis=0)
    o_ref[...] = out_t.T


def _attn_prompt_t(q, qi, wi, kcat, kb, vt, bsz, seq):
    nq = seq // QB
    topk = min(TOPK_MAX, seq // 4)
    qrow = lambda w: pl.BlockSpec((QB, w), lambda b, i: (b * nq + i, 0))
    seqblk = lambda w: pl.BlockSpec((seq, w), lambda b, i: (b, 0))
    return pl.pallas_call(
        functools.partial(_attn_t_kernel, topk=float(topk)),
        grid=(bsz, nq),
        in_specs=[qrow(256), qrow(256), qrow(128), seqblk(128), seqblk(256),
                  pl.BlockSpec((256, seq), lambda b, i: (0, b))],
        out_specs=qrow(256),
        out_shape=jax.ShapeDtypeStruct((bsz * seq, 256), f32),
        scratch_shapes=[pltpu.VMEM((seq, QB), f32), pltpu.VMEM((N_IDX_HEADS, 128, QB), bf16),
                        pltpu.VMEM((256, N_HEADS * QB), bf16), pltpu.VMEM((256, QB), f32)],
        compiler_params=_cparams(("arbitrary", "arbitrary")),
        name="attn_prompt",
    )(q, qi, wi, kcat, kb, vt)


TC = 256
HALO_B, HALO_C, HALO_D = 8, 32, 8


def _per_head_lanes(a):
    lane = lax.broadcasted_iota(i32, (a.shape[0], N_HEADS * HEAD_DIM), 1)
    out = jnp.broadcast_to(a[:, N_HEADS - 1:N_HEADS], lane.shape)
    for h in range(N_HEADS - 2, -1, -1):
        out = jnp.where(lane < (h + 1) * HEAD_DIM, a[:, h:h + 1], out)
    return out


def _cumsum_rows(x):
    n = x.shape[0]
    r = lax.broadcasted_iota(i32, (n, n), 0)
    c = lax.broadcasted_iota(i32, (n, n), 1)
    tril = jnp.where(c <= r, 1.0, 0.0).astype(bf16)
    p1, rest = _split_f32(x)
    p2, p3 = _split_f32(rest)
    return _dot(tril, p1.astype(bf16)) + _dot(tril, p2.astype(bf16)) + _dot(tril, p3.astype(bf16))


def _ssd_chunk(xs, bm, cm, dtv, a_row, hcat_ref):
    r = lax.broadcasted_iota(i32, (128, 128), 0)
    c = lax.broadcasted_iota(i32, (128, 128), 1)
    causal = c <= r
    lane = lax.broadcasted_iota(i32, (128, 256), 1)
    da = dtv * a_row
    cs = _cumsum_rows(da)
    cs_t = cs.T
    dt_x = _per_head_lanes(dtv)
    ecs_x = _per_head_lanes(jnp.exp(cs))
    cs_last = cs[127:128, :]
    wend_x = _per_head_lanes(jnp.exp(cs_last - cs) * dtv)
    xdt = (xs * dt_x).astype(bf16)
    bmb, cmb = bm.astype(bf16), cm.astype(bf16)
    hb = hcat_ref[...].astype(bf16)
    y = jnp.zeros((128, 256), f32)
    ystate = []
    for g in range(2):
        cg = cmb[:, g * 128:(g + 1) * 128]
        cb = _dot_nt(cg, bmb[:, g * 128:(g + 1) * 128])
        ystate.append(_dot_nt(cg, hb))
        for h in (2 * g, 2 * g + 1):
            seg = cs[:, h:h + 1] - cs_t[h:h + 1, :]
            dec = jnp.where(causal, jnp.exp(jnp.where(causal, seg, 0.0)), 0.0)
            yh = _dot((cb * dec).astype(bf16), xdt)
            y = jnp.where(lane // 64 == h, yh, y)
    y = y + jnp.where(lane < 128, ystate[0], ystate[1]) * ecs_x
    xw_t = (xs * wend_x).T.astype(bf16)
    upd = jnp.concatenate([_dot(xw_t[0:128], bmb[:, 0:128]), _dot(xw_t[128:256], bmb[:, 128:256])], axis=0)
    elast = jnp.exp(cs_last)
    dcol = jnp.concatenate([jnp.broadcast_to(elast[:, h:h + 1], (HEAD_DIM, D_STATE)) for h in range(N_HEADS)], axis=0)
    hcat_ref[...] = hcat_ref[...] * dcol + upd
    return y


def _gated_rms(ys, xs, zg, dskip, dnorm):
    yg = (ys + dskip * xs) * _silu(zg)
    return yg * lax.rsqrt(jnp.mean(yg * yg, axis=-1, keepdims=True) + RMS_EPS) * dnorm


def _layer_norm_silu(x, g, b):
    mu = jnp.mean(x, axis=-1, keepdims=True)
    var = jnp.mean(jnp.square(x - mu), axis=-1, keepdims=True)
    return _silu((x - mu) * lax.rsqrt(var + LN_EPS) * g + b)


def _mix_kernel(bgc_ref, glu_ref, zg_ref, xbc_ref, dt_ref,
                bw_ref, cw_ref, cb_ref, lng_ref, lnb_ref, dw_ref, db_ref, dtb_ref, alog_ref, dskip_ref, dnorm_ref,
                yb_ref, yc_ref, yd_ref, nb_ref, nc_ref, nd_ref, ssm_ref,
                eb_ref, ec_ref, ed_ref, hcat_ref, pc_ref):
    i = pl.program_id(1)

    @pl.when(i == 0)
    def _():
        eb_ref[0:HALO_B, :] = jnp.zeros((HALO_B, 256), f32)
        ec_ref[0:HALO_C, :] = jnp.zeros((HALO_C, 256), f32)
        ed_ref[0:HALO_D, :] = jnp.zeros((HALO_D, D_XBC), f32)
        hcat_ref[...] = jnp.zeros_like(hcat_ref)

    bgc = bgc_ref[...]
    eb_ref[HALO_B:HALO_B + TC, :] = bgc[:, 512:768] * bgc[:, 0:256]
    conv = jnp.zeros((TC, 256), f32)
    for k in range(B_CONV):
        conv = conv + bw_ref[k:k + 1, :] * eb_ref[pl.ds(HALO_B - (B_CONV - 1) + k, TC), :]
    yb_ref[...] = bgc[:, 256:512] * conv
    nb_ref[0] = eb_ref[HALO_B + TC - (B_CONV - 1):HALO_B + TC, :]
    eb_ref[0:HALO_B, :] = eb_ref[TC:TC + HALO_B, :]

    glu = glu_ref[...]
    ec_ref[HALO_C:HALO_C + TC, :] = glu[:, 0:256] * jax.nn.sigmoid(glu[:, 256:512])
    off = HALO_C - (C_CONV - 1)
    conv = jnp.zeros((TC, 256), f32)
    for r in range(8):
        taps = [j for j in range(r, HALO_C + 1, 8) if off <= j < off + C_CONV]
        rows = TC if r == 0 else TC + 8
        part = jnp.zeros((rows, 256), f32)
        for j in taps:
            part = part + cw_ref[j - off:j - off + 1, :] * ec_ref[j - r:j - r + rows, :]
        if r == 0:
            conv = conv + part
        else:
            pc_ref[...] = part
            conv = conv + pc_ref[r:r + TC, :]
    yc_ref[...] = _layer_norm_silu(conv + cb_ref[...], lng_ref[...], lnb_ref[...])
    nc_ref[0] = ec_ref[HALO_C + TC - (C_CONV - 1):HALO_C + TC, :]
    ec_ref[0:HALO_C, :] = ec_ref[TC:TC + HALO_C, :]

    ed_ref[HALO_D:HALO_D + TC, :] = xbc_ref[...]
    conv = jnp.zeros((TC, D_XBC), f32)
    for k in range(D_CONV):
        conv = conv + dw_ref[k:k + 1, :] * ed_ref[pl.ds(HALO_D - (D_CONV - 1) + k, TC), :]
    act = _silu(conv + db_ref[...])
    nd_ref[0] = ed_ref[HALO_D + TC - (D_CONV - 1):HALO_D + TC, :]
    ed_ref[0:HALO_D, :] = ed_ref[TC:TC + HALO_D, :]
    l128 = lax.broadcasted_iota(i32, (1, 128), 1)
    a_row = jnp.where(l128 < N_HEADS, -jnp.exp(alog_ref[...]), 0.0)
    dtv = jax.nn.softplus(dt_ref[...] + dtb_ref[...])
    zg = zg_ref[...]
    for j in range(TC // SSD_CHUNK):
        rs = slice(j * SSD_CHUNK, (j + 1) * SSD_CHUNK)
        xs = act[rs, 0:256]
        y = _ssd_chunk(xs, act[rs, 256:512], act[rs, 512:768], dtv[rs], a_row, hcat_ref)
        yd_ref[rs, :] = _gated_rms(y, xs, zg[rs], dskip_ref[...], dnorm_ref[...])
    ssm_ref[0] = hcat_ref[...]


def _mix_prompt(bgc, glu, zg, xbc, dt, prm, bsz, seq):
    nt = seq // TC
    row = lambda w: pl.BlockSpec((TC, w), lambda b, i: (b * nt + i, 0))
    full = lambda a: pl.BlockSpec(a.shape, lambda b, i: (0,) * a.ndim)
    st = lambda r, w: pl.BlockSpec((1, r, w), lambda b, i: (b, 0, 0))
    t = bsz * seq
    return pl.pallas_call(
        _mix_kernel,
        grid=(bsz, nt),
        in_specs=[row(768), row(512), row(256), row(768), row(128)] + [full(a) for a in prm],
        out_specs=[row(256), row(256), row(256), st(2, 256), st(30, 256), st(3, D_XBC), st(256, 128)],
        out_shape=[jax.ShapeDtypeStruct((t, 256), f32)] * 3 + [
            jax.ShapeDtypeStruct((bsz, 2, 256), f32), jax.ShapeDtypeStruct((bsz, 30, 256), f32),
            jax.ShapeDtypeStruct((bsz, 3, D_XBC), f32), jax.ShapeDtypeStruct((bsz, 256, 128), f32)],
        scratch_shapes=[pltpu.VMEM((HALO_B + TC, 256), f32), pltpu.VMEM((HALO_C + TC, 256), f32),
                        pltpu.VMEM((HALO_D + TC, D_XBC), f32), pltpu.VMEM((256, 128), f32),
                        pltpu.VMEM((TC + 8, 256), f32)],
        compiler_params=_cparams(("arbitrary", "arbitrary")),
        name="mix_prompt",
    )(bgc, glu, zg, xbc, dt, *prm)


FF_CHUNK = 1408


def _outffn_kernel(ya_ref, yb_ref, yc_ref, yd_ref, x_ref, wo_ref, gpost_ref, gpre_ref, wg_ref, wu_ref, wd_ref, g_ref,
                   o_ref, acc_ref, x1_ref, hf_ref):
    j = pl.program_id(1)

    @pl.when(j == 0)
    def _():
        mix = _dot(ya_ref[...].astype(bf16), wo_ref[0:256, :])
        mix = mix + _dot(yb_ref[...].astype(bf16), wo_ref[256:512, :])
        mix = mix + _dot(yc_ref[...].astype(bf16), wo_ref[512:768, :])
        mix = mix + _dot(yd_ref[...].astype(bf16), wo_ref[768:1024, :])
        x1 = x_ref[...] + _rms(mix, gpost_ref[...])
        x1_ref[...] = x1
        hf_ref[...] = _rms(x1, gpre_ref[...]).astype(bf16)

    hf = hf_ref[...]
    a = _silu(_dot(hf, wg_ref[...])) * _dot(hf, wu_ref[...])
    part = _dot(a.astype(bf16), wd_ref[...])

    @pl.when(j == 0)
    def _():
        acc_ref[...] = part

    @pl.when(j > 0)
    def _():
        acc_ref[...] = acc_ref[...] + part

    @pl.when(j == pl.num_programs(1) - 1)
    def _():
        o_ref[...] = x1_ref[...] + _rms(acc_ref[...], g_ref[...])


def _outffn(ya, yb, yc, yd, x, wo, gpost, gpre, wg, wu, wd, g, tm):
    t = x.shape[0]
    nj = D_FF // FF_CHUNK
    row = lambda w_: pl.BlockSpec((tm, w_), lambda i, j: (i, 0))
    full = lambda a: pl.BlockSpec(a.shape, lambda i, j: (0,) * a.ndim)
    return pl.pallas_call(
        _outffn_kernel,
        grid=(t // tm, nj),
        in_specs=[row(256)] * 4 + [row(D_MODEL), full(wo), full(gpost), full(gpre),
                                   pl.BlockSpec((D_MODEL, FF_CHUNK), lambda i, j: (0, j)),
                                   pl.BlockSpec((D_MODEL, FF_CHUNK), lambda i, j: (0, j)),
                                   pl.BlockSpec((FF_CHUNK, D_MODEL), lambda i, j: (j, 0)), full(g)],
        out_specs=row(D_MODEL),
        out_shape=jax.ShapeDtypeStruct((t, D_MODEL), f32),
        scratch_shapes=[pltpu.VMEM((tm, D_MODEL), f32), pltpu.VMEM((tm, D_MODEL), f32),
                        pltpu.VMEM((tm, D_MODEL), bf16)],
        compiler_params=_cparams(("arbitrary", "arbitrary")),
        name="outffn",
    )(ya, yb, yc, yd, x, wo, gpost, gpre, wg, wu, wd, g)


def _page_copies(pt_ref, src_hbm, dst, sem, b, n_pages, base):
    return [pltpu.make_async_copy(src_hbm.at[base + pt_ref[b * n_pages + p]],
                                  dst.at[:, pl.ds(p * PAGE, PAGE)], sem) for p in range(n_pages)]


SCORE_GROUP = 8


def _dec_score_kernel(pt_ref, qi_ref, wi_ref, cki_hbm, o_ref, buf, sem, *, n_pages, base):
    g = pl.program_id(0)
    ng = pl.num_programs(0)

    def copies(gg, slot):
        return [c for j in range(SCORE_GROUP)
                for c in _page_copies(pt_ref, cki_hbm, buf.at[slot, j], sem.at[slot], gg * SCORE_GROUP + j, n_pages, base)]

    @pl.when(g == 0)
    def _():
        for c in copies(0, 0):
            c.start()

    @pl.when(g + 1 < ng)
    def _():
        for c in copies(g + 1, (g + 1) % 2):
            c.start()

    slot = g % 2
    for c in copies(g, slot):
        c.wait()
    for j in range(SCORE_GROUP):
        b = g * SCORE_GROUP + j
        s = _dot(qi_ref[b], buf[slot, j], HI)
        w = jnp.tile(wi_ref[b], (1, n_pages))
        o_ref[j] = jnp.sum(jnp.maximum(s, 0.0) * w, axis=0, keepdims=True)


def _dec_scores(pt, qi_s, wi_t, cki, n_pages, base):
    nb = qi_s.shape[0]
    past = n_pages * PAGE
    gs = pltpu.PrefetchScalarGridSpec(
        num_scalar_prefetch=1, grid=(nb // SCORE_GROUP,),
        in_specs=[pl.BlockSpec(qi_s.shape, lambda b, pt_: (0, 0, 0)), pl.BlockSpec(wi_t.shape, lambda b, pt_: (0, 0, 0)),
                  pl.BlockSpec(memory_space=pl.ANY)],
        out_specs=pl.BlockSpec((SCORE_GROUP, 1, past), lambda b, pt_: (b, 0, 0)),
        scratch_shapes=[pltpu.VMEM((2, SCORE_GROUP, IDX_DIM, past), f32), pltpu.SemaphoreType.DMA((2,))])
    return pl.pallas_call(
        functools.partial(_dec_score_kernel, n_pages=n_pages, base=base),
        grid_spec=gs, out_shape=jax.ShapeDtypeStruct((nb, 1, past), f32),
        compiler_params=_cparams(("arbitrary",)), name="dec_scores",
    )(pt, qi_s, wi_t, cki)


def _dec_select_kernel(sc_ref, qi_ref, ki_ref, wi_ref, bias_ref, bnew_ref, *, topk):
    nb, past = sc_ref.shape
    wi = wi_ref[...] * (IDX_DIM ** -0.5 * N_IDX_HEADS ** -0.5)
    gj = lax.broadcasted_iota(i32, (256, 128), 0)
    gh = lax.broadcasted_iota(i32, (256, 128), 1)
    seg = jnp.where(gj // IDX_DIM == gh, 1.0, 0.0)
    ki = jnp.concatenate([ki_ref[...], ki_ref[...]], axis=1)
    s_new = _dot(qi_ref[...] * ki, seg, HI)
    sc_new = jnp.broadcast_to(jnp.sum(jnp.maximum(s_new, 0.0) * wi, axis=-1, keepdims=True), (nb, 128))
    sc = sc_ref[...]
    ones = jnp.ones((past, 128), bf16)
    wide = lambda t: jnp.concatenate([t] * (past // 128), axis=1)

    def count(pred_past, pred_new):
        return _dot(jnp.where(pred_past, 1.0, 0.0).astype(bf16), ones) + jnp.where(pred_new, 1.0, 0.0)

    def count_ge(t):
        return count(sc >= wide(t), sc_new >= t)

    thr_key, thr = _kth_largest(count_ge, (nb, 128), topk)
    bias_ref[...] = jnp.where(sc >= wide(thr), 0.0, NEG)
    bnew_ref[...] = jnp.where(sc_new >= thr, 0.0, NEG)
    tie = count_ge(thr) > topk

    @pl.when(jnp.max(jnp.where(tie, 1.0, 0.0)) > 0.0)
    def _():
        thr2 = _refine_between_floats(count_ge, thr, thr_key, tie, topk)
        need = topk - count(sc > wide(thr2), sc_new > thr2)
        eqb = jnp.where(sc == wide(thr2), 1.0, 0.0).astype(bf16)
        r = lax.broadcasted_iota(i32, (PAGE, PAGE), 0)
        c = lax.broadcasted_iota(i32, (PAGE, PAGE), 1)
        tri = jnp.where(r <= c, 1.0, 0.0).astype(bf16)
        carry = jnp.zeros((nb, 128), f32)
        for p in range(past // PAGE):
            sl = slice(p * PAGE, (p + 1) * PAGE)
            cum = _dot(eqb[:, sl], tri) + carry
            keep = jnp.where(sc[:, sl] > thr2, 1.0, jnp.where(cum <= need, eqb[:, sl].astype(f32), 0.0))
            bias_ref[:, sl] = jnp.where(keep > 0.0, 0.0, NEG)
            carry = carry + _dot(eqb[:, sl], ones[0:PAGE])
        keep_new = jnp.where(sc_new > thr2, 1.0, jnp.where((sc_new == thr2) & (carry + 1.0 <= need), 1.0, 0.0))
        bnew_ref[...] = jnp.where(keep_new > 0.0, 0.0, NEG)


def _dec_select(sc, qi, ki4, wi, topk):
    nb, past = sc.shape
    return pl.pallas_call(
        functools.partial(_dec_select_kernel, topk=float(topk)),
        out_shape=[jax.ShapeDtypeStruct((nb, past), f32), jax.ShapeDtypeStruct((nb, 128), f32)],
        compiler_params=pltpu.CompilerParams(vmem_limit_bytes=VMEM_LIMIT), name="dec_select",
    )(sc, qi, ki4, wi)


def _dec_attn_kernel(pt_ref, qm_ref, kn_ref, vn_ref, bias_ref, bnew_ref, ck_hbm, cv_hbm, o_ref,
                     kbuf, vbuf, sem, *, n_pages, base):
    b = pl.program_id(0)
    nb = pl.num_programs(0)

    def copies(bb, slot):
        return (_page_copies(pt_ref, ck_hbm, kbuf.at[slot], sem.at[0, slot], bb, n_pages, base)
                + _page_copies(pt_ref, cv_hbm, vbuf.at[slot], sem.at[1, slot], bb, n_pages, base))

    @pl.when(b == 0)
    def _():
        for c in copies(0, 0):
            c.start()

    @pl.when(b + 1 < nb)
    def _():
        for c in copies(b + 1, (b + 1) % 2):
            c.start()

    slot = b % 2
    for c in copies(b, slot):
        c.wait()
    qm = qm_ref[0] * (HEAD_DIM ** -0.5)
    s = _dot(qm.astype(bf16), kbuf[slot].astype(bf16)) + bias_ref[0]
    s_new = jnp.sum(qm * kn_ref[0], axis=-1, keepdims=True) + bnew_ref[0][:, 0:1]
    m = jnp.maximum(jnp.max(s, axis=-1, keepdims=True), s_new)
    p = jnp.exp(s - m)
    p_new = jnp.exp(s_new - m)
    den = jnp.sum(p, axis=-1, keepdims=True) + p_new
    out8 = (_dot_nt(p.astype(bf16), vbuf[slot].astype(bf16)) + p_new * vn_ref[0]) / den
    row = lax.broadcasted_iota(i32, (8, 256), 0)
    lane = lax.broadcasted_iota(i32, (8, 256), 1)
    o_ref[0] = jnp.sum(jnp.where(lane // HEAD_DIM == row, out8, 0.0), axis=0, keepdims=True)


def _dec_attn(pt, qm, kn, vn, bias, bnew, ck, cv, n_pages, base):
    nb = qm.shape[0]
    past = n_pages * PAGE
    per = lambda r, w: pl.BlockSpec((1, r, w), lambda b, pt_: (b, 0, 0))
    gs = pltpu.PrefetchScalarGridSpec(
        num_scalar_prefetch=1, grid=(nb,),
        in_specs=[per(8, 256), per(1, 256), per(1, 256), per(1, past), per(1, 128),
                  pl.BlockSpec(memory_space=pl.ANY), pl.BlockSpec(memory_space=pl.ANY)],
        out_specs=per(1, 256),
        scratch_shapes=[pltpu.VMEM((2, 256, past), f32), pltpu.VMEM((2, 256, past), f32),
                        pltpu.SemaphoreType.DMA((2, 2))])
    return pl.pallas_call(
        functools.partial(_dec_attn_kernel, n_pages=n_pages, base=base),
        grid_spec=gs, out_shape=jax.ShapeDtypeStruct((nb, 1, 256), f32),
        compiler_params=_cparams(("arbitrary",)), name="dec_attn",
    )(pt, qm, kn, vn, bias, bnew, ck, cv)


def _dec_conv_kernel(bgc_ref, glu_ref, xbc_ref, dt_ref, sb_ref, sc_ref, sd_ref,
                     bw_ref, cw_ref, cb_ref, lng_ref, lnb_ref, dw_ref, db_ref, dtb_ref,
                     yb_ref, yc_ref, act_ref, dtv_ref, nb_ref, nc_ref, nd_ref):
    bgc = bgc_ref[...]
    ub = bgc[:, 512:768] * bgc[:, 0:256]
    sb = sb_ref[...]
    conv = bw_ref[0:1, :] * sb[:, 0:256] + bw_ref[1:2, :] * sb[:, 256:512] + bw_ref[2:3, :] * ub
    yb_ref[...] = bgc[:, 256:512] * conv
    nb_ref[...] = jnp.concatenate([sb[:, 256:512], ub], axis=1)

    glu = glu_ref[...]
    uc = glu[:, 0:256] * jax.nn.sigmoid(glu[:, 256:512])
    conv = cw_ref[C_CONV - 1:C_CONV, :] * uc
    for k in range(C_CONV - 1):
        conv = conv + cw_ref[k:k + 1, :] * sc_ref[:, k * 256:(k + 1) * 256]
    yc_ref[...] = _layer_norm_silu(conv + cb_ref[...], lng_ref[...], lnb_ref[...])
    nc_ref[:, 0:(C_CONV - 2) * 256] = sc_ref[:, 256:(C_CONV - 1) * 256]
    nc_ref[:, (C_CONV - 2) * 256:(C_CONV - 1) * 256] = uc

    xbc = xbc_ref[...]
    conv = dw_ref[D_CONV - 1:D_CONV, :] * xbc
    for k in range(D_CONV - 1):
        conv = conv + dw_ref[k:k + 1, :] * sd_ref[:, k * D_XBC:(k + 1) * D_XBC]
    act_ref[...] = _silu(conv + db_ref[...])
    nd_ref[:, 0:(D_CONV - 2) * D_XBC] = sd_ref[:, D_XBC:(D_CONV - 1) * D_XBC]
    nd_ref[:, (D_CONV - 2) * D_XBC:(D_CONV - 1) * D_XBC] = xbc
    dtv_ref[...] = jax.nn.softplus(dt_ref[...] + dtb_ref[...])


def _dec_conv(bgc, glu, xbc, dt, sb, sc, sd, prm):
    nb = bgc.shape[0]
    shp = lambda w: jax.ShapeDtypeStruct((nb, w), f32)
    return pl.pallas_call(
        _dec_conv_kernel,
        out_shape=[shp(256), shp(256), shp(D_XBC), shp(128), shp(2 * 256), shp(30 * 256), shp(3 * D_XBC)],
        compiler_params=pltpu.CompilerParams(vmem_limit_bytes=VMEM_LIMIT), name="dec_conv",
    )(bgc, glu, xbc, dt, sb, sc, sd, *prm)


SSM_GROUP = 8


def _dec_ssm_kernel(act_ref, dtv_ref, zg_ref, h_ref, alog_ref, dskip_ref, dnorm_ref, yd_ref, hn_ref):
    l128 = lax.broadcasted_iota(i32, (1, 128), 1)
    a_row = jnp.where(l128 < N_HEADS, -jnp.exp(alog_ref[...]), 0.0)
    r = lax.broadcasted_iota(i32, (128, 256), 0)
    lane = lax.broadcasted_iota(i32, (128, 256), 1)
    r8 = lax.broadcasted_iota(i32, (8, 128), 0)
    l256 = lax.broadcasted_iota(i32, (1, 256), 1)
    zrow = jnp.zeros((1, 128), f32)
    group = range(act_ref.shape[0])
    xs_, cm_, lt_, rmat_ = [], [], [], []
    for g in group:
        act = act_ref[g]
        xs, bm, cm = act[:, 0:256], act[:, 256:512], act[:, 512:768]
        dtv = dtv_ref[g]
        dec_x = _per_head_lanes(jnp.exp(dtv * a_row))
        xdt = xs * _per_head_lanes(dtv)
        lrows = jnp.where((r == 0) & (lane < 128), xdt, 0.0) + jnp.where((r == 1) & (lane >= 128), xdt, 0.0)
        lt_.append((lrows + jnp.where(r == 2, dec_x, 0.0)).T)
        b0 = jnp.concatenate([bm[:, 0:128], zrow], axis=1)
        b1 = jnp.concatenate([bm[:, 128:256], zrow], axis=1)
        rmat_.append(jnp.where(r == 0, b0, 0.0) + jnp.where(r == 1, b1, 0.0)
                     + jnp.where((r == 2) & (lane >= 128), 1.0, 0.0))
        xs_.append(xs)
        cm_.append(cm)
    res_ = [_dot(lt_[g], rmat_[g], HI) for g in group]
    hn_ = [h_ref[g] * res_[g][:, 128:256] + res_[g][:, 0:128] for g in group]
    for g in group:
        hn_ref[g] = hn_[g]
    crows_ = [jnp.where(r8 == 0, cm_[g][:, 0:128], 0.0) + jnp.where(r8 == 1, cm_[g][:, 128:256], 0.0) for g in group]
    y8_ = [_dot_nt(crows_[g], hn_[g], HI) for g in group]
    for g in group:
        y = jnp.where(l256 < 128, y8_[g][0:1, :], y8_[g][1:2, :])
        yd_ref[g] = _gated_rms(y, xs_[g], zg_ref[g], dskip_ref[...], dnorm_ref[...])


def _dec_ssm(act, dtv, zg, h, alog, dskip, dnorm):
    nb = act.shape[0]
    per = lambda r, w: pl.BlockSpec((SSM_GROUP, r, w), lambda b: (b, 0, 0))
    full = lambda a: pl.BlockSpec(a.shape, lambda b: (0,) * a.ndim)
    return pl.pallas_call(
        _dec_ssm_kernel,
        grid=(nb // SSM_GROUP,),
        in_specs=[per(1, D_XBC), per(1, 128), per(1, 256), per(256, 128), full(alog), full(dskip), full(dnorm)],
        out_specs=[per(1, 256), per(256, 128)],
        out_shape=[jax.ShapeDtypeStruct((nb, 1, 256), f32), jax.ShapeDtypeStruct((nb, 256, 128), f32)],
        compiler_params=_cparams(("arbitrary",)), name="dec_ssm",
    )(act, dtv, zg, h, alog, dskip, dnorm)


def _rope_tables(pos, head_dim):
    rot = head_dim // 4
    half = rot // 2
    inv = ROPE_THETA ** (-jnp.arange(half, dtype=f32) * 2.0 / rot)
    ang = pos.astype(f32)[:, None] * inv[None, :]
    cos, sin = jnp.cos(ang), jnp.sin(ang)
    n = pos.shape[0]
    pad = jnp.zeros((n, head_dim - rot), f32)
    c = jnp.concatenate([cos, cos, pad + 1.0], axis=1)
    sm = jnp.concatenate([-sin, jnp.zeros((n, half), f32), pad], axis=1)
    sp = jnp.concatenate([jnp.zeros((n, half), f32), sin, pad], axis=1)
    rep = 128 // head_dim
    return [jnp.tile(t, (1, rep)) for t in (c, sm, sp)]


def _pad_lanes(v, width=128):
    return jnp.pad(v.astype(f32), (0, width - v.shape[0]))[None, :]


def _layer_params(l, w_in, w_out, g_pre_mix, g_post_mix, g_pre_ffn, g_post_ffn, bconv_w, cconv_w, cconv_b, cln_g,
                  cln_b, dconv_w, dconv_b, dt_bias, a_log, d_skip, d_norm, ffn_gate, ffn_up, ffn_down):
    w = w_in[l]
    cuts = np.cumsum((0,) + IN_SIZES)
    col = lambda j: w[:, cuts[j]:cuts[j + 1]]
    zpad = lambda a: jnp.pad(a, ((0, 0), (0, 128 - a.shape[1])))
    row = lambda a: a[l][None, :].astype(f32)
    wih, wil = _split(jnp.concatenate([col(3), jnp.tile(col(4), (1, 128 // IDX_DIM)),
                                       zpad(jnp.concatenate([col(5), col(12)], axis=1))], axis=1))
    return dict(
        wm=jnp.concatenate([w[:, :768], w[:, cuts[6]:cuts[12]]], axis=1).astype(bf16), wih=wih, wil=wil,
        g_pre_mix=row(g_pre_mix), g_post_mix=row(g_post_mix), g_pre_ffn=row(g_pre_ffn), g_post_ffn=row(g_post_ffn),
        w_out=w_out[l].astype(bf16), wg=ffn_gate[l].astype(bf16), wu=ffn_up[l].astype(bf16), wd=ffn_down[l].astype(bf16),
        bw=jnp.pad(bconv_w[l], ((0, 8 - B_CONV), (0, 0))), cw=jnp.pad(cconv_w[l], ((0, 32 - C_CONV), (0, 0))),
        cb=row(cconv_b), lng=row(cln_g), lnb=row(cln_b),
        dw=jnp.pad(dconv_w[l], ((0, 8 - D_CONV), (0, 0))), db=row(dconv_b),
        dtb=_pad_lanes(dt_bias[l]), alog=_pad_lanes(a_log[l]),
        dskip=jnp.repeat(d_skip[l].astype(f32), HEAD_DIM)[None, :], dnorm=row(d_norm),
    )


def _finish(p, ya, yb, yc, yd, x, tm):
    return _outffn(ya, yb, yc, yd, x, p['w_out'], p['g_post_mix'], p['g_pre_ffn'], p['wg'], p['wu'], p['wd'],
                   p['g_post_ffn'], tm)


def kernel(x_prompt, x_sample, cache_k, cache_v, cache_kidx, page_table, state_bconv, state_cconv, state_dconv, state_ssm, w_in, w_out, g_pre_mix, g_post_mix, g_pre_ffn, g_post_ffn, bconv_w, cconv_w, cconv_b, cln_g, cln_b, dconv_w, dconv_b, dt_bias, a_log, d_skip, d_norm, ffn_gate, ffn_up, ffn_down):
    bsz, seq, _ = x_prompt.shape
    nb, t_dec, _ = x_sample.shape
    depth = w_in.shape[0]
    n_phys = cache_k.shape[1]
    n_pages = page_table.shape[1]
    past = n_pages * PAGE
    assert t_dec == 1 and seq % QB == 0 and seq % TC == 0 and past + t_dec > TOPK_MAX * 4
    assert nb % SSM_GROUP == 0 and nb % SCORE_GROUP == 0
    weights = (w_in, w_out, g_pre_mix, g_post_mix, g_pre_ffn, g_post_ffn, bconv_w, cconv_w, cconv_b, cln_g, cln_b,
               dconv_w, dconv_b, dt_bias, a_log, d_skip, d_norm, ffn_gate, ffn_up, ffn_down)
    pos_p = jnp.arange(seq, dtype=jnp.int32)
    pos_s = jnp.full((nb,), past, jnp.int32)
    tabs_p = _rope_tables(pos_p, HEAD_DIM) + _rope_tables(pos_p, IDX_DIM)
    tabs_s = _rope_tables(pos_s, HEAD_DIM) + _rope_tables(pos_s, IDX_DIM)
    ck = cache_k.transpose(0, 1, 3, 4, 2).reshape(depth * n_phys, 256, PAGE)
    cv = cache_v.transpose(0, 1, 3, 4, 2).reshape(depth * n_phys, 256, PAGE)
    cki = cache_kidx.transpose(0, 1, 3, 2).reshape(depth * n_phys, IDX_DIM, PAGE)
    pt = page_table.reshape(-1).astype(jnp.int32)
    lane_head = (jnp.arange(256) // HEAD_DIM)[None, None, :] == jnp.arange(8)[None, :, None]

    hp = x_prompt.reshape(bsz * seq, D_MODEL)
    hs = x_sample.reshape(nb, D_MODEL)
    outs_p, outs_s = [], []
    for l in range(depth):
        p = _layer_params(l, *weights)
        conv_prm = (p['bw'], p['cw'], p['cb'], p['lng'], p['lnb'], p['dw'], p['db'], p['dtb'])
        ssm_prm = (p['alog'], p['dskip'], p['dnorm'])

        q, k, kb, v, vt, qi, ki4, kcat, wi, dt, bgc, glu, zg, xbc = _inproj(
            hp, p['g_pre_mix'], p['wm'], p['wih'], p['wil'], tabs_p, 256)
        ya = _attn_prompt_t(q, qi, wi, kcat, kb, vt, bsz, seq)
        yb, yc, yd, nbp, ncp, ndp, ssm_p = _mix_prompt(bgc, glu, zg, xbc, dt, conv_prm + ssm_prm, bsz, seq)
        hp = _finish(p, ya, yb, yc, yd, hp, 512)
        outs_p.append((k.reshape(bsz, seq, N_HEADS, HEAD_DIM), v.reshape(bsz, seq, N_HEADS, HEAD_DIM),
                       ki4[:, :IDX_DIM].reshape(bsz, seq, IDX_DIM), nbp, ncp, ndp,
                       ssm_p.reshape(bsz, N_HEADS, HEAD_DIM, D_STATE)))

        q, k, kb, v, vt, qi, ki4, kcat, wi, dt, bgc, glu, zg, xbc = _inproj(
            hs, p['g_pre_mix'], p['wm'], p['wih'], p['wil'], tabs_s, nb)
        topk = min(TOPK_MAX, (past + t_dec) // 4)
        qi_s = qi.reshape(nb, N_IDX_HEADS, IDX_DIM)
        wi_t = jnp.broadcast_to((wi[:, :N_IDX_HEADS] * (IDX_DIM ** -0.5 * N_IDX_HEADS ** -0.5))[:, :, None],
                                (nb, N_IDX_HEADS, 128))
        sc = _dec_scores(pt, qi_s, wi_t, cki, n_pages, l * n_phys).reshape(nb, past)
        bias, bnew = _dec_select(sc, qi, ki4, wi, topk)
        qm = jnp.where(lane_head, q[:, None, :], 0.0)
        ya = _dec_attn(pt, qm, k.reshape(nb, 1, 256), v.reshape(nb, 1, 256), bias.reshape(nb, 1, past),
                       bnew.reshape(nb, 1, 128), ck, cv, n_pages, l * n_phys).reshape(nb, 256)
        yb, yc, act, dtv, nbs, ncs, nds = _dec_conv(
            bgc, glu, xbc, dt, state_bconv[l].reshape(nb, -1), state_cconv[l].reshape(nb, -1),
            state_dconv[l].reshape(nb, -1), conv_prm)
        yd, ssm_s = _dec_ssm(act.reshape(nb, 1, D_XBC), dtv.reshape(nb, 1, 128), zg.reshape(nb, 1, 256),
                             state_ssm[l].reshape(nb, 256, D_STATE), *ssm_prm)
        hs = _finish(p, ya, yb, yc, yd.reshape(nb, 256), hs, nb)
        outs_s.append((k.reshape(nb, 1, N_HEADS, HEAD_DIM), v.reshape(nb, 1, N_HEADS, HEAD_DIM),
                       ki4[:, :IDX_DIM].reshape(nb, 1, IDX_DIM), nbs.reshape(nb, 2, 256), ncs.reshape(nb, 30, 256),
                       nds.reshape(nb, 3, D_XBC), ssm_s.reshape(nb, N_HEADS, HEAD_DIM, D_STATE)))

    k_p, v_p, kidx_p, bconv_p, cconv_p, dconv_p, ssm_p = [jnp.stack(a) for a in zip(*outs_p)]
    k_s, v_s, kidx_s, bconv_s, cconv_s, dconv_s, ssm_s = [jnp.stack(a) for a in zip(*outs_s)]
    return (hp.reshape(bsz, seq, D_MODEL), hs.reshape(nb, t_dec, D_MODEL), k_p, v_p, kidx_p, k_s, v_s, kidx_s,
            bconv_p, bconv_s, cconv_p, cconv_s, dconv_p, dconv_s, ssm_p, ssm_s)
```

```python
import functools
import math

import jax
import jax.numpy as jnp
import numpy as np
from jax import lax
from jax.experimental import pallas as pl
from jax.experimental.pallas import tpu as pltpu

f32, bf16, i32 = jnp.float32, jnp.bfloat16, jnp.int32
HI = lax.Precision.HIGHEST

D_MODEL = 1024
PAGE = 128
GW = 256
HEAD_DIM = 64
N_HEADS = 4
N_IDX_HEADS = 8
IDX_DIM = 32
TOPK_MAX = 256
ROPE_THETA = 500000.0
C_CONV = 31
B_CONV = 3
D_CONV = 4
D_XBC = 768
D_STATE = 128
SSD_CHUNK = 128
D_FF = 2816
RMS_EPS = 1e-6
LN_EPS = 1e-5
IN_SIZES = (256, 256, 256, 256, 32, 8, 256, 256, 256, 512, 256, 768, 4)
INT_MIN = -(2 ** 31)
KEY_NEG_INF = INT_MIN + 0x7FFFFF
NEG = -1e30
VMEM_LIMIT = 56 * 1024 * 1024


def _dot_nt(a, b, prec=None):
    return lax.dot_general(a, b, (((1,), (1,)), ((), ())), precision=prec, preferred_element_type=f32)


def _dot(a, b, prec=None):
    return jnp.dot(a, b, precision=prec, preferred_element_type=f32)


def _cparams(sem):
    return pltpu.CompilerParams(dimension_semantics=sem, vmem_limit_bytes=VMEM_LIMIT)


def _rms(x, g):
    return x * lax.rsqrt(jnp.mean(x * x, axis=-1, keepdims=True) + RMS_EPS) * g


def _silu(x):
    return x * jax.nn.sigmoid(x)


def _key_to_float(key):
    return lax.bitcast_convert_type(jnp.where(key < 0, key ^ jnp.int32(0x7FFFFFFF), key), f32)


def _split_f32(x):
    c = x * (2.0 ** 16 + 1.0)
    hi = c - (c - x)
    return hi, x - hi


def _split(x):
    hi, lo = _split_f32(x)
    return hi.astype(bf16), lo.astype(bf16)


def _kth_largest(count_ge, shape, topk):
    def bit_step(bi, key):
        cand = key + lax.shift_left(jnp.int32(1), 31 - bi)
        ok = (cand <= KEY_NEG_INF) | (count_ge(_key_to_float(cand)) >= topk)
        return jnp.where(ok, cand, key)

    key = lax.fori_loop(0, 32, bit_step, jnp.full(shape, INT_MIN, i32))
    return key, _key_to_float(key)


def _refine_between_floats(count_ge, lo, key, rows, topk, steps=32):
    hi = _key_to_float(key + 1)

    def split(lo, hi):
        mid = lo + (hi - lo) * 0.5
        return mid, rows & (mid > lo) & (mid < hi)

    def any_open(lo, hi):
        return jnp.max(jnp.where(split(lo, hi)[1], 1.0, 0.0))

    def cond(c):
        return (c[2] > 0.0) & (c[3] < steps)

    def body(c):
        lo, hi, _, it = c
        mid, is_open = split(lo, hi)
        ge = count_ge(mid) >= topk
        lo = jnp.where(is_open & ge, mid, lo)
        hi = jnp.where(is_open & jnp.logical_not(ge), mid, hi)
        return lo, hi, any_open(lo, hi), it + 1

    return lax.while_loop(cond, body, (lo, hi, any_open(lo, hi), jnp.int32(0)))[0]


def _rope(v, c, sm, sp, half):
    outs = []
    for s in range(v.shape[1] // 128):
        xs = v[:, s * 128:(s + 1) * 128]
        outs.append(xs * c + pltpu.roll(xs, 128 - half, 1) * sm + pltpu.roll(xs, half, 1) * sp)
    return jnp.concatenate(outs, axis=1)


def _inproj_kernel(x_ref, g_ref, wm_ref, wih_ref, wil_ref, c64_ref, sm64_ref, sp64_ref, c32_ref, sm32_ref, sp32_ref,
                   q_ref, k_ref, kb_ref, v_ref, vt_ref, qi_ref, ki_ref, kcat_ref, wi_o_ref, dt_o_ref,
                   bgc_ref, glu_ref, zg_ref, xbc_ref):
    u = _rms(x_ref[...], g_ref[...])
    ub, ul = _split(u)

    def mm(c0, c1):
        return _dot(ub, wm_ref[:, c0:c1])

    def mm3(c0, c1):
        wh = wih_ref[:, c0:c1]
        return _dot(ub, wh) + _dot(ul, wh) + _dot(ub, wil_ref[:, c0:c1])

    c64, sm64, sp64 = c64_ref[...], sm64_ref[...], sp64_ref[...]
    q_ref[...] = _rope(mm(0, 256), c64, sm64, sp64, 8)
    k = _rope(mm(256, 512), c64, sm64, sp64, 8)
    k_ref[...] = k
    kb_ref[...] = k.astype(bf16)
    v = mm(512, 768)
    v_ref[...] = v
    vt_ref[...] = v.T.astype(bf16)
    bgc_ref[...] = mm(768, 1536)
    glu_ref[...] = mm(1536, 2048)
    zg_ref[...] = mm(2048, 2304)
    xbc_ref[...] = mm(2304, 3072)
    c32, sm32, sp32 = c32_ref[...], sm32_ref[...], sp32_ref[...]
    qi_ref[...] = _rope(mm3(0, 256), c32, sm32, sp32, 4)
    ki4 = _rope(mm3(256, 384), c32, sm32, sp32, 4)
    ki_ref[...] = ki4
    kh, kl = _split(ki4)
    lane = lax.broadcasted_iota(i32, ki4.shape, 1)
    kcat_ref[...] = jnp.where((lane >= IDX_DIM) & (lane < 2 * IDX_DIM), kl, kh)
    small = mm3(384, 512)
    wi_o_ref[...] = small
    dt_o_ref[...] = pltpu.roll(small, 128 - N_IDX_HEADS, 1)


def _inproj(x, g, wm, wih, wil, tabs, tm):
    t = x.shape[0]
    nt = t // tm
    ntab = tabs[0].shape[0] // tm
    row = lambda w: pl.BlockSpec((tm, w), lambda i: (i, 0))
    full = lambda a: pl.BlockSpec(a.shape, lambda i: (0,) * a.ndim)
    tab = pl.BlockSpec((tm, 128), lambda i: (i % ntab, 0))
    widths = (256, 256, 256, 256, 256, 256, 128, 128, 128, 128, 768, 512, 256, 768)
    dtypes = (f32, f32, bf16, f32, bf16, f32, f32, bf16, f32, f32, f32, f32, f32, f32)
    out_specs = [row(w) for w in widths]
    out_shape = [jax.ShapeDtypeStruct((t, w), d) for w, d in zip(widths, dtypes)]
    out_specs[4] = pl.BlockSpec((256, tm), lambda i: (0, i))
    out_shape[4] = jax.ShapeDtypeStruct((256, t), bf16)
    return pl.pallas_call(
        _inproj_kernel,
        grid=(nt,),
        in_specs=[row(D_MODEL), full(g), full(wm), full(wih), full(wil)] + [tab] * 6,
        out_specs=out_specs,
        out_shape=out_shape,
        compiler_params=_cparams(("arbitrary",)),
        name="inproj",
    )(x, g, wm, wih, wil, *tabs)


QB = 256
SUB = 128


def _attn_t_kernel(q_ref, qi_ref, wi_ref, kcat_ref, k_ref, vt_ref, o_ref, sc_ref, cat_ref, qm_ref, acc_ref, *, topk):
    i = pl.program_id(1)
    nkb = i + 1
    n_sub = QB // SUB
    hi, lo = _split_f32(qi_ref[...])
    hi_t, lo_t = hi.T, lo.T
    for h in range(N_IDX_HEADS):
        rs = slice(h * IDX_DIM, (h + 1) * IDX_DIM)
        cat_ref[h] = jnp.concatenate([hi_t[rs], hi_t[rs], lo_t[rs], jnp.zeros((IDX_DIM, QB), f32)], axis=0).astype(bf16)
    q_t = (q_ref[...] * (HEAD_DIM ** -0.5)).T
    row = lax.broadcasted_iota(i32, (256, QB), 0)
    for h in range(N_HEADS):
        qm_ref[:, h * QB:(h + 1) * QB] = jnp.where(row // HEAD_DIM == h, q_t, 0.0).astype(bf16)
    w8 =(wi_ref[...] * (IDX_DIM ** -0.5 * N_IDX_HEADS ** -0.5)).T[0:N_IDX_HEADS]
    qpos = i * QB + lax.broadcasted_iota(i32, (1, QB), 1)
    kio = lax.broadcasted_iota(i32, (SUB, 1), 0)

    def tiles(kb):
        return [pl.ds(pl.multiple_of(kb * QB + j * SUB, SUB), SUB) for j in range(n_sub)]

    def score_block(kb, carry):
        for j, sl in enumerate(tiles(kb)):
            kc = kcat_ref[sl, :]
            acc = jnp.zeros((SUB, QB), f32)
            for h in range(N_IDX_HEADS):
                acc = acc + jnp.maximum(_dot(kc, cat_ref[h]), 0.0) * w8[h:h + 1, :]
            sc_ref[sl, :] = jnp.where(kb * QB + j * SUB + kio <= qpos, acc, -jnp.inf)
        return carry

    lax.fori_loop(0, nkb, score_block, 0)

    def count(pred):
        def body(kb, cnt):
            for sl in tiles(kb):
                cnt = cnt + jnp.where(pred(sc_ref[sl, :]), 1.0, 0.0).reshape(SUB // 8, 8, QB).sum(axis=0)
            return cnt

        def body2(kb2, cnt):
            return body(2 * kb2 + 1, body(2 * kb2, cnt))

        cnt = lax.fori_loop(0, nkb // 2, body2, jnp.zeros((8, QB), f32))
        cnt = lax.fori_loop(2 * (nkb // 2), nkb, body, cnt)
        return jnp.sum(cnt, axis=0, keepdims=True)

    def count_ge(t):
        return count(lambda s: s >= t)

    thr_key, thr = _kth_largest(count_ge, (1, QB), topk)
    real = thr > -jnp.inf
    tie = real & (count_ge(thr) > topk)
    any_tie = jnp.max(jnp.where(tie, 1.0, 0.0)) > 0.0

    @pl.when(jnp.logical_not(any_tie))
    def _():
        thr_fin = jnp.maximum(thr, jnp.finfo(f32).min)

        def mask_block(kb, carry):
            for sl in tiles(kb):
                sc_ref[sl, :] = jnp.where(sc_ref[sl, :] >= thr_fin, 0.0, NEG)
            return carry

        lax.fori_loop(0, nkb, mask_block, 0)

    @pl.when(any_tie)
    def _():
        thr2 = _refine_between_floats(count_ge, thr, thr_key, tie, topk)
        need = topk - count(lambda s: s > thr2)
        realf = jnp.where(real, 1.0, 0.0)
        r = lax.broadcasted_iota(i32, (SUB, SUB), 0)
        c = lax.broadcasted_iota(i32, (SUB, SUB), 1)
        tri = jnp.where(c <= r, 1.0, 0.0).astype(bf16)

        def tie_blocks(kb0, carry, n):
            sls = [sl for j in range(n) for sl in tiles(kb0 + j)]
            eqbs = [jnp.where(sc_ref[sl, :] == thr2, realf, 0.0).astype(bf16) for sl in sls]
            locs = [_dot(tri, e) for e in eqbs]
            for sl, eqb, loc in zip(sls, eqbs, locs):
                cum = loc + carry
                keep = jnp.where(sc_ref[sl, :] > thr2, 1.0, jnp.where(cum <= need, eqb.astype(f32), 0.0))
                sc_ref[sl, :] = jnp.where(keep > 0.0, 0.0, NEG)
                carry = cum[SUB - 1:SUB, :]
            return carry

        carry = lax.fori_loop(0, nkb // 2, lambda kb2, c: tie_blocks(2 * kb2, c, 2), jnp.zeros((1, QB), f32))
        lax.fori_loop(2 * (nkb // 2), nkb, lambda kb, c: tie_blocks(kb, c, 1), carry)

    heads = [slice(h * HEAD_DIM, (h + 1) * HEAD_DIM) for h in range(N_HEADS)]

    def attn_blocks(kb0, carry, n):
        ms, ls = list(carry[0]), list(carry[1])
        sls = [pl.ds(pl.multiple_of((kb0 + j) * QB, QB), QB) for j in range(n)]
        s_alls = [_dot(k_ref[sl, :], qm_ref[...]) for sl in sls]
        alphas, pvs = [], []
        for j, sl in enumerate(sls):
            mask = sc_ref[sl, :]
            al, ps = [], []
            for h in range(N_HEADS):
                s = s_alls[j][:, h * QB:(h + 1) * QB] + mask
                m_new = jnp.maximum(ms[h], jnp.max(s, axis=0, keepdims=True))
                al.append(jnp.exp(ms[h] - m_new))
                p = jnp.exp(s - m_new)
                ls[h] = al[h] * ls[h] + jnp.sum(p, axis=0, keepdims=True)
                ps.append(p.astype(bf16))
                ms[h] = m_new
            alphas.append(al)
            pvs.append([_dot(vt_ref[heads[h], sl], ps[h]) for h in range(N_HEADS)])
        for j in range(n):
            for h in range(N_HEADS):
                acc_ref[heads[h], :] = alphas[j][h] * acc_ref[heads[h], :] + pvs[j][h]
        return tuple(ms), tuple(ls)

    acc_ref[...] = jnp.zeros_like(acc_ref)
    m0 = tuple(jnp.full((1, QB), NEG, f32) for _ in range(N_HEADS))
    l0 = tuple(jnp.zeros((1, QB), f32) for _ in range(N_HEADS))
    carry = lax.fori_loop(0, nkb // 2, lambda kb2, c: attn_blocks(2 * kb2, c, 2), (m0, l0))
    _, ls = lax.fori_loop(2 * (nkb // 2), nkb, lambda kb, c: attn_blocks(kb, c, 1), carry)
    out_t = jnp.concatenate([acc_ref[h * HEAD_DIM:(h + 1) * HEAD_DIM, :] / ls[h] for h in range(N_HEADS)], axis=0)
    o_ref[...] = out_t.T


def _attn_prompt_t(q, qi, wi, kcat, kb, vt, bsz, seq):
    nq = seq // QB
    topk = min(TOPK_MAX, seq // 4)
    qrow = lambda w: pl.BlockSpec((QB, w), lambda b, i: (b * nq + i, 0))
    seqblk = lambda w: pl.BlockSpec((seq, w), lambda b, i: (b, 0))
    return pl.pallas_call(
        functools.partial(_attn_t_kernel, topk=float(topk)),
        grid=(bsz, nq),
        in_specs=[qrow(256), qrow(256), qrow(128), seqblk(128), seqblk(256),
                  pl.BlockSpec((256, seq), lambda b, i: (0, b))],
        out_specs=qrow(256),
        out_shape=jax.ShapeDtypeStruct((bsz * seq, 256), f32),
        scratch_shapes=[pltpu.VMEM((seq, QB), f32), pltpu.VMEM((N_IDX_HEADS, 128, QB), bf16),
                        pltpu.VMEM((256, N_HEADS * QB), bf16), pltpu.VMEM((256, QB), f32)],
        compiler_params=_cparams(("arbitrary", "arbitrary")),
        name="attn_prompt",
    )(q, qi, wi, kcat, kb, vt)


TC = 256
HALO_B, HALO_C, HALO_D = 8, 32, 8


def _per_head_lanes(a):
    lane = lax.broadcasted_iota(i32, (a.shape[0], N_HEADS * HEAD_DIM), 1)
    out = jnp.broadcast_to(a[:, N_HEADS - 1:N_HEADS], lane.shape)
    for h in range(N_HEADS - 2, -1, -1):
        out = jnp.where(lane < (h + 1) * HEAD_DIM, a[:, h:h + 1], out)
    return out


def _cumsum_rows(x):
    n = x.shape[0]
    r = lax.broadcasted_iota(i32, (n, n), 0)
    c = lax.broadcasted_iota(i32, (n, n), 1)
    tril = jnp.where(c <= r, 1.0, 0.0).astype(bf16)
    p1, rest = _split_f32(x)
    p2, p3 = _split_f32(rest)
    return _dot(tril, p1.astype(bf16)) + _dot(tril, p2.astype(bf16)) + _dot(tril, p3.astype(bf16))


def _ssd_chunk(xs, bm, cm, dtv, a_row, hcat_ref):
    r = lax.broadcasted_iota(i32, (128, 128), 0)
    c = lax.broadcasted_iota(i32, (128, 128), 1)
    causal = c <= r
    lane = lax.broadcasted_iota(i32, (128, 256), 1)
    da = dtv * a_row
    cs = _cumsum_rows(da)
    cs_t = cs.T
    dt_x = _per_head_lanes(dtv)
    ecs_x = _per_head_lanes(jnp.exp(cs))
    cs_last = cs[127:128, :]
    wend_x = _per_head_lanes(jnp.exp(cs_last - cs) * dtv)
    xdt = (xs * dt_x).astype(bf16)
    bmb, cmb = bm.astype(bf16), cm.astype(bf16)
    hb = hcat_ref[...].astype(bf16)
    y = jnp.zeros((128, 256), f32)
    ystate = []
    for g in range(2):
        cg = cmb[:, g * 128:(g + 1) * 128]
        cb = _dot_nt(cg, bmb[:, g * 128:(g + 1) * 128])
        ystate.append(_dot_nt(cg, hb))
        for h in (2 * g, 2 * g + 1):
            seg = cs[:, h:h + 1] - cs_t[h:h + 1, :]
            dec = jnp.where(causal, jnp.exp(jnp.where(causal, seg, 0.0)), 0.0)
            yh = _dot((cb * dec).astype(bf16), xdt)
            y = jnp.where(lane // 64 == h, yh, y)
    y = y + jnp.where(lane < 128, ystate[0], ystate[1]) * ecs_x
    xw_t = (xs * wend_x).T.astype(bf16)
    upd = jnp.concatenate([_dot(xw_t[0:128], bmb[:, 0:128]), _dot(xw_t[128:256], bmb[:, 128:256])], axis=0)
    elast = jnp.exp(cs_last)
    dcol = jnp.concatenate([jnp.broadcast_to(elast[:, h:h + 1], (HEAD_DIM, D_STATE)) for h in range(N_HEADS)], axis=0)
    hcat_ref[...] = hcat_ref[...] * dcol + upd
    return y


def _gated_rms(ys, xs, zg, dskip, dnorm):
    yg = (ys + dskip * xs) * _silu(zg)
    return yg * lax.rsqrt(jnp.mean(yg * yg, axis=-1, keepdims=True) + RMS_EPS) * dnorm


def _layer_norm_silu(x, g, b):
    mu = jnp.mean(x, axis=-1, keepdims=True)
    var = jnp.mean(jnp.square(x - mu), axis=-1, keepdims=True)
    return _silu((x - mu) * lax.rsqrt(var + LN_EPS) * g + b)


def _mix_kernel(bgc_ref, glu_ref, zg_ref, xbc_ref, dt_ref,
                bw_ref, cw_ref, cb_ref, lng_ref, lnb_ref, dw_ref, db_ref, dtb_ref, alog_ref, dskip_ref, dnorm_ref,
                yb_ref, yc_ref, yd_ref, nb_ref, nc_ref, nd_ref, ssm_ref,
                eb_ref, ec_ref, ed_ref, hcat_ref, pc_ref):
    i = pl.program_id(1)

    @pl.when(i == 0)
    def _():
        eb_ref[0:HALO_B, :] = jnp.zeros((HALO_B, 256), f32)
        ec_ref[0:HALO_C, :] = jnp.zeros((HALO_C, 256), f32)
        ed_ref[0:HALO_D, :] = jnp.zeros((HALO_D, D_XBC), f32)
        hcat_ref[...] = jnp.zeros_like(hcat_ref)

    bgc = bgc_ref[...]
    eb_ref[HALO_B:HALO_B + TC, :] = bgc[:, 512:768] * bgc[:, 0:256]
    conv = jnp.zeros((TC, 256), f32)
    for k in range(B_CONV):
        conv = conv + bw_ref[k:k + 1, :] * eb_ref[pl.ds(HALO_B - (B_CONV - 1) + k, TC), :]
    yb_ref[...] = bgc[:, 256:512] * conv
    nb_ref[0] = eb_ref[HALO_B + TC - (B_CONV - 1):HALO_B + TC, :]
    eb_ref[0:HALO_B, :] = eb_ref[TC:TC + HALO_B, :]

    glu = glu_ref[...]
    ec_ref[HALO_C:HALO_C + TC, :] = glu[:, 0:256] * jax.nn.sigmoid(glu[:, 256:512])
    off = HALO_C - (C_CONV - 1)
    conv = jnp.zeros((TC, 256), f32)
    for r in range(8):
        taps = [j for j in range(r, HALO_C + 1, 8) if off <= j < off + C_CONV]
        rows = TC if r == 0 else TC + 8
        part = jnp.zeros((rows, 256), f32)
        for j in taps:
            part = part + cw_ref[j - off:j - off + 1, :] * ec_ref[j - r:j - r + rows, :]
        if r == 0:
            conv = conv + part
        else:
            pc_ref[...] = part
            conv = conv + pc_ref[r:r + TC, :]
    yc_ref[...] = _layer_norm_silu(conv + cb_ref[...], lng_ref[...], lnb_ref[...])
    nc_ref[0] = ec_ref[HALO_C + TC - (C_CONV - 1):HALO_C + TC, :]
    ec_ref[0:HALO_C, :] = ec_ref[TC:TC + HALO_C, :]

    ed_ref[HALO_D:HALO_D + TC, :] = xbc_ref[...]
    conv = jnp.zeros((TC, D_XBC), f32)
    for k in range(D_CONV):
        conv = conv + dw_ref[k:k + 1, :] * ed_ref[pl.ds(HALO_D - (D_CONV - 1) + k, TC), :]
    act = _silu(conv + db_ref[...])
    nd_ref[0] = ed_ref[HALO_D + TC - (D_CONV - 1):HALO_D + TC, :]
    ed_ref[0:HALO_D, :] = ed_ref[TC:TC + HALO_D, :]
    l128 = lax.broadcasted_iota(i32, (1, 128), 1)
    a_row = jnp.where(l128 < N_HEADS, -jnp.exp(alog_ref[...]), 0.0)
    dtv = jax.nn.softplus(dt_ref[...] + dtb_ref[...])
    zg = zg_ref[...]
    for j in range(TC // SSD_CHUNK):
        rs = slice(j * SSD_CHUNK, (j + 1) * SSD_CHUNK)
        xs = act[rs, 0:256]
        y = _ssd_chunk(xs, act[rs, 256:512], act[rs, 512:768], dtv[rs], a_row, hcat_ref)
        yd_ref[rs, :] = _gated_rms(y, xs, zg[rs], dskip_ref[...], dnorm_ref[...])
    ssm_ref[0] = hcat_ref[...]


def _mix_prompt(bgc, glu, zg, xbc, dt, prm, bsz, seq):
    nt = seq // TC
    row = lambda w: pl.BlockSpec((TC, w), lambda b, i: (b * nt + i, 0))
    full = lambda a: pl.BlockSpec(a.shape, lambda b, i: (0,) * a.ndim)
    st = lambda r, w: pl.BlockSpec((1, r, w), lambda b, i: (b, 0, 0))
    t = bsz * seq
    return pl.pallas_call(
        _mix_kernel,
        grid=(bsz, nt),
        in_specs=[row(768), row(512), row(256), row(768), row(128)] + [full(a) for a in prm],
        out_specs=[row(256), row(256), row(256), st(2, 256), st(30, 256), st(3, D_XBC), st(256, 128)],
        out_shape=[jax.ShapeDtypeStruct((t, 256), f32)] * 3 + [
            jax.ShapeDtypeStruct((bsz, 2, 256), f32), jax.ShapeDtypeStruct((bsz, 30, 256), f32),
            jax.ShapeDtypeStruct((bsz, 3, D_XBC), f32), jax.ShapeDtypeStruct((bsz, 256, 128), f32)],
        scratch_shapes=[pltpu.VMEM((HALO_B + TC, 256), f32), pltpu.VMEM((HALO_C + TC, 256), f32),
                        pltpu.VMEM((HALO_D + TC, D_XBC), f32), pltpu.VMEM((256, 128), f32),
                        pltpu.VMEM((TC + 8, 256), f32)],
        compiler_params=_cparams(("arbitrary", "arbitrary")),
        name="mix_prompt",
    )(bgc, glu, zg, xbc, dt, *prm)


FF_CHUNK = 1408


def _outffn_kernel(ya_ref, yb_ref, yc_ref, yd_ref, x_ref, wo_ref, gpost_ref, gpre_ref, wg_ref, wu_ref, wd_ref, g_ref,
                   o_ref, acc_ref, x1_ref, hf_ref):
    j = pl.program_id(1)

    @pl.when(j == 0)
    def _():
        mix = _dot(ya_ref[...].astype(bf16), wo_ref[0:256, :])
        mix = mix + _dot(yb_ref[...].astype(bf16), wo_ref[256:512, :])
        mix = mix + _dot(yc_ref[...].astype(bf16), wo_ref[512:768, :])
        mix = mix + _dot(yd_ref[...].astype(bf16), wo_ref[768:1024, :])
        x1 = x_ref[...] + _rms(mix, gpost_ref[...])
        x1_ref[...] = x1
        hf_ref[...] = _rms(x1, gpre_ref[...]).astype(bf16)

    hf = hf_ref[...]
    a = _silu(_dot(hf, wg_ref[...])) * _dot(hf, wu_ref[...])
    part = _dot(a.astype(bf16), wd_ref[...])

    @pl.when(j == 0)
    def _():
        acc_ref[...] = part

    @pl.when(j > 0)
    def _():
        acc_ref[...] = acc_ref[...] + part

    @pl.when(j == pl.num_programs(1) - 1)
    def _():
        o_ref[...] = x1_ref[...] + _rms(acc_ref[...], g_ref[...])


def _outffn(ya, yb, yc, yd, x, wo, gpost, gpre, wg, wu, wd, g, tm):
    t = x.shape[0]
    nj = D_FF // FF_CHUNK
    row = lambda w_: pl.BlockSpec((tm, w_), lambda i, j: (i, 0))
    full = lambda a: pl.BlockSpec(a.shape, lambda i, j: (0,) * a.ndim)
    return pl.pallas_call(
        _outffn_kernel,
        grid=(t // tm, nj),
        in_specs=[row(256)] * 4 + [row(D_MODEL), full(wo), full(gpost), full(gpre),
                                   pl.BlockSpec((D_MODEL, FF_CHUNK), lambda i, j: (0, j)),
                                   pl.BlockSpec((D_MODEL, FF_CHUNK), lambda i, j: (0, j)),
                                   pl.BlockSpec((FF_CHUNK, D_MODEL), lambda i, j: (j, 0)), full(g)],
        out_specs=row(D_MODEL),
        out_shape=jax.ShapeDtypeStruct((t, D_MODEL), f32),
        scratch_shapes=[pltpu.VMEM((tm, D_MODEL), f32), pltpu.VMEM((tm, D_MODEL), f32),
                        pltpu.VMEM((tm, D_MODEL), bf16)],
        compiler_params=_cparams(("arbitrary", "arbitrary")),
        name="outffn",
    )(ya, yb, yc, yd, x, wo, gpost, gpre, wg, wu, wd, g)


def _page_copies(pt_ref, src_hbm, dst, sem, b, n_pages, base):
    return [pltpu.make_async_copy(src_hbm.at[base + pt_ref[b * n_pages + p]],
                                  dst.at[:, pl.ds(p * PAGE, PAGE)], sem) for p in range(n_pages)]


SCORE_GROUP = 8


def _dec_score_kernel(pt_ref, qi_ref, wi_ref, cki_hbm, o_ref, buf, sem, *, n_pages, base):
    g = pl.program_id(0)
    ng = pl.num_programs(0)

    def copies(gg, slot):
        return [c for j in range(SCORE_GROUP)
                for c in _page_copies(pt_ref, cki_hbm, buf.at[slot, j], sem.at[slot], gg * SCORE_GROUP + j, n_pages, base)]

    @pl.when(g == 0)
    def _():
        for c in copies(0, 0):
            c.start()

    @pl.when(g + 1 < ng)
    def _():
        for c in copies(g + 1, (g + 1) % 2):
            c.start()

    slot = g % 2
    for c in copies(g, slot):
        c.wait()
    for j in range(SCORE_GROUP):
        b = g * SCORE_GROUP + j
        s = _dot(qi_ref[b], buf[slot, j], HI)
        w = jnp.tile(wi_ref[b], (1, n_pages))
        o_ref[j] = jnp.sum(jnp.maximum(s, 0.0) * w, axis=0, keepdims=True)


def _dec_scores(pt, qi_s, wi_t, cki, n_pages, base):
    nb = qi_s.shape[0]
    past = n_pages * PAGE
    gs = pltpu.PrefetchScalarGridSpec(
        num_scalar_prefetch=1, grid=(nb // SCORE_GROUP,),
        in_specs=[pl.BlockSpec(qi_s.shape, lambda b, pt_: (0, 0, 0)), pl.BlockSpec(wi_t.shape, lambda b, pt_: (0, 0, 0)),
                  pl.BlockSpec(memory_space=pl.ANY)],
        out_specs=pl.BlockSpec((SCORE_GROUP, 1, past), lambda b, pt_: (b, 0, 0)),
        scratch_shapes=[pltpu.VMEM((2, SCORE_GROUP, IDX_DIM, past), f32), pltpu.SemaphoreType.DMA((2,))])
    return pl.pallas_call(
        functools.partial(_dec_score_kernel, n_pages=n_pages, base=base),
        grid_spec=gs, out_shape=jax.ShapeDtypeStruct((nb, 1, past), f32),
        compiler_params=_cparams(("arbitrary",)), name="dec_scores",
    )(pt, qi_s, wi_t, cki)


def _dec_select_kernel(sc_ref, qi_ref, ki_ref, wi_ref, bias_ref, bnew_ref, *, topk):
    nb, past = sc_ref.shape
    wi = wi_ref[...] * (IDX_DIM ** -0.5 * N_IDX_HEADS ** -0.5)
    gj = lax.broadcasted_iota(i32, (256, 128), 0)
    gh = lax.broadcasted_iota(i32, (256, 128), 1)
    seg = jnp.where(gj // IDX_DIM == gh, 1.0, 0.0)
    ki = jnp.concatenate([ki_ref[...], ki_ref[...]], axis=1)
    s_new = _dot(qi_ref[...] * ki, seg, HI)
    sc_new = jnp.broadcast_to(jnp.sum(jnp.maximum(s_new, 0.0) * wi, axis=-1, keepdims=True), (nb, 128))
    sc = sc_ref[...]
    ones = jnp.ones((past, 128), bf16)
    wide = lambda t: jnp.concatenate([t] * (past // 128), axis=1)

    def count(pred_past, pred_new):
        return _dot(jnp.where(pred_past, 1.0, 0.0).astype(bf16), ones) + jnp.where(pred_new, 1.0, 0.0)

    def count_ge(t):
        return count(sc >= wide(t), sc_new >= t)

    thr_key, thr = _kth_largest(count_ge, (nb, 128), topk)
    bias_ref[...] = jnp.where(sc >= wide(thr), 0.0, NEG)
    bnew_ref[...] = jnp.where(sc_new >= thr, 0.0, NEG)
    tie = count_ge(thr) > topk

    @pl.when(jnp.max(jnp.where(tie, 1.0, 0.0)) > 0.0)
    def _():
        thr2 = _refine_between_floats(count_ge, thr, thr_key, tie, topk)
        need = topk - count(sc > wide(thr2), sc_new > thr2)
        eqb = jnp.where(sc == wide(thr2), 1.0, 0.0).astype(bf16)
        r = lax.broadcasted_iota(i32, (PAGE, PAGE), 0)
        c = lax.broadcasted_iota(i32, (PAGE, PAGE), 1)
        tri = jnp.where(r <= c, 1.0, 0.0).astype(bf16)
        carry = jnp.zeros((nb, 128), f32)
        for p in range(past // PAGE):
            sl = slice(p * PAGE, (p + 1) * PAGE)
            cum = _dot(eqb[:, sl], tri) + carry
            keep = jnp.where(sc[:, sl] > thr2, 1.0, jnp.where(cum <= need, eqb[:, sl].astype(f32), 0.0))
            bias_ref[:, sl] = jnp.where(keep > 0.0, 0.0, NEG)
            carry = carry + _dot(eqb[:, sl], ones[0:PAGE])
        keep_new = jnp.where(sc_new > thr2, 1.0, jnp.where((sc_new == thr2) & (carry + 1.0 <= need), 1.0, 0.0))
        bnew_ref[...] = jnp.where(keep_new > 0.0, 0.0, NEG)


def _dec_select(sc, qi, ki4, wi, topk):
    nb, past = sc.shape
    return pl.pallas_call(
        functools.partial(_dec_select_kernel, topk=float(topk)),
        out_shape=[jax.ShapeDtypeStruct((nb, past), f32), jax.ShapeDtypeStruct((nb, 128), f32)],
        compiler_params=pltpu.CompilerParams(vmem_limit_bytes=VMEM_LIMIT), name="dec_select",
    )(sc, qi, ki4, wi)


DEC_SLOTS = 3


def _dec_attn_kernel(pt_ref, qm_ref, kn_ref, vn_ref, bias_ref, bnew_ref, ck_hbm, cv_hbm, o_ref,
                     kbuf, vbuf, sem, *, n_pages, base):
    b = pl.program_id(0)
    nb = pl.num_programs(0)

    def copies(bb, slot):
        return (_page_copies(pt_ref, ck_hbm, kbuf.at[slot], sem.at[0, slot], bb, n_pages, base)
                + _page_copies(pt_ref, cv_hbm, vbuf.at[slot], sem.at[1, slot], bb, n_pages, base))

    @pl.when(b == 0)
    def _():
        for c in copies(0, 0):
            c.start()

    @pl.when((b == 0) & (nb > 1))
    def _():
        for c in copies(1, 1):
            c.start()

    @pl.when(b + 2 < nb)
    def _():
        for c in copies(b + 2, (b + 2) % DEC_SLOTS):
            c.start()

    slot = b % DEC_SLOTS
    for c in copies(b, slot):
        c.wait()
    qm = qm_ref[0] * (HEAD_DIM ** -0.5)
    s = _dot(qm.astype(bf16), kbuf[slot].astype(bf16)) + bias_ref[0]
    s_new = jnp.sum(qm * kn_ref[0], axis=-1, keepdims=True) + bnew_ref[0][:, 0:1]
    m = jnp.maximum(jnp.max(s, axis=-1, keepdims=True), s_new)
    p = jnp.exp(s - m)
    p_new = jnp.exp(s_new - m)
    den = jnp.sum(p, axis=-1, keepdims=True) + p_new
    out8 = (_dot_nt(p.astype(bf16), vbuf[slot].astype(bf16)) + p_new * vn_ref[0]) / den
    row = lax.broadcasted_iota(i32, (8, 256), 0)
    lane = lax.broadcasted_iota(i32, (8, 256), 1)
    o_ref[0] = jnp.sum(jnp.where(lane // HEAD_DIM == row, out8, 0.0), axis=0, keepdims=True)


def _dec_attn(pt, qm, kn, vn, bias, bnew, ck, cv, n_pages, base):
    nb = qm.shape[0]
    past = n_pages * PAGE
    per = lambda r, w: pl.BlockSpec((1, r, w), lambda b, pt_: (b, 0, 0))
    gs = pltpu.PrefetchScalarGridSpec(
        num_scalar_prefetch=1, grid=(nb,),
        in_specs=[per(8, 256), per(1, 256), per(1, 256), per(1, past), per(1, 128),
                  pl.BlockSpec(memory_space=pl.ANY), pl.BlockSpec(memory_space=pl.ANY)],
        out_specs=per(1, 256),
        scratch_shapes=[pltpu.VMEM((DEC_SLOTS, 256, past), f32), pltpu.VMEM((DEC_SLOTS, 256, past), f32),
                        pltpu.SemaphoreType.DMA((2, DEC_SLOTS))])
    return pl.pallas_call(
        functools.partial(_dec_attn_kernel, n_pages=n_pages, base=base),
        grid_spec=gs, out_shape=jax.ShapeDtypeStruct((nb, 1, 256), f32),
        compiler_params=_cparams(("arbitrary",)), name="dec_attn",
    )(pt, qm, kn, vn, bias, bnew, ck, cv)


def _dec_conv_kernel(bgc_ref, glu_ref, xbc_ref, dt_ref, sb_ref, sc_ref, sd_ref,
                     bw_ref, cw_ref, cb_ref, lng_ref, lnb_ref, dw_ref, db_ref, dtb_ref,
                     yb_ref, yc_ref, act_ref, dtv_ref, nb_ref, nc_ref, nd_ref):
    bgc = bgc_ref[...]
    ub = bgc[:, 512:768] * bgc[:, 0:256]
    sb = sb_ref[...]
    conv = bw_ref[0:1, :] * sb[:, 0:256] + bw_ref[1:2, :] * sb[:, 256:512] + bw_ref[2:3, :] * ub
    yb_ref[...] = bgc[:, 256:512] * conv
    nb_ref[...] = jnp.concatenate([sb[:, 256:512], ub], axis=1)

    glu = glu_ref[...]
    uc = glu[:, 0:256] * jax.nn.sigmoid(glu[:, 256:512])
    conv = cw_ref[C_CONV - 1:C_CONV, :] * uc
    for k in range(C_CONV - 1):
        conv = conv + cw_ref[k:k + 1, :] * sc_ref[:, k * 256:(k + 1) * 256]
    yc_ref[...] = _layer_norm_silu(conv + cb_ref[...], lng_ref[...], lnb_ref[...])
    nc_ref[:, 0:(C_CONV - 2) * 256] = sc_ref[:, 256:(C_CONV - 1) * 256]
    nc_ref[:, (C_CONV - 2) * 256:(C_CONV - 1) * 256] = uc

    xbc = xbc_ref[...]
    conv = dw_ref[D_CONV - 1:D_CONV, :] * xbc
    for k in range(D_CONV - 1):
        conv = conv + dw_ref[k:k + 1, :] * sd_ref[:, k * D_XBC:(k + 1) * D_XBC]
    act_ref[...] = _silu(conv + db_ref[...])
    nd_ref[:, 0:(D_CONV - 2) * D_XBC] = sd_ref[:, D_XBC:(D_CONV - 1) * D_XBC]
    nd_ref[:, (D_CONV - 2) * D_XBC:(D_CONV - 1) * D_XBC] = xbc
    dtv_ref[...] = jax.nn.softplus(dt_ref[...] + dtb_ref[...])


def _dec_conv(bgc, glu, xbc, dt, sb, sc, sd, prm):
    nb = bgc.shape[0]
    shp = lambda w: jax.ShapeDtypeStruct((nb, w), f32)
    return pl.pallas_call(
        _dec_conv_kernel,
        out_shape=[shp(256), shp(256), shp(D_XBC), shp(128), shp(2 * 256), shp(30 * 256), shp(3 * D_XBC)],
        compiler_params=pltpu.CompilerParams(vmem_limit_bytes=VMEM_LIMIT), name="dec_conv",
    )(bgc, glu, xbc, dt, sb, sc, sd, *prm)


SSM_GROUP = 8


def _dec_ssm_kernel(act_ref, dtv_ref, zg_ref, h_ref, alog_ref, dskip_ref, dnorm_ref, yd_ref, hn_ref):
    l128 = lax.broadcasted_iota(i32, (1, 128), 1)
    a_row = jnp.where(l128 < N_HEADS, -jnp.exp(alog_ref[...]), 0.0)
    r = lax.broadcasted_iota(i32, (128, 256), 0)
    lane = lax.broadcasted_iota(i32, (128, 256), 1)
    r8 = lax.broadcasted_iota(i32, (8, 128), 0)
    l256 = lax.broadcasted_iota(i32, (1, 256), 1)
    zrow = jnp.zeros((1, 128), f32)
    group = range(act_ref.shape[0])
    xs_, cm_, lt_, rmat_ = [], [], [], []
    for g in group:
        act = act_ref[g]
        xs, bm, cm = act[:, 0:256], act[:, 256:512], act[:, 512:768]
        dtv = dtv_ref[g]
        dec_x = _per_head_lanes(jnp.exp(dtv * a_row))
        xdt = xs * _per_head_lanes(dtv)
        lrows = jnp.where((r == 0) & (lane < 128), xdt, 0.0) + jnp.where((r == 1) & (lane >= 128), xdt, 0.0)
        lt_.append((lrows + jnp.where(r == 2, dec_x, 0.0)).T)
        b0 = jnp.concatenate([bm[:, 0:128], zrow], axis=1)
        b1 = jnp.concatenate([bm[:, 128:256], zrow], axis=1)
        rmat_.append(jnp.where(r == 0, b0, 0.0) + jnp.where(r == 1, b1, 0.0)
                     + jnp.where((r == 2) & (lane >= 128), 1.0, 0.0))
        xs_.append(xs)
        cm_.append(cm)
    res_ = [_dot(lt_[g], rmat_[g], HI) for g in group]
    hn_ = [h_ref[g] * res_[g][:, 128:256] + res_[g][:, 0:128] for g in group]
    for g in group:
        hn_ref[g] = hn_[g]
    crows_ = [jnp.where(r8 == 0, cm_[g][:, 0:128], 0.0) + jnp.where(r8 == 1, cm_[g][:, 128:256], 0.0) for g in group]
    y8_ = [_dot_nt(crows_[g], hn_[g], HI) for g in group]
    for g in group:
        y = jnp.where(l256 < 128, y8_[g][0:1, :], y8_[g][1:2, :])
        yd_ref[g] = _gated_rms(y, xs_[g], zg_ref[g], dskip_ref[...], dnorm_ref[...])


def _dec_ssm(act, dtv, zg, h, alog, dskip, dnorm):
    nb = act.shape[0]
    per = lambda r, w: pl.BlockSpec((SSM_GROUP, r, w), lambda b: (b, 0, 0))
    full = lambda a: pl.BlockSpec(a.shape, lambda b: (0,) * a.ndim)
    return pl.pallas_call(
        _dec_ssm_kernel,
        grid=(nb // SSM_GROUP,),
        in_specs=[per(1, D_XBC), per(1, 128), per(1, 256), per(256, 128), full(alog), full(dskip), full(dnorm)],
        out_specs=[per(1, 256), per(256, 128)],
        out_shape=[jax.ShapeDtypeStruct((nb, 1, 256), f32), jax.ShapeDtypeStruct((nb, 256, 128), f32)],
        compiler_params=_cparams(("arbitrary",)), name="dec_ssm",
    )(act, dtv, zg, h, alog, dskip, dnorm)


def _rope_tables(pos, head_dim):
    rot = head_dim // 4
    half = rot // 2
    inv = ROPE_THETA ** (-jnp.arange(half, dtype=f32) * 2.0 / rot)
    ang = pos.astype(f32)[:, None] * inv[None, :]
    cos, sin = jnp.cos(ang), jnp.sin(ang)
    n = pos.shape[0]
    pad = jnp.zeros((n, head_dim - rot), f32)
    c = jnp.concatenate([cos, cos, pad + 1.0], axis=1)
    sm = jnp.concatenate([-sin, jnp.zeros((n, half), f32), pad], axis=1)
    sp = jnp.concatenate([jnp.zeros((n, half), f32), sin, pad], axis=1)
    rep = 128 // head_dim
    return [jnp.tile(t, (1, rep)) for t in (c, sm, sp)]


def _pad_lanes(v, width=128):
    return jnp.pad(v.astype(f32), (0, width - v.shape[0]))[None, :]


def _layer_params(l, w_in, w_out, g_pre_mix, g_post_mix, g_pre_ffn, g_post_ffn, bconv_w, cconv_w, cconv_b, cln_g,
                  cln_b, dconv_w, dconv_b, dt_bias, a_log, d_skip, d_norm, ffn_gate, ffn_up, ffn_down):
    w = w_in[l]
    cuts = np.cumsum((0,) + IN_SIZES)
    col = lambda j: w[:, cuts[j]:cuts[j + 1]]
    zpad = lambda a: jnp.pad(a, ((0, 0), (0, 128 - a.shape[1])))
    row = lambda a: a[l][None, :].astype(f32)
    wih, wil = _split(jnp.concatenate([col(3), jnp.tile(col(4), (1, 128 // IDX_DIM)),
                                       zpad(jnp.concatenate([col(5), col(12)], axis=1))], axis=1))
    return dict(
        wm=jnp.concatenate([w[:, :768], w[:, cuts[6]:cuts[12]]], axis=1).astype(bf16), wih=wih, wil=wil,
        g_pre_mix=row(g_pre_mix), g_post_mix=row(g_post_mix), g_pre_ffn=row(g_pre_ffn), g_post_ffn=row(g_post_ffn),
        w_out=w_out[l].astype(bf16), wg=ffn_gate[l].astype(bf16), wu=ffn_up[l].astype(bf16), wd=ffn_down[l].astype(bf16),
        bw=jnp.pad(bconv_w[l], ((0, 8 - B_CONV), (0, 0))), cw=jnp.pad(cconv_w[l], ((0, 32 - C_CONV), (0, 0))),
        cb=row(cconv_b), lng=row(cln_g), lnb=row(cln_b),
        dw=jnp.pad(dconv_w[l], ((0, 8 - D_CONV), (0, 0))), db=row(dconv_b),
        dtb=_pad_lanes(dt_bias[l]), alog=_pad_lanes(a_log[l]),
        dskip=jnp.repeat(d_skip[l].astype(f32), HEAD_DIM)[None, :], dnorm=row(d_norm),
    )


def _finish(p, ya, yb, yc, yd, x, tm):
    return _outffn(ya, yb, yc, yd, x, p['w_out'], p['g_post_mix'], p['g_pre_ffn'], p['wg'], p['wu'], p['wd'],
                   p['g_post_ffn'], tm)


def kernel(x_prompt, x_sample, cache_k, cache_v, cache_kidx, page_table, state_bconv, state_cconv, state_dconv, state_ssm, w_in, w_out, g_pre_mix, g_post_mix, g_pre_ffn, g_post_ffn, bconv_w, cconv_w, cconv_b, cln_g, cln_b, dconv_w, dconv_b, dt_bias, a_log, d_skip, d_norm, ffn_gate, ffn_up, ffn_down):
    bsz, seq, _ = x_prompt.shape
    nb, t_dec, _ = x_sample.shape
    depth = w_in.shape[0]
    n_phys = cache_k.shape[1]
    n_pages = page_table.shape[1]
    past = n_pages * PAGE
    assert t_dec == 1 and seq % QB == 0 and seq % TC == 0 and past + t_dec > TOPK_MAX * 4
    assert nb % SSM_GROUP == 0 and nb % SCORE_GROUP == 0
    weights = (w_in, w_out, g_pre_mix, g_post_mix, g_pre_ffn, g_post_ffn, bconv_w, cconv_w, cconv_b, cln_g, cln_b,
               dconv_w, dconv_b, dt_bias, a_log, d_skip, d_norm, ffn_gate, ffn_up, ffn_down)
    pos_p = jnp.arange(seq, dtype=jnp.int32)
    pos_s = jnp.full((nb,), past, jnp.int32)
    tabs_p = _rope_tables(pos_p, HEAD_DIM) + _rope_tables(pos_p, IDX_DIM)
    tabs_s = _rope_tables(pos_s, HEAD_DIM) + _rope_tables(pos_s, IDX_DIM)
    ck = cache_k.transpose(0, 1, 3, 4, 2).reshape(depth * n_phys, 256, PAGE)
    cv = cache_v.transpose(0, 1, 3, 4, 2).reshape(depth * n_phys, 256, PAGE)
    cki = cache_kidx.transpose(0, 1, 3, 2).reshape(depth * n_phys, IDX_DIM, PAGE)
    pt = page_table.reshape(-1).astype(jnp.int32)
    lane_head = (jnp.arange(256) // HEAD_DIM)[None, None, :] == jnp.arange(8)[None, :, None]

    hp = x_prompt.reshape(bsz * seq, D_MODEL)
    hs = x_sample.reshape(nb, D_MODEL)
    outs_p, outs_s = [], []
    for l in range(depth):
        p = _layer_params(l, *weights)
        conv_prm = (p['bw'], p['cw'], p['cb'], p['lng'], p['lnb'], p['dw'], p['db'], p['dtb'])
        ssm_prm = (p['alog'], p['dskip'], p['dnorm'])

        q, k, kb, v, vt, qi, ki4, kcat, wi, dt, bgc, glu, zg, xbc = _inproj(
            hp, p['g_pre_mix'], p['wm'], p['wih'], p['wil'], tabs_p, 256)
        ya = _attn_prompt_t(q, qi, wi, kcat, kb, vt, bsz, seq)
        yb, yc, yd, nbp, ncp, ndp, ssm_p = _mix_prompt(bgc, glu, zg, xbc, dt, conv_prm + ssm_prm, bsz, seq)
        hp = _finish(p, ya, yb, yc, yd, hp, 512)
        outs_p.append((k.reshape(bsz, seq, N_HEADS, HEAD_DIM), v.reshape(bsz, seq, N_HEADS, HEAD_DIM),
                       ki4[:, :IDX_DIM].reshape(bsz, seq, IDX_DIM), nbp, ncp, ndp,
                       ssm_p.reshape(bsz, N_HEADS, HEAD_DIM, D_STATE)))

        q, k, kb, v, vt, qi, ki4, kcat, wi, dt, bgc, glu, zg, xbc = _inproj(
            hs, p['g_pre_mix'], p['wm'], p['wih'], p['wil'], tabs_s, nb)
        topk = min(TOPK_MAX, (past + t_dec) // 4)
        qi_s = qi.reshape(nb, N_IDX_HEADS, IDX_DIM)
        wi_t = jnp.broadcast_to((wi[:, :N_IDX_HEADS] * (IDX_DIM ** -0.5 * N_IDX_HEADS ** -0.5))[:, :, None],
                                (nb, N_IDX_HEADS, 128))
        sc = _dec_scores(pt, qi_s, wi_t, cki, n_pages, l * n_phys).reshape(nb, past)
        bias, bnew = _dec_select(sc, qi, ki4, wi, topk)
        qm = jnp.where(lane_head, q[:, None, :], 0.0)
        ya = _dec_attn(pt, qm, k.reshape(nb, 1, 256), v.reshape(nb, 1, 256), bias.reshape(nb, 1, past),
                       bnew.reshape(nb, 1, 128), ck, cv, n_pages, l * n_phys).reshape(nb, 256)
        yb, yc, act, dtv, nbs, ncs, nds = _dec_conv(
            bgc, glu, xbc, dt, state_bconv[l].reshape(nb, -1), state_cconv[l].reshape(nb, -1),
            state_dconv[l].reshape(nb, -1), conv_prm)
        yd, ssm_s = _dec_ssm(act.reshape(nb, 1, D_XBC), dtv.reshape(nb, 1, 128), zg.reshape(nb, 1, 256),
                             state_ssm[l].reshape(nb, 256, D_STATE), *ssm_prm)
        hs = _finish(p, ya, yb, yc, yd.reshape(nb, 256), hs, nb)
        outs_s.append((k.reshape(nb, 1, N_HEADS, HEAD_DIM), v.reshape(nb, 1, N_HEADS, HEAD_DIM),
                       ki4[:, :IDX_DIM].reshape(nb, 1, IDX_DIM), nbs.reshape(nb, 2, 256), ncs.reshape(nb, 30, 256),
                       nds.reshape(nb, 3, D_XBC), ssm_s.reshape(nb, N_HEADS, HEAD_DIM, D_STATE)))

    k_p, v_p, kidx_p, bconv_p, cconv_p, dconv_p, ssm_p = [jnp.stack(a) for a in zip(*outs_p)]
    k_s, v_s, kidx_s, bconv_s, cconv_s, dconv_s, ssm_s = [jnp.stack(a) for a in zip(*outs_s)]
    return (hp.reshape(bsz, seq, D_MODEL), hs.reshape(nb, t_dec, D_MODEL), k_p, v_p, kidx_p, k_s, v_s, kidx_s,
            bconv_p, bconv_s, cconv_p, cconv_s, dconv_p, dconv_s, ssm_p, ssm_s)
```
